```python
import math
import functools
import jax
import jax.numpy as jnp
from jax import lax
import numpy as np

D_MODEL = 1024
BATCH = 4
SEQ = 8192
DEPTH = 4

GRID_W = 64
CTX_LEN = 256
N_MIXERS = 2
N_HYENA = (DEPTH + N_MIXERS - 1) // N_MIXERS
N_MLSTM = DEPTH // N_MIXERS
N_SUB = 3
N_MOD = 3 * N_SUB
D_FF = 2816
NORM_EPS = 1e-6
POS_BASE = 10000.0
HY_ORDER = 2
HY_SHORT = 3
HY_EMB = 33
HY_BANDS = (HY_EMB - 1) // 2
HY_FILT_HIDDEN = 64
HY_DECAY_SLOW = -math.log(1e-2) / 1.5
HY_DECAY_FAST = -math.log(1e-2) / 0.3
ML_INNER = 2 * D_MODEL
ML_HEADS = 4
ML_HEAD_DIM = ML_INNER // ML_HEADS
ML_QKV_BLOCK = 4
ML_CONV = 3
ML_CHUNK = 64
ML_NORM_EPS = 1e-5
ML_FGATE_BIAS_LO = 3.0
ML_FGATE_BIAS_HI = 6.0

kernel_name = 'hyena_mlstm_macaron_dit'


def rms_norm(x, g):
    xf = x.astype(jnp.float32)
    y = xf * lax.rsqrt(jnp.mean(xf * xf, axis=-1, keepdims=True) + NORM_EPS)
    return (y * g.astype(jnp.float32)).astype(x.dtype)


def ada_chunks(cond, w, b):
    return jnp.split(jax.nn.silu(cond) @ w + b, N_MOD, axis=-1)


def ada_pre(h, g, shift, scale):
    return rms_norm(h, g) * (1.0 + scale) + shift


def swiglu(u, w_in, w_out):
    gate, val = jnp.split(u @ w_in, 2, axis=-1)
    return (jax.nn.silu(gate) * val) @ w_out


def conv_centred(x, w, b):
    width = w.shape[0]
    pad = width // 2
    n = x.shape[1]
    xp = jnp.pad(x, ((0, 0), (pad, pad), (0, 0)))
    return sum(xp[:, i:i + n] * w[i] for i in range(width)) + b


def pos_embed_2d(n_tokens, d, dtype):
    rows = n_tokens // GRID_W
    quarter = d // 4
    omega = 1.0 / (POS_BASE ** (jnp.arange(quarter, dtype=jnp.float32) / quarter))
    ang_r = jnp.arange(rows, dtype=jnp.float32)[:, None] * omega
    ang_c = jnp.arange(GRID_W, dtype=jnp.float32)[:, None] * omega
    row_emb = jnp.concatenate([jnp.sin(ang_r), jnp.cos(ang_r)], axis=-1)
    col_emb = jnp.concatenate([jnp.sin(ang_c), jnp.cos(ang_c)], axis=-1)
    emb = jnp.concatenate([
        jnp.broadcast_to(row_emb[:, None, :], (rows, GRID_W, d // 2)),
        jnp.broadcast_to(col_emb[None, :, :], (rows, GRID_W, d // 2))], axis=-1)
    return emb.reshape(rows * GRID_W, d).astype(dtype)


def hyena_filters(n, f_w1, f_b1, f_w2, f_b2, f_w3, f_b3, f_freq, log_decay):
    f32 = jnp.float32
    d = log_decay.shape[-1]
    t = jnp.arange(n, dtype=f32)
    t_norm = t / max(n - 1, 1)
    bands = jnp.linspace(1e-4, HY_BANDS - 1, HY_BANDS, dtype=f32)
    ang = ((2.0 * math.pi / n) * t)[:, None] * bands[None, :]
    feats = jnp.concatenate([t_norm[:, None], jnp.cos(ang), -jnp.sin(ang)], axis=-1)
    h = jnp.sin(f_freq[0].astype(f32) * (feats @ f_w1.astype(f32) + f_b1.astype(f32)))
    h = jnp.sin(f_freq[1].astype(f32) * (h @ f_w2.astype(f32) + f_b2.astype(f32)))
    h = (h @ f_w3.astype(f32) + f_b3.astype(f32)).reshape(n, HY_ORDER, 2, d)
    h = h.transpose(1, 2, 0, 3)
    window = jnp.exp(-t_norm[None, None, :, None] * jnp.exp(log_decay.astype(f32))[:, :, None, :])
    h = h * window
    fwd = h[:, 0]
    bwd = h[:, 1, 1:][:, ::-1]
    k = jnp.concatenate([fwd, jnp.zeros((HY_ORDER, 1, d), f32), bwd], axis=1)
    return k * lax.rsqrt(jnp.sum(k * k, axis=1, keepdims=True) + 1e-12)


def long_conv(z, k_freq, skip):
    n = z.shape[1]
    zf = jnp.fft.rfft(z, n=2 * n, axis=1)
    y = jnp.fft.irfft(zf * k_freq[None], n=2 * n, axis=1)[:, :n]
    return y + z * skip.astype(jnp.float32)


def hyena_mixer(u, w_in, b_in, conv_w, conv_b, f_w1, f_b1, f_w2, f_b2, f_w3, f_b3,
                f_freq, log_decay, skip, w_out, b_out):
    n = u.shape[1]
    proj = conv_centred(u @ w_in + b_in, conv_w, conv_b).astype(jnp.float32)
    x1, x2, v = jnp.split(proj, 3, axis=-1)
    k = hyena_filters(n, f_w1, f_b1, f_w2, f_b2, f_w3, f_b3, f_freq, log_decay)
    k_freq = jnp.fft.rfft(k, axis=1)
    z = x1 * long_conv(v, k_freq[0], skip[0])
    z = x2 * long_conv(z, k_freq[1], skip[1])
    return z.astype(u.dtype) @ w_out + b_out


def blockdiag(x, w):
    bsz, n, ch = x.shape
    nb, bs, _ = w.shape
    return jnp.einsum('blnd,nde->blne', x.reshape(bsz, n, nb, bs), w).reshape(bsz, n, ch)


def mlstm_scan(q, k, v, i_pre, f_pre):
    bsz, n, nh, dh = q.shape
    n_chunks = n // ML_CHUNK

    def chunks(a):
        a = a.reshape(bsz, n_chunks, ML_CHUNK, nh, *a.shape[3:])
        return jnp.moveaxis(a, (1, 3), (0, 2))

    log_f = jax.nn.log_sigmoid(f_pre)
    mask = jnp.tril(jnp.ones((ML_CHUNK, ML_CHUNK), dtype=bool))

    def step(carry, inp):
        c_st, n_st, m_st = carry
        qc, kc, vc, ic, lfc = inp
        b = jnp.cumsum(lfc, axis=-1)
        a = b + m_st[..., None]
        dmat = jnp.where(mask, b[..., :, None] - b[..., None, :] + ic[..., None, :], -jnp.inf)
        m_j = jnp.maximum(a, jnp.max(dmat, axis=-1))
        w_inter = jnp.exp(a - m_j)
        s = jnp.einsum('bhjd,bhsd->bhjs', qc, kc) * jnp.exp(dmat - m_j[..., None])
        num = w_inter[..., None] * jnp.einsum('bhjd,bhde->bhje', qc, c_st) + jnp.einsum('bhjs,bhse->bhje', s, vc)
        qn = w_inter * jnp.einsum('bhjd,bhd->bhj', qc, n_st) + jnp.sum(s, axis=-1)
        h = num / jnp.maximum(jnp.abs(qn), jnp.exp(-m_j))[..., None]
        b_last = b[..., -1]
        g = b_last[..., None] - b + ic
        m_new = jnp.maximum(b_last + m_st, jnp.max(g, axis=-1))
        decay = jnp.exp(b_last + m_st - m_new)
        kw = kc * jnp.exp(g - m_new[..., None])[..., None]
        c_new = decay[..., None, None] * c_st + jnp.einsum('bhsd,bhse->bhde', kw, vc)
        n_new = decay[..., None] * n_st + jnp.sum(kw, axis=-2)
        return (c_new, n_new, m_new), h

    f32 = jnp.float32
    init = (jnp.zeros((bsz, nh, dh, dh), f32), jnp.zeros((bsz, nh, dh), f32), jnp.zeros((bsz, nh), f32))
    _, h = lax.scan(step, init, (chunks(q), chunks(k), chunks(v), chunks(i_pre), chunks(log_f)))
    return jnp.moveaxis(h, (0, 2), (1, 3)).reshape(bsz, n, nh, dh)


def mlstm_mixer(u_ctx, u_lat, w_up, conv_w, conv_b, w_q, w_k, w_v, w_gate, b_gate,
                norm_g, skip, w_down, with_ctx_out):
    n_ctx = u_ctx.shape[1]

    def branch(u):
        xm, z = jnp.split(u @ w_up, 2, axis=-1)
        xc = jax.nn.silu(conv_centred(xm, conv_w, conv_b))
        q = blockdiag(xc, w_q)
        k = blockdiag(xc, w_k)
        v = blockdiag(xm, w_v)
        gates = jnp.concatenate([q, k, v], axis=-1) @ w_gate + b_gate
        return xc, z, q, k, v, gates

    xc, z, q, k, v, gates = [jnp.concatenate([pc, pl], axis=1)
                             for pc, pl in zip(branch(u_ctx), branch(u_lat))]
    bsz, n, _ = q.shape

    def heads(a):
        return a.astype(jnp.float32).reshape(bsz, n, ML_HEADS, ML_HEAD_DIM)

    q, k, v = heads(q), heads(k) * (ML_HEAD_DIM ** -0.5), heads(v)
    gates = gates.astype(jnp.float32).reshape(bsz, n, 2, 2, ML_HEADS)

    def rev(a):
        return jnp.concatenate([a[:, :n_ctx][:, ::-1], a[:, n_ctx:][:, ::-1]], axis=1)

    h_fwd = mlstm_scan(q, k, v, gates[:, :, 0, 0], gates[:, :, 0, 1])
    h_bwd = rev(mlstm_scan(rev(q), rev(k), rev(v), rev(gates[:, :, 1, 0]), rev(gates[:, :, 1, 1])))
    h = h_fwd + h_bwd
    mu = jnp.mean(h, axis=-1, keepdims=True)
    var = jnp.mean(jnp.square(h - mu), axis=-1, keepdims=True)
    h = ((h - mu) * lax.rsqrt(var + ML_NORM_EPS)).reshape(bsz, n, ML_INNER) * norm_g.astype(jnp.float32)
    h = (h.astype(u_lat.dtype) + skip * xc) * jax.nn.silu(z)
    y_lat = h[:, n_ctx:] @ w_down
    y_ctx = h[:, :n_ctx] @ w_down if with_ctx_out else None
    return y_ctx, y_lat


def setup_inputs(seed: int = 0) -> dict:
    key = jax.random.key(seed)
    ks = iter(jax.random.split(key, 48))
    f32 = jnp.float32

    def nrm(shape, scale):
        return jax.random.normal(next(ks), shape, f32) * scale

    d, ff, inner, nh, qb, fh = D_MODEL, D_FF, ML_INNER, ML_HEADS, ML_QKV_BLOCK, HY_FILT_HIDDEN
    i_bias = nrm((N_MLSTM, 2, 1, nh), 0.1)
    f_bias = jax.random.uniform(next(ks), (N_MLSTM, 2, 1, nh), f32, ML_FGATE_BIAS_LO, ML_FGATE_BIAS_HI)
    return {
        'x': nrm((BATCH, SEQ, d), 1.0),
        'c': nrm((BATCH, d), 1.0),
        'ctx': nrm((BATCH, CTX_LEN, d), 1.0),
        'c_ctx': nrm((d,), 1.0),
        'ada_w': nrm((DEPTH, d, N_MOD * d), 0.5 * d ** -0.5),
        'ada_b': nrm((DEPTH, N_MOD * d), 0.01),
        'norm_g': 1.0 + nrm((DEPTH, N_SUB, d), 0.02),
        'final_g': 1.0 + nrm((d,), 0.02),
        'ffn_w_in': nrm((DEPTH, 2, d, 2 * ff), d ** -0.5),
        'ffn_w_out': nrm((DEPTH, 2, ff, d), ff ** -0.5),
        'hy_w_in': nrm((N_HYENA, d, 3 * d), d ** -0.5),
        'hy_b_in': nrm((N_HYENA, 3 * d), 0.01),
        'hy_conv_w': nrm((N_HYENA, HY_SHORT, 3 * d), HY_SHORT ** -0.5),
        'hy_conv_b': nrm((N_HYENA, 3 * d), 0.01),
        'hy_f_w1': nrm((N_HYENA, HY_EMB, fh), HY_EMB ** -0.5),
        'hy_f_b1': nrm((N_HYENA, fh), 0.01),
        'hy_f_w2': nrm((N_HYENA, fh, fh), fh ** -0.5),
        'hy_f_b2': nrm((N_HYENA, fh), 0.01),
        'hy_f_w3': nrm((N_HYENA, fh, HY_ORDER * 2 * d), fh ** -0.5),
        'hy_f_b3': nrm((N_HYENA, HY_ORDER * 2 * d), 0.01),
        'hy_f_freq': 1.0 + nrm((N_HYENA, 2, fh), 0.02),
        'hy_log_decay': jax.random.uniform(next(ks), (N_HYENA, HY_ORDER, 2, d), f32,
                                           math.log(HY_DECAY_SLOW), math.log(HY_DECAY_FAST)),
        'hy_skip': nrm((N_HYENA, HY_ORDER, d), 0.5),
        'hy_w_out': nrm((N_HYENA, d, d), d ** -0.5),
        'hy_b_out': nrm((N_HYENA, d), 0.01),
        'ml_w_up': nrm((N_MLSTM, d, 2 * inner), d ** -0.5),
        'ml_conv_w': nrm((N_MLSTM, ML_CONV, inner), ML_CONV ** -0.5),
        'ml_conv_b': nrm((N_MLSTM, inner), 0.01),
        'ml_w_q': nrm((N_MLSTM, inner // qb, qb, qb), qb ** -0.5),
        'ml_w_k': nrm((N_MLSTM, inner // qb, qb, qb), qb ** -0.5),
        'ml_w_v': nrm((N_MLSTM, inner // qb, qb, qb), qb ** -0.5),
        'ml_w_gate': nrm((N_MLSTM, 3 * inner, 4 * nh), 0.1 * (3 * inner) ** -0.5),
        'ml_b_gate': jnp.concatenate([i_bias, f_bias], axis=2).reshape(N_MLSTM, 4 * nh),
        'ml_norm_g': 1.0 + nrm((N_MLSTM, inner), 0.02),
        'ml_skip': 1.0 + nrm((N_MLSTM, inner), 0.02),
        'ml_w_down': nrm((N_MLSTM, inner, d), inner ** -0.5),
    }


def reference(x, c, ctx, c_ctx, ada_w, ada_b, norm_g, final_g, ffn_w_in, ffn_w_out,
              hy_w_in, hy_b_in, hy_conv_w, hy_conv_b, hy_f_w1, hy_f_b1, hy_f_w2, hy_f_b2,
              hy_f_w3, hy_f_b3, hy_f_freq, hy_log_decay, hy_skip, hy_w_out, hy_b_out,
              ml_w_up, ml_conv_w, ml_conv_b, ml_w_q, ml_w_k, ml_w_v, ml_w_gate, ml_b_gate,
              ml_norm_g, ml_skip, ml_w_down):
    n_lat = x.shape[1]
    h_lat = x + pos_embed_2d(n_lat, x.shape[-1], x.dtype)[None]
    h_ctx = ctx
    for l in range(DEPTH):
        last = l == DEPTH - 1
        j = l // N_MIXERS
        m_lat = [m[:, None, :] for m in ada_chunks(c, ada_w[l], ada_b[l])]
        m_ctx = ada_chunks(c_ctx, ada_w[l], ada_b[l])
        h_lat = h_lat + 0.5 * m_lat[2] * swiglu(ada_pre(h_lat, norm_g[l, 0], m_lat[0], m_lat[1]),
                                                ffn_w_in[l, 0], ffn_w_out[l, 0])
        h_ctx = h_ctx + 0.5 * m_ctx[2] * swiglu(ada_pre(h_ctx, norm_g[l, 0], m_ctx[0], m_ctx[1]),
                                                ffn_w_in[l, 0], ffn_w_out[l, 0])
        u_lat = ada_pre(h_lat, norm_g[l, 1], m_lat[3], m_lat[4])
        u_ctx = ada_pre(h_ctx, norm_g[l, 1], m_ctx[3], m_ctx[4])
        if l % N_MIXERS == 0:
            hy = functools.partial(
                hyena_mixer, w_in=hy_w_in[j], b_in=hy_b_in[j], conv_w=hy_conv_w[j], conv_b=hy_conv_b[j],
                f_w1=hy_f_w1[j], f_b1=hy_f_b1[j], f_w2=hy_f_w2[j], f_b2=hy_f_b2[j], f_w3=hy_f_w3[j],
                f_b3=hy_f_b3[j], f_freq=hy_f_freq[j], log_decay=hy_log_decay[j], skip=hy_skip[j],
                w_out=hy_w_out[j], b_out=hy_b_out[j])
            y_lat = hy(u_lat)
            y_ctx = None if last else hy(u_ctx)
        else:
            y_ctx, y_lat = mlstm_mixer(u_ctx, u_lat, ml_w_up[j], ml_conv_w[j], ml_conv_b[j], ml_w_q[j],
                                       ml_w_k[j], ml_w_v[j], ml_w_gate[j], ml_b_gate[j], ml_norm_g[j],
                                       ml_skip[j], ml_w_down[j], with_ctx_out=not last)
        h_lat = h_lat + m_lat[5] * y_lat
        h_lat = h_lat + 0.5 * m_lat[8] * swiglu(ada_pre(h_lat, norm_g[l, 2], m_lat[6], m_lat[7]),
                                                ffn_w_in[l, 1], ffn_w_out[l, 1])
        if not last:
            h_ctx = h_ctx + m_ctx[5] * y_ctx
            h_ctx = h_ctx + 0.5 * m_ctx[8] * swiglu(ada_pre(h_ctx, norm_g[l, 2], m_ctx[6], m_ctx[7]),
                                                    ffn_w_in[l, 1], ffn_w_out[l, 1])
    return rms_norm(h_lat, final_g)
```

```python
import functools
import math

import jax
import jax.numpy as jnp
import numpy as np
from jax import lax
from jax.experimental import pallas as pl
from jax.experimental.pallas import tpu as pltpu

F32 = jnp.float32
BF16 = jnp.bfloat16

NORM_EPS = 1e-6
GRID_W = 64
POS_BASE = 10000.0
N_MOD = 9
HY_EMB = 33
HY_BANDS = (HY_EMB - 1) // 2
ML_HEADS = 4
ML_NORM_EPS = 1e-5
ML_QKV_GROUP = 256
SCAN_CHUNK = 256
HALO = 8

V7X_VMEM_LIMIT = 56 * 1024 * 1024


def _cparams(*sem):
    return pltpu.CompilerParams(dimension_semantics=sem, vmem_limit_bytes=V7X_VMEM_LIMIT)


def _bdot(a, b):
    return jnp.dot(a.astype(BF16), b.astype(BF16), preferred_element_type=F32)


def _rms_mod(h, g, shift, scale):
    y = h * lax.rsqrt(jnp.mean(h * h, axis=-1, keepdims=True) + NORM_EPS)
    return (y * g) * (1.0 + scale) + shift


def _silu(x):
    return x * (1.0 / (1.0 + jnp.exp(-x)))


def _tile(n, pref):
    if n <= pref:
        return n
    for t in range(pref, 7, -1):
        if n % t == 0 and t % 8 == 0:
            return t
    return n


def _mod_kernel(c_ref, w_ref, b_ref, o_ref):
    o_ref[...] = _bdot(_silu(c_ref[...]), w_ref[...]) + b_ref[...]


def ada_modulation(cond, ada_w, ada_b):
    n_layers, d, n_out = ada_w.shape
    r = cond.shape[0]
    tn = _tile(n_out, 2304) if n_out % 128 == 0 else n_out
    return pl.pallas_call(
        _mod_kernel,
        grid=(n_layers, n_out // tn),
        in_specs=[
            pl.BlockSpec((r, d), lambda l, j: (0, 0)),
            pl.BlockSpec((None, d, tn), lambda l, j: (l, 0, j)),
            pl.BlockSpec((None, 1, tn), lambda l, j: (l, 0, j)),
        ],
        out_specs=pl.BlockSpec((None, r, tn), lambda l, j: (l, 0, j)),
        out_shape=jax.ShapeDtypeStruct((n_layers, r, n_out), F32),
        compiler_params=_cparams("parallel", "parallel"),
        name="ada_modulation",
    )(cond, ada_w, ada_b.reshape(n_layers, 1, n_out))


def _ffn_kernel(*refs, has_pos, final):
    it = iter(refs)
    h_ref = next(it)
    pos_ref = next(it) if has_pos else None
    mod_ref, g_ref, wg_ref, wv_ref, wo_ref = next(it), next(it), next(it), next(it), next(it)
    fg_ref = next(it) if final else None
    o_ref, u_ref, acc_ref = next(it), next(it), next(it)
    hb_ref = next(it) if has_pos else None
    f = pl.program_id(2)

    @pl.when(f == 0)
    def _():
        h = h_ref[...]
        if has_pos:
            h = h + pos_ref[...]
            hb_ref[...] = h
        u_ref[...] = _rms_mod(h, g_ref[...], mod_ref[0:1, :], mod_ref[1:2, :]).astype(BF16)
        acc_ref[...] = jnp.zeros_like(acc_ref)

    u = u_ref[...]
    gate = jnp.dot(u, wg_ref[...], preferred_element_type=F32)
    val = jnp.dot(u, wv_ref[...], preferred_element_type=F32)
    acc_ref[...] += _bdot(_silu(gate) * val, wo_ref[...])

    @pl.when(f == pl.num_programs(2) - 1)
    def _():
        h = hb_ref[...] if has_pos else h_ref[...]
        out = h + (0.5 * mod_ref[2:3, :]) * acc_ref[...]
        if final:
            out = out * lax.rsqrt(jnp.mean(out * out, axis=-1, keepdims=True) + NORM_EPS) * fg_ref[...]
        o_ref[...] = out


def ffn_block(h, mods, g, w_in, w_out, *, pos=None, final_g=None, tm=512, tf=1408):
    bsz, n, d = h.shape
    ff = w_out.shape[0]
    tm = _tile(n, tm)
    tf = tf if ff % tf == 0 else ff
    nf = ff // tf
    has_pos, final = pos is not None, final_g is not None
    in_specs = [pl.BlockSpec((None, tm, d), lambda b, i, f: (b, i, 0))]
    args = [h]
    if has_pos:
        in_specs.append(pl.BlockSpec((tm, d), lambda b, i, f: (i, 0)))
        args.append(pos)
    in_specs += [
        pl.BlockSpec((None, 3, d), lambda b, i, f: (b, 0, 0)),
        pl.BlockSpec((1, d), lambda b, i, f: (0, 0)),
        pl.BlockSpec((d, tf), lambda b, i, f: (0, f)),
        pl.BlockSpec((d, tf), lambda b, i, f: (0, f + nf)),
        pl.BlockSpec((tf, d), lambda b, i, f: (f, 0)),
    ]
    args += [mods, g.reshape(1, d), w_in, w_in, w_out]
    if final:
        in_specs.append(pl.BlockSpec((1, d), lambda b, i, f: (0, 0)))
        args.append(final_g.reshape(1, d))
    scratch = [pltpu.VMEM((tm, d), BF16), pltpu.VMEM((tm, d), F32)]
    if has_pos:
        scratch.append(pltpu.VMEM((tm, d), F32))
    return pl.pallas_call(
        functools.partial(_ffn_kernel, has_pos=has_pos, final=final),
        grid=(bsz, n // tm, nf),
        in_specs=in_specs,
        out_specs=pl.BlockSpec((None, tm, d), lambda b, i, f: (b, i, 0)),
        out_shape=jax.ShapeDtypeStruct((bsz, n, d), F32),
        scratch_shapes=scratch,
        compiler_params=_cparams("parallel", "parallel", "arbitrary"),
        name="ffn_block",
    )(*args)


def _linres_kernel(y_ref, h_ref, gate_ref, w_ref, b_ref, o_ref):
    o_ref[...] = h_ref[...] + gate_ref[...] * (_bdot(y_ref[...], w_ref[...]) + b_ref[...])


def linear_residual(y, h, gate, w, b, *, tm=512):
    bsz, n, kdim = y.shape
    d = h.shape[-1]
    tm = _tile(n, tm)
    return pl.pallas_call(
        _linres_kernel,
        grid=(bsz, n // tm),
        in_specs=[
            pl.BlockSpec((None, tm, kdim), lambda b_, i: (b_, i, 0)),
            pl.BlockSpec((None, tm, d), lambda b_, i: (b_, i, 0)),
            pl.BlockSpec((None, 1, d), lambda b_, i: (b_, 0, 0)),
            pl.BlockSpec((kdim, d), lambda b_, i: (0, 0)),
            pl.BlockSpec((1, d), lambda b_, i: (0, 0)),
        ],
        out_specs=pl.BlockSpec((None, tm, d), lambda b_, i: (b_, i, 0)),
        out_shape=jax.ShapeDtypeStruct((bsz, n, d), F32),
        compiler_params=_cparams("parallel", "parallel"),
        name="linear_residual",
    )(y, h, gate, w, b.reshape(1, d))


def _halo_specs(n, tm, d):
    hb, last = tm // HALO, n // HALO - 1
    return [
        pl.BlockSpec((None, HALO, d), lambda b, i: (b, jnp.maximum(i * hb - 1, 0), 0)),
        pl.BlockSpec((None, tm, d), lambda b, i: (b, i, 0)),
        pl.BlockSpec((None, HALO, d), lambda b, i: (b, jnp.minimum((i + 1) * hb, last), 0)),
    ]


def _mask_seq_ends(p, tm):
    i, ni = pl.program_id(1), pl.num_programs(1)
    r = lax.broadcasted_iota(jnp.int32, p.shape, 0)
    lo = jnp.where(i == 0, HALO, 0)
    hi = jnp.where(i == ni - 1, tm + HALO, tm + 2 * HALO)
    return jnp.where(r < lo, 0.0, jnp.where(r >= hi, 0.0, p))


def _conv3(p, cw_ref, tm):
    rows = p.shape[0]
    prev = pltpu.roll(p, 1, 0)[HALO:HALO + tm]
    nxt = pltpu.roll(p, rows - 1, 0)[HALO:HALO + tm]
    return prev * cw_ref[0:1, :] + p[HALO:HALO + tm] * cw_ref[1:2, :] + nxt * cw_ref[2:3, :]


def _hyena_in_kernel(hp_ref, h_ref, hn_ref, mod_ref, g_ref, w_ref, b_ref, cw_ref, cb_ref, o_ref):
    tm = h_ref.shape[0]
    rows = jnp.concatenate([hp_ref[...], h_ref[...], hn_ref[...]], axis=0)
    u = _rms_mod(rows, g_ref[...], mod_ref[0:1, :], mod_ref[1:2, :])
    p = _mask_seq_ends(_bdot(u, w_ref[...]) + b_ref[...], tm)
    o_ref[...] = _conv3(p, cw_ref, tm) + cb_ref[...]


def hyena_in_proj(h, mods, g, w, b, conv_w, conv_b, *, tm=512):
    bsz, n, d = h.shape
    nout = w.shape[1]
    tm = _tile(n, tm)
    const = lambda shape: pl.BlockSpec(shape, lambda b_, i: (0,) * len(shape))
    return pl.pallas_call(
        _hyena_in_kernel,
        grid=(bsz, n // tm),
        in_specs=_halo_specs(n, tm, d) + [
            pl.BlockSpec((None, 3, d), lambda b_, i: (b_, 0, 0)),
            const((1, d)), const((d, nout)), const((1, nout)), const((3, nout)), const((1, nout)),
        ],
        out_specs=pl.BlockSpec((None, tm, nout), lambda b_, i: (b_, i, 0)),
        out_shape=jax.ShapeDtypeStruct((bsz, n, nout), F32),
        compiler_params=_cparams("parallel", "parallel"),
        name="hyena_in_proj",
    )(h, h, h, mods, g.reshape(1, d), w, b.reshape(1, nout), conv_w, conv_b.reshape(1, nout))


def _hdot(a, b):
    return jnp.dot(a, b, preferred_element_type=F32, precision=lax.Precision.HIGHEST)


def _filter_kernel(ft_ref, w1_ref, b1_ref, w2_ref, b2_ref, w3_ref, b3_ref, fr_ref, ld_ref, k_ref, ss_ref):
    i = pl.program_id(0)
    tt = ft_ref.shape[0]
    c = k_ref.shape[-1]
    ft = ft_ref[...]
    h = jnp.sin(fr_ref[0:1, :] * (_hdot(ft, w1_ref[...]) + b1_ref[...]))
    h = jnp.sin(fr_ref[1:2, :] * (_hdot(h, w2_ref[...]) + b2_ref[...]))
    k = (_hdot(h, w3_ref[...]) + b3_ref[...]) * jnp.exp(-ft[:, 0:1] * jnp.exp(ld_ref[...]))
    first = (lax.broadcasted_iota(jnp.int32, (tt, c), 0) + i * tt) == 0

    @pl.when(i == 0)
    def _():
        ss_ref[...] = jnp.zeros_like(ss_ref)

    for q in range(k_ref.shape[0]):
        kq = k[:, q * c:(q + 1) * c]
        if q % 2 == 1:
            kq = jnp.where(first, 0.0, kq)
        k_ref[q] = kq
        ss_ref[q] += jnp.broadcast_to(jnp.sum(kq * kq, axis=0, keepdims=True), ss_ref.shape[1:])


def hyena_filters_time(n, f_w1, f_b1, f_w2, f_b2, f_w3, f_b3, f_freq, log_decay, *, tt=512):
    nq, d = log_decay.shape[0] * log_decay.shape[1], log_decay.shape[-1]
    fh = f_w2.shape[0]
    emb_pad = 64
    t = np.arange(n, dtype=np.float32)
    t_norm = t / np.float32(max(n - 1, 1))
    bands = np.linspace(1e-4, HY_BANDS - 1, HY_BANDS, dtype=np.float32)
    ang = (np.float32(2.0 * math.pi / n) * t)[:, None] * bands[None, :]
    feats = np.zeros((n, emb_pad), np.float32)
    feats[:, :HY_EMB] = np.concatenate([t_norm[:, None], np.cos(ang), -np.sin(ang)], axis=-1)
    w1 = jnp.zeros((emb_pad, fh), F32).at[:HY_EMB].set(f_w1)
    tt = _tile(n, tt)
    const = lambda shape: pl.BlockSpec(shape, lambda i: (0,) * len(shape))
    return pl.pallas_call(
        _filter_kernel,
        grid=(n // tt,),
        in_specs=[
            pl.BlockSpec((tt, emb_pad), lambda i: (i, 0)),
            const((emb_pad, fh)), const((1, fh)), const((fh, fh)), const((1, fh)),
            const((fh, nq * d)), const((1, nq * d)), const((2, fh)), const((1, nq * d)),
        ],
        out_specs=[
            pl.BlockSpec((nq, tt, d), lambda i: (0, i, 0)),
            pl.BlockSpec((nq, 8, d), lambda i: (0, 0, 0)),
        ],
        out_shape=[jax.ShapeDtypeStruct((nq, n, d), F32), jax.ShapeDtypeStruct((nq, 8, d), F32)],
        compiler_params=_cparams("arbitrary"),
        name="hyena_filter_mlp",
    )(jnp.asarray(feats), w1, f_b1.reshape(1, fh), f_w2, f_b2.reshape(1, fh), f_w3,
      f_b3.reshape(1, nq * d), f_freq, log_decay.reshape(1, nq * d))


def _fft_dims(n):
    nb = 1 << int(math.floor(math.log2(math.sqrt(2 * n))))
    na = 2 * n // nb
    assert na * nb == 2 * n and na % 16 == 0 and nb % 8 == 0, (n, na, nb)
    return na, nb


@functools.lru_cache(maxsize=None)
def _fft_tables(n):
    na, nb = _fft_dims(n)
    na2, nn = na // 2, 2 * n
    ka = np.arange(na, dtype=np.int64)[None, :, None]
    a = np.arange(na2, dtype=np.int64)[None, None, :]
    b = np.arange(nb, dtype=np.int64)[:, None, None]
    ang = (2.0 * np.pi / nn) * ((ka * (a * nb + b)) % nn)
    mr, mi = np.cos(ang), -np.sin(ang)
    f1 = np.concatenate([np.concatenate([mr, -mi], 2), np.concatenate([mi, mr], 2)], 1)
    mrt, mit = np.swapaxes(mr, 1, 2) / nn, np.swapaxes(mi, 1, 2) / nn
    g1 = np.concatenate([np.concatenate([mrt, mit], 2), np.concatenate([-mit, mrt], 2)], 1)
    kb = np.arange(nb, dtype=np.int64)
    ang2 = (2.0 * np.pi / nb) * ((kb[:, None] * kb[None, :]) % nb)
    er, ei = np.cos(ang2), -np.sin(ang2)
    f2 = np.block([[er, -ei], [ei, er]])
    g2 = np.block([[er, ei], [-ei, er]])
    as_bf16 = lambda x: x.astype(np.float32).astype(BF16)
    return dict(f1=as_bf16(f1), f1_real=as_bf16(f1[:, :, :na2]), g1=as_bf16(g1), f2=as_bf16(f2), g2=as_bf16(g2))


def _fft_s1_kernel(x_ref, f_ref, o_ref):
    na = o_ref.shape[1]
    for j in range(x_ref.shape[2]):
        xs = [x_ref[r, :, j, :] for r in range(x_ref.shape[0])]
        xs = xs[0] if len(xs) == 1 else jnp.concatenate(xs, axis=0)
        res = jnp.dot(f_ref[j], xs.astype(BF16), preferred_element_type=F32)
        o_ref[0, :, j, :] = res[:na]
        o_ref[1, :, j, :] = res[na:]


def fft_stage1(x, n, c, c_off, table, rows_per_seq, *, jb=8, cb=1024):
    na, nb = _fft_dims(n)
    na2 = na // 2
    r = rows_per_seq
    p = x.shape[0] // r
    ctot = x.shape[-1]
    cb = min(cb, c)
    off = c_off // cb
    xv = x.reshape(x.shape[0], na2, nb, ctot)
    return pl.pallas_call(
        _fft_s1_kernel,
        grid=(nb // jb, p, c // cb),
        in_specs=[
            pl.BlockSpec((r, na2, jb, cb), lambda bb, q, ci: (q, 0, bb, ci + off)),
            pl.BlockSpec((jb, 2 * na, r * na2), lambda bb, q, ci: (bb, 0, 0)),
        ],
        out_specs=pl.BlockSpec((None, 2, na, jb, cb), lambda bb, q, ci: (q, 0, 0, bb, ci)),
        out_shape=jax.ShapeDtypeStruct((p, 2, na, nb, c), F32),
        compiler_params=_cparams("parallel", "parallel", "parallel"),
        name="fft_stage1",
    )(xv, table)


def _fft_s2_kernel(a_ref, k_ref, f2_ref, g2_ref, o_ref):
    nb = a_ref.shape[2]
    for k in range(a_ref.shape[1]):
        a = jnp.concatenate([a_ref[0, k], a_ref[1, k]], axis=0)
        x = jnp.dot(f2_ref[...], a.astype(BF16), preferred_element_type=F32)
        xr, xi = x[:nb], x[nb:]
        kr, ki = k_ref[0, k], k_ref[1, k]
        y = jnp.concatenate([xr * kr - xi * ki, xr * ki + xi * kr], axis=0)
        bv = jnp.dot(g2_ref[...], y.astype(BF16), preferred_element_type=F32)
        o_ref[0, k] = bv[:nb]
        o_ref[1, k] = bv[nb:]


def fft_stage2(a, kf, order, tables, *, ka=8, cb=512):
    p, _, na, nb, c = a.shape
    cb = min(cb, c)
    blk = (None, 2, ka, nb, cb)
    return pl.pallas_call(
        _fft_s2_kernel,
        grid=(na // ka, c // cb, p),
        in_specs=[
            pl.BlockSpec(blk, lambda kk, ci, q: (q, 0, kk, 0, ci)),
            pl.BlockSpec(blk, lambda kk, ci, q: (order, 0, kk, 0, ci)),
            pl.BlockSpec((2 * nb, 2 * nb), lambda kk, ci, q: (0, 0)),
            pl.BlockSpec((2 * nb, 2 * nb), lambda kk, ci, q: (0, 0)),
        ],
        out_specs=pl.BlockSpec(blk, lambda kk, ci, q: (q, 0, kk, 0, ci)),
        out_shape=jax.ShapeDtypeStruct(a.shape, F32),
        compiler_params=_cparams("parallel", "parallel", "parallel"),
        name="fft_stage2",
    )(a, kf, tables["f2"], tables["g2"])


def _fft_s2_filter_kernel(a_ref, ss_ref, f2_ref, o_ref):
    nb = a_ref.shape[3]
    for o in range(o_ref.shape[0]):
        scale = lax.rsqrt(ss_ref[2 * o, 0:1, :] + ss_ref[2 * o + 1, 0:1, :] + 1e-12)
        for k in range(a_ref.shape[2]):
            hf = jnp.dot(f2_ref[...], jnp.concatenate([a_ref[2 * o, 0, k], a_ref[2 * o, 1, k]], axis=0).astype(BF16),
                         preferred_element_type=F32)
            hb = jnp.dot(f2_ref[...], jnp.concatenate([a_ref[2 * o + 1, 0, k], a_ref[2 * o + 1, 1, k]], axis=0).astype(BF16),
                         preferred_element_type=F32)
            o_ref[o, 0, k] = (hf[:nb] + hb[:nb]) * scale
            o_ref[o, 1, k] = (hf[nb:] - hb[nb:]) * scale


def fft_filter_spectrum(a, ss, tables, *, ka=8, cb=512):
    nq, _, na, nb, c = a.shape
    cb = min(cb, c)
    return pl.pallas_call(
        _fft_s2_filter_kernel,
        grid=(na // ka, c // cb),
        in_specs=[
            pl.BlockSpec((nq, 2, ka, nb, cb), lambda kk, ci: (0, 0, kk, 0, ci)),
            pl.BlockSpec((nq, 8, cb), lambda kk, ci: (0, 0, ci)),
            pl.BlockSpec((2 * nb, 2 * nb), lambda kk, ci: (0, 0)),
        ],
        out_specs=pl.BlockSpec((nq // 2, 2, ka, nb, cb), lambda kk, ci: (0, 0, kk, 0, ci)),
        out_shape=jax.ShapeDtypeStruct((nq // 2, 2, na, nb, c), F32),
        compiler_params=_cparams("parallel", "parallel"),
        name="fft_filter_spectrum",
    )(a, ss, tables["f2"])


def _fft_s3_kernel(b_ref, g_ref, z_ref, m_ref, skip_ref, o_ref):
    na2 = z_ref.shape[1]
    skip = skip_ref[...]
    for j in range(z_ref.shape[2]):
        bs = jnp.concatenate([b_ref[0, :, j, :], b_ref[1, :, j, :]], axis=0)
        y = jnp.dot(g_ref[j], bs.astype(BF16), preferred_element_type=F32)
        for r in range(2):
            o_ref[r, :, j, :] = m_ref[r, :, j, :] * (y[r * na2:(r + 1) * na2] + skip * z_ref[r, :, j, :])


def fft_stage3(bsp, z, z_off, m, m_off, skip, n, tables, *, jb=8, cb=512):
    na, nb = _fft_dims(n)
    na2 = na // 2
    p, c = bsp.shape[0], bsp.shape[-1]
    cb = min(cb, c)
    zv = z.reshape(z.shape[0], na2, nb, z.shape[-1])
    mv = m.reshape(m.shape[0], na2, nb, m.shape[-1])
    zo, mo = z_off // cb, m_off // cb
    out = pl.pallas_call(
        _fft_s3_kernel,
        grid=(nb // jb, p, c // cb),
        in_specs=[
            pl.BlockSpec((None, 2, na, jb, cb), lambda bb, q, ci: (q, 0, 0, bb, ci)),
            pl.BlockSpec((jb, na, 2 * na), lambda bb, q, ci: (bb, 0, 0)),
            pl.BlockSpec((2, na2, jb, cb), lambda bb, q, ci: (q, 0, bb, ci + zo)),
            pl.BlockSpec((2, na2, jb, cb), lambda bb, q, ci: (q, 0, bb, ci + mo)),
            pl.BlockSpec((1, cb), lambda bb, q, ci: (0, ci)),
        ],
        out_specs=pl.BlockSpec((2, na2, jb, cb), lambda bb, q, ci: (q, 0, bb, ci)),
        out_shape=jax.ShapeDtypeStruct((2 * p, na2, nb, c), F32),
        compiler_params=_cparams("parallel", "parallel", "parallel"),
        name="fft_stage3",
    )(bsp, tables["g1"], zv, mv, skip.reshape(1, c))
    return out.reshape(2 * p, n, c)


def hyena_mixer_block(h, mods, g, prm):
    bsz, n, d = h.shape
    tables = _fft_tables(n)
    proj = hyena_in_proj(h, mods, g, prm["w_in"], prm["b_in"], prm["conv_w"], prm["conv_b"])
    kt, ss = hyena_filters_time(n, prm["f_w1"], prm["f_b1"], prm["f_w2"], prm["f_b2"], prm["f_w3"],
                                prm["f_b3"], prm["f_freq"], prm["log_decay"])
    kf = fft_filter_spectrum(fft_stage1(kt, n, d, 0, tables["f1_real"], 1), ss, tables)
    a = fft_stage1(proj, n, d, 2 * d, tables["f1"], 2)
    z = fft_stage3(fft_stage2(a, kf, 0, tables), proj, 2 * d, proj, 0, prm["skip"][0], n, tables)
    a = fft_stage1(z, n, d, 0, tables["f1"], 2)
    y = fft_stage3(fft_stage2(a, kf, 1, tables), z, 0, proj, d, prm["skip"][1], n, tables)
    return linear_residual(y, h, mods[:, 2:3, :], prm["w_out"], prm["b_out"])


def _mlstm_pre_kernel(hp_ref, h_ref, hn_ref, mod_ref, g_ref, wup_ref, cw_ref, cb_ref, wq_ref, wk_ref, wv_ref,
                      wg_ref, bg_ref, q_ref, k_ref, v_ref, xc_ref, z_ref, gt_ref, *, k_scale):
    tm = h_ref.shape[0]
    inner = q_ref.shape[-1]
    grp = wq_ref.shape[-1]
    rows = jnp.concatenate([hp_ref[...], h_ref[...], hn_ref[...]], axis=0)
    u = _rms_mod(rows, g_ref[...], mod_ref[0:1, :], mod_ref[1:2, :])
    up = _bdot(u, wup_ref[...])
    xm_h = _mask_seq_ends(up[:, :inner], tm)
    xm = xm_h[HALO:HALO + tm]
    z_ref[...] = up[HALO:HALO + tm, inner:]
    xc = _silu(_conv3(xm_h, cw_ref, tm) + cb_ref[...])
    xc_ref[...] = xc
    gates = jnp.zeros(gt_ref.shape, F32) + bg_ref[...]
    for gi in range(inner // grp):
        lo, hi = gi * grp, (gi + 1) * grp
        q = _bdot(xc[:, lo:hi], wq_ref[gi])
        k = _bdot(xc[:, lo:hi], wk_ref[gi])
        v = _bdot(xm[:, lo:hi], wv_ref[gi])
        q_ref[:, lo:hi] = q
        k_ref[:, lo:hi] = k * k_scale
        v_ref[:, lo:hi] = v
        gates += (_bdot(q, wg_ref[lo:hi, :]) + _bdot(k, wg_ref[inner + lo:inner + hi, :])
                  + _bdot(v, wg_ref[2 * inner + lo:2 * inner + hi, :]))
    gt_ref[...] = gates


def mlstm_pre(h, mods, g, prm, *, tm=256):
    bsz, n, d = h.shape
    inner = prm["w_up"].shape[1] // 2
    ngate = prm["w_gate"].shape[1]
    grp = prm["wq_dense"].shape[-1]
    tm = _tile(n, tm)
    const = lambda shape: pl.BlockSpec(shape, lambda b_, i: (0,) * len(shape))
    row_spec = lambda w: pl.BlockSpec((None, tm, w), lambda b_, i: (b_, i, 0))
    big = jax.ShapeDtypeStruct((bsz, n, inner), F32)
    return pl.pallas_call(
        functools.partial(_mlstm_pre_kernel, k_scale=float((inner // ML_HEADS) ** -0.5)),
        grid=(bsz, n // tm),
        in_specs=_halo_specs(n, tm, d) + [
            pl.BlockSpec((None, 3, d), lambda b_, i: (b_, 0, 0)),
            const((1, d)), const((d, 2 * inner)), const((3, inner)), const((1, inner)),
            const((inner // grp, grp, grp)), const((inner // grp, grp, grp)), const((inner // grp, grp, grp)),
            const((3 * inner, ngate)), const((1, ngate)),
        ],
        out_specs=[row_spec(inner)] * 5 + [row_spec(ngate)],
        out_shape=[big] * 5 + [jax.ShapeDtypeStruct((bsz, n, ngate), F32)],
        compiler_params=_cparams("parallel", "parallel"),
        name="mlstm_pre",
    )(h, h, h, mods, g.reshape(1, d), prm["w_up"], prm["conv_w"], prm["conv_b"].reshape(1, inner),
      prm["wq_dense"], prm["wk_dense"], prm["wv_dense"], prm["w_gate"], prm["b_gate"].reshape(1, ngate))


def _dense_blockdiag(w, grp):
    nb, bs, _ = w.shape
    per = grp // bs
    eye = jnp.eye(per, dtype=w.dtype)
    dense = jnp.einsum("gmde,mn->gmdne", w.reshape(nb // per, per, bs, bs), eye)
    return dense.reshape(nb // per, grp, grp).astype(BF16)


def _split3(x):
    p1 = x.astype(BF16)
    r1 = x - p1.astype(F32)
    p2 = r1.astype(BF16)
    p3 = (r1 - p2.astype(F32)).astype(BF16)
    return p1, p2, p3


def _nt(a, b):
    return lax.dot_general(a, b, (((1,), (1,)), ((), ())), preferred_element_type=F32)


def _scan_kernel(*refs, has_init, emit_state, aug):
    it = iter(refs)
    q_ref, k_ref, v_ref, gt_ref = next(it), next(it), next(it), next(it)
    c0_ref, m0_ref = (next(it), next(it)) if has_init else (None, None)
    h_ref = next(it)
    cf_ref, mf_ref = (next(it), next(it)) if emit_state else (None, None)
    c_sc, m_sc = next(it), next(it)
    d, c = pl.program_id(0), pl.program_id(3)
    ch, dh = q_ref.shape

    @pl.when(c == 0)
    def _():
        if has_init:
            c_sc[...] = c0_ref[...]
            m_sc[...] = m0_ref[...]
        else:
            c_sc[...] = jnp.zeros_like(c_sc)
            m_sc[...] = jnp.zeros_like(m_sc)

    row = lax.broadcasted_iota(jnp.int32, (ch, ch), 0)
    col = lax.broadcasted_iota(jnp.int32, (ch, ch), 1)
    fwd = d == 0
    allowed = jnp.where(fwd, (col <= row).astype(F32), (col >= row).astype(F32))
    allowed_t = jnp.where(fwd, (row <= col).astype(F32), (row >= col).astype(F32))
    eye = (row == col).astype(F32)

    i_row = gt_ref[0:1, :]
    f_row = gt_ref[1:2, :]
    lf_row = jnp.minimum(f_row, 0.0) - jnp.log1p(jnp.exp(-jnp.abs(f_row)))
    pieces = jnp.concatenate(list(_split3(lf_row)) + list(_split3(i_row)) + [jnp.zeros((2, ch), BF16)], axis=0)
    cum_col = _nt(allowed.astype(BF16), pieces)
    tr_col = _nt(eye.astype(BF16), pieces)
    cum_row = jnp.dot(pieces, allowed_t.astype(BF16), preferred_element_type=F32)
    b_col = cum_col[:, 0:1] + cum_col[:, 1:2] + cum_col[:, 2:3]
    i_col = tr_col[:, 3:4] + tr_col[:, 4:5] + tr_col[:, 5:6]
    b_row = cum_row[0:1, :] + cum_row[1:2, :] + cum_row[2:3, :]
    total = jnp.sum(lf_row, axis=1, keepdims=True)
    e_row, e_col = i_row - b_row, i_col - b_col

    m_st = m_sc[0:1, 0:1]
    dmat = jnp.where(allowed > 0.5, b_col + e_row, -jnp.inf)
    a_col = b_col + m_st
    m_j = jnp.maximum(a_col, jnp.max(dmat, axis=1, keepdims=True))
    w_inter = jnp.exp(a_col - m_j)

    q = q_ref[...].astype(BF16)
    k = k_ref[...]
    ones_col = (lax.broadcasted_iota(jnp.int32, (ch, aug), 1) == 0).astype(BF16)
    v_aug = jnp.concatenate([v_ref[...].astype(BF16), ones_col], axis=1)
    s = _nt(q, k.astype(BF16)) * jnp.exp(dmat - m_j)
    c_st = c_sc[...]
    num = w_inter * jnp.dot(q, c_st.astype(BF16), preferred_element_type=F32) + _bdot(s, v_aug)
    qn = num[:, dh:dh + 1]
    h_ref[...] = num[:, :dh] / jnp.maximum(jnp.abs(qn), jnp.exp(-m_j))

    m_new = jnp.maximum(total + m_st, jnp.max(total + e_row, axis=1, keepdims=True))
    decay = jnp.exp(total + m_st - m_new)
    kw = (k * jnp.exp(total + e_col - m_new)).astype(BF16)
    c_new = decay * c_st + lax.dot_general(kw, v_aug, (((0,), (0,)), ((), ())), preferred_element_type=F32)
    c_sc[...] = c_new
    m_sc[...] = jnp.broadcast_to(m_new, m_sc.shape)

    if emit_state:
        @pl.when(c == pl.num_programs(3) - 1)
        def _():
            cf_ref[...] = c_new
            mf_ref[...] = jnp.broadcast_to(m_new, mf_ref.shape)


def mlstm_scan(q, k, v, gates_t, init=None, *, emit_state=False):
    bsz, n, inner = q.shape
    nh = gates_t.shape[2]
    dh = inner // nh
    aug = 128
    ch = min(SCAN_CHUNK, n)
    nc = n // ch
    ceff = lambda d, c: c + d * (nc - 1 - 2 * c)
    seq = pl.BlockSpec((None, ch, dh), lambda d, b, hd, c: (b, ceff(d, c), hd))
    st_c = pl.BlockSpec((None, None, None, dh, dh + aug), lambda d, b, hd, c: (d, b, hd, 0, 0))
    st_m = pl.BlockSpec((None, None, None, 8, 128), lambda d, b, hd, c: (d, b, hd, 0, 0))
    in_specs = [seq, seq, seq, pl.BlockSpec((None, None, None, 2, ch), lambda d, b, hd, c: (b, d, hd, 0, ceff(d, c)))]
    args = [q, k, v, gates_t]
    if init is not None:
        in_specs += [st_c, st_m]
        args += list(init)
    out_specs = [pl.BlockSpec((None, None, ch, dh), lambda d, b, hd, c: (d, b, ceff(d, c), hd))]
    out_shape = [jax.ShapeDtypeStruct((2, bsz, n, inner), F32)]
    if emit_state:
        out_specs += [st_c, st_m]
        out_shape += [jax.ShapeDtypeStruct((2, bsz, nh, dh, dh + aug), F32),
                      jax.ShapeDtypeStruct((2, bsz, nh, 8, 128), F32)]
    res = pl.pallas_call(
        functools.partial(_scan_kernel, has_init=init is not None, emit_state=emit_state, aug=aug),
        grid=(2, bsz, nh, nc),
        in_specs=in_specs,
        out_specs=out_specs,
        out_shape=out_shape,
        scratch_shapes=[pltpu.VMEM((dh, dh + aug), F32), pltpu.VMEM((8, 128), F32)],
        compiler_params=_cparams("parallel", "parallel", "parallel", "arbitrary"),
        name="mlstm_scan",
    )(*args)
    return (res[0], (res[1], res[2])) if emit_state else (res[0], None)


def _mlstm_post_kernel(hf_ref, hb_ref, xc_ref, z_ref, res_ref, gate_ref, ng_ref, sk_ref, wd_ref, o_ref):
    inner = xc_ref.shape[-1]
    dh = inner // ML_HEADS
    acc = jnp.zeros(o_ref.shape, F32)
    for hd in range(ML_HEADS):
        lo, hi = hd * dh, (hd + 1) * dh
        x = hf_ref[:, lo:hi] + hb_ref[:, lo:hi]
        xz = x - jnp.mean(x, axis=-1, keepdims=True)
        var = jnp.mean(xz * xz, axis=-1, keepdims=True)
        hn = xz * lax.rsqrt(var + ML_NORM_EPS) * ng_ref[:, lo:hi]
        t = (hn + sk_ref[:, lo:hi] * xc_ref[:, lo:hi]) * _silu(z_ref[:, lo:hi])
        acc += _bdot(t, wd_ref[lo:hi, :])
    o_ref[...] = res_ref[...] + gate_ref[...] * acc


def mlstm_post(hs, xc, z, h_res, gate, prm, *, tm=256):
    bsz, n, inner = xc.shape
    d = h_res.shape[-1]
    tm = _tile(n, tm)
    const = lambda shape: pl.BlockSpec(shape, lambda b_, i: (0,) * len(shape))
    return pl.pallas_call(
        _mlstm_post_kernel,
        grid=(bsz, n // tm),
        in_specs=[
            pl.BlockSpec((None, None, tm, inner), lambda b_, i: (0, b_, i, 0)),
            pl.BlockSpec((None, None, tm, inner), lambda b_, i: (1, b_, i, 0)),
            pl.BlockSpec((None, tm, inner), lambda b_, i: (b_, i, 0)),
            pl.BlockSpec((None, tm, inner), lambda b_, i: (b_, i, 0)),
            pl.BlockSpec((None, tm, d), lambda b_, i: (b_, i, 0)),
            pl.BlockSpec((None, 1, d), lambda b_, i: (b_, 0, 0)),
            const((1, inner)), const((1, inner)), const((inner, d)),
        ],
        out_specs=pl.BlockSpec((None, tm, d), lambda b_, i: (b_, i, 0)),
        out_shape=jax.ShapeDtypeStruct((bsz, n, d), F32),
        compiler_params=_cparams("parallel", "parallel"),
        name="mlstm_post",
    )(hs, hs, xc, z, h_res, gate, prm["norm_g"].reshape(1, inner), prm["skip"].reshape(1, inner), prm["w_down"])


def _gates_by_head(gates, nh):
    bsz, n, _ = gates.shape
    return jnp.transpose(gates.reshape(bsz, n, 2, 2, nh), (0, 2, 4, 3, 1))


def mlstm_mixer_block(h_ctx, h_lat, mods_ctx, mods_lat, g, prm, *, with_ctx_out):
    pc = mlstm_pre(h_ctx, mods_ctx, g, prm)
    pq = mlstm_pre(h_lat, mods_lat, g, prm)
    hs_ctx, state = mlstm_scan(pc[0], pc[1], pc[2], _gates_by_head(pc[5], ML_HEADS), emit_state=True)
    hs_lat, _ = mlstm_scan(pq[0], pq[1], pq[2], _gates_by_head(pq[5], ML_HEADS), init=state)
    out_lat = mlstm_post(hs_lat, pq[3], pq[4], h_lat, mods_lat[:, 2:3, :], prm)
    out_ctx = mlstm_post(hs_ctx, pc[3], pc[4], h_ctx, mods_ctx[:, 2:3, :], prm) if with_ctx_out else None
    return out_ctx, out_lat


def _pos_embed_2d(n_tokens, d):
    rows = n_tokens // GRID_W
    quarter = d // 4
    omega = 1.0 / (POS_BASE ** (jnp.arange(quarter, dtype=F32) / quarter))
    ang_r = jnp.arange(rows, dtype=F32)[:, None] * omega
    ang_c = jnp.arange(GRID_W, dtype=F32)[:, None] * omega
    row_emb = jnp.concatenate([jnp.sin(ang_r), jnp.cos(ang_r)], axis=-1)
    col_emb = jnp.concatenate([jnp.sin(ang_c), jnp.cos(ang_c)], axis=-1)
    emb = jnp.concatenate([
        jnp.broadcast_to(row_emb[:, None, :], (rows, GRID_W, d // 2)),
        jnp.broadcast_to(col_emb[None, :, :], (rows, GRID_W, d // 2))], axis=-1)
    return emb.reshape(rows * GRID_W, d)


def kernel(x, c, ctx, c_ctx, ada_w, ada_b, norm_g, final_g, ffn_w_in, ffn_w_out, hy_w_in, hy_b_in, hy_conv_w, hy_conv_b, hy_f_w1, hy_f_b1, hy_f_w2, hy_f_b2, hy_f_w3, hy_f_b3, hy_f_freq, hy_log_decay, hy_skip, hy_w_out, hy_b_out, ml_w_up, ml_conv_w, ml_conv_b, ml_w_q, ml_w_k, ml_w_v, ml_w_gate, ml_b_gate, ml_norm_g, ml_skip, ml_w_down):
    bsz, n_lat, d = x.shape
    n_ctx = ctx.shape[1]
    depth = ada_w.shape[0]
    n_mixers = 2

    cond_rows = 8
    cond = jnp.zeros((cond_rows, d), F32).at[:bsz].set(c).at[bsz].set(c_ctx)
    mods_all = ada_modulation(cond, ada_w, ada_b).reshape(depth, cond_rows, N_MOD, d)

    ffn_w_in = ffn_w_in.astype(BF16)
    ffn_w_out = ffn_w_out.astype(BF16)
    pos = _pos_embed_2d(n_lat, d)

    h_lat = x
    h_ctx = ctx.reshape(1, bsz * n_ctx, d)
    for l in range(depth):
        last = l == depth - 1
        j = l // n_mixers
        m_lat = mods_all[l, :bsz]
        m_ctx = mods_all[l, bsz:bsz + 1]
        m_ctx_b = jnp.broadcast_to(m_ctx, (bsz, N_MOD, d))

        h_lat = ffn_block(h_lat, m_lat[:, 0:3], norm_g[l, 0], ffn_w_in[l, 0], ffn_w_out[l, 0],
                          pos=pos if l == 0 else None)
        h_ctx = ffn_block(h_ctx, m_ctx[:, 0:3], norm_g[l, 0], ffn_w_in[l, 0], ffn_w_out[l, 0])

        h_ctx = h_ctx.reshape(bsz, n_ctx, d)
        if l % n_mixers == 0:
            prm = dict(w_in=hy_w_in[j].astype(BF16), b_in=hy_b_in[j], conv_w=hy_conv_w[j], conv_b=hy_conv_b[j],
                       f_w1=hy_f_w1[j], f_b1=hy_f_b1[j], f_w2=hy_f_w2[j], f_b2=hy_f_b2[j], f_w3=hy_f_w3[j],
                       f_b3=hy_f_b3[j], f_freq=hy_f_freq[j], log_decay=hy_log_decay[j], skip=hy_skip[j],
                       w_out=hy_w_out[j].astype(BF16), b_out=hy_b_out[j])
            h_lat = hyena_mixer_block(h_lat, m_lat[:, 3:6], norm_g[l, 1], prm)
            if not last:
                h_ctx = hyena_mixer_block(h_ctx, m_ctx_b[:, 3:6], norm_g[l, 1], prm)
        else:
            prm = dict(w_up=ml_w_up[j].astype(BF16), conv_w=ml_conv_w[j], conv_b=ml_conv_b[j],
                       wq_dense=_dense_blockdiag(ml_w_q[j], ML_QKV_GROUP),
                       wk_dense=_dense_blockdiag(ml_w_k[j], ML_QKV_GROUP),
                       wv_dense=_dense_blockdiag(ml_w_v[j], ML_QKV_GROUP),
                       w_gate=ml_w_gate[j].astype(BF16), b_gate=ml_b_gate[j], norm_g=ml_norm_g[j],
                       skip=ml_skip[j], w_down=ml_w_down[j].astype(BF16))
            new_ctx, h_lat = mlstm_mixer_block(h_ctx, h_lat, m_ctx_b[:, 3:6], m_lat[:, 3:6], norm_g[l, 1], prm,
                                               with_ctx_out=not last)
            h_ctx = h_ctx if last else new_ctx
        h_ctx = h_ctx.reshape(1, bsz * n_ctx, d)

        h_lat = ffn_block(h_lat, m_lat[:, 6:9], norm_g[l, 2], ffn_w_in[l, 1], ffn_w_out[l, 1],
                          final_g=final_g if last else None)
        if not last:
            h_ctx = ffn_block(h_ctx, m_ctx[:, 6:9], norm_g[l, 2], ffn_w_in[l, 1], ffn_w_out[l, 1])
    return h_lat
```

```python
import functools
import math

import jax
import jax.numpy as jnp
import numpy as np
from jax import lax
from jax.experimental import pallas as pl
from jax.experimental.pallas import tpu as pltpu

F32 = jnp.float32
BF16 = jnp.bfloat16

NORM_EPS = 1e-6
GRID_W = 64
POS_BASE = 10000.0
N_MOD = 9
HY_EMB = 33
HY_BANDS = (HY_EMB - 1) // 2
ML_HEADS = 4
ML_NORM_EPS = 1e-5
ML_QKV_GROUP = 256
SCAN_CHUNK = 256
HALO = 8

V7X_VMEM_LIMIT = 56 * 1024 * 1024


def _cparams(*sem):
    return pltpu.CompilerParams(dimension_semantics=sem, vmem_limit_bytes=V7X_VMEM_LIMIT)


def _bdot(a, b):
    return jnp.dot(a.astype(BF16), b.astype(BF16), preferred_element_type=F32)


def _rms_mod(h, g, shift, scale):
    y = h * lax.rsqrt(jnp.mean(h * h, axis=-1, keepdims=True) + NORM_EPS)
    return (y * g) * (1.0 + scale) + shift


def _silu(x):
    return x * (1.0 / (1.0 + jnp.exp(-x)))


def _tile(n, pref):
    if n <= pref:
        return n
    for t in range(pref, 7, -1):
        if n % t == 0 and t % 8 == 0:
            return t
    return n


def _mod_kernel(c_ref, w_ref, b_ref, o_ref):
    o_ref[...] = _bdot(_silu(c_ref[...]), w_ref[...]) + b_ref[...]


def ada_modulation(cond, ada_w, ada_b):
    n_layers, d, n_out = ada_w.shape
    r = cond.shape[0]
    tn = _tile(n_out, 2304) if n_out % 128 == 0 else n_out
    return pl.pallas_call(
        _mod_kernel,
        grid=(n_layers, n_out // tn),
        in_specs=[
            pl.BlockSpec((r, d), lambda l, j: (0, 0)),
            pl.BlockSpec((None, d, tn), lambda l, j: (l, 0, j)),
            pl.BlockSpec((None, 1, tn), lambda l, j: (l, 0, j)),
        ],
        out_specs=pl.BlockSpec((None, r, tn), lambda l, j: (l, 0, j)),
        out_shape=jax.ShapeDtypeStruct((n_layers, r, n_out), F32),
        compiler_params=_cparams("parallel", "parallel"),
        name="ada_modulation",
    )(cond, ada_w, ada_b.reshape(n_layers, 1, n_out))


def _ffn_kernel(*refs, has_pos, final):
    it = iter(refs)
    h_ref = next(it)
    pos_ref = next(it) if has_pos else None
    mod_ref, g_ref, wg_ref, wv_ref, wo_ref = next(it), next(it), next(it), next(it), next(it)
    fg_ref = next(it) if final else None
    o_ref = next(it)
    h = h_ref[...]
    if has_pos:
        h = h + pos_ref[...]
    u = _rms_mod(h, g_ref[...], mod_ref[0:1, :], mod_ref[1:2, :]).astype(BF16)
    gate = jnp.dot(u, wg_ref[...], preferred_element_type=F32)
    val = jnp.dot(u, wv_ref[...], preferred_element_type=F32)
    out = h + (0.5 * mod_ref[2:3, :]) * _bdot(_silu(gate) * val, wo_ref[...])
    if final:
        out = out * lax.rsqrt(jnp.mean(out * out, axis=-1, keepdims=True) + NORM_EPS) * fg_ref[...]
    o_ref[...] = out


def _resident(shape, index_map):
    return pl.BlockSpec(shape, index_map, pipeline_mode=pl.Buffered(1))


def ffn_block(h, mods, g, w_in, w_out, *, widx=(), pos=None, final_g=None, tm=512):
    bsz, n, d = h.shape
    ff = w_out.shape[-2]
    tm = _tile(n, tm)
    has_pos, final = pos is not None, final_g is not None
    lead = (None,) * len(widx)
    in_specs = [pl.BlockSpec((None, tm, d), lambda b, i: (b, i, 0))]
    args = [h]
    if has_pos:
        in_specs.append(pl.BlockSpec((tm, d), lambda b, i: (i, 0)))
        args.append(pos)
    in_specs += [
        pl.BlockSpec((None, 3, d), lambda b, i: (b, 0, 0)),
        pl.BlockSpec((1, d), lambda b, i: (0, 0)),
        _resident(lead + (d, ff), lambda b, i: widx + (0, 0)),
        _resident(lead + (d, ff), lambda b, i: widx + (0, 1)),
        _resident(lead + (ff, d), lambda b, i: widx + (0, 0)),
    ]
    args += [mods, g.reshape(1, d), w_in, w_in, w_out]
    if final:
        in_specs.append(pl.BlockSpec((1, d), lambda b, i: (0, 0)))
        args.append(final_g.reshape(1, d))
    return pl.pallas_call(
        functools.partial(_ffn_kernel, has_pos=has_pos, final=final),
        grid=(bsz, n // tm),
        in_specs=in_specs,
        out_specs=pl.BlockSpec((None, tm, d), lambda b, i: (b, i, 0)),
        out_shape=jax.ShapeDtypeStruct((bsz, n, d), F32),
        compiler_params=_cparams("parallel", "parallel"),
        name="ffn_block",
    )(*args)


def _linres_kernel(y_ref, h_ref, gate_ref, w_ref, b_ref, o_ref):
    o_ref[...] = h_ref[...] + gate_ref[...] * (_bdot(y_ref[...], w_ref[...]) + b_ref[...])


def linear_residual(y, h, gate, w, b, *, tm=512):
    bsz, n, kdim = y.shape
    d = h.shape[-1]
    tm = _tile(n, tm)
    return pl.pallas_call(
        _linres_kernel,
        grid=(bsz, n // tm),
        in_specs=[
            pl.BlockSpec((None, tm, kdim), lambda b_, i: (b_, i, 0)),
            pl.BlockSpec((None, tm, d), lambda b_, i: (b_, i, 0)),
            pl.BlockSpec((None, 1, d), lambda b_, i: (b_, 0, 0)),
            _resident((kdim, d), lambda b_, i: (0, 0)),
            pl.BlockSpec((1, d), lambda b_, i: (0, 0)),
        ],
        out_specs=pl.BlockSpec((None, tm, d), lambda b_, i: (b_, i, 0)),
        out_shape=jax.ShapeDtypeStruct((bsz, n, d), F32),
        compiler_params=_cparams("parallel", "parallel"),
        name="linear_residual",
    )(y, h, gate, w, b.reshape(1, d))


def _halo_specs(n, tm, d):
    hb, last = tm // HALO, n // HALO - 1
    return [
        pl.BlockSpec((None, HALO, d), lambda b, i: (b, jnp.maximum(i * hb - 1, 0), 0)),
        pl.BlockSpec((None, tm, d), lambda b, i: (b, i, 0)),
        pl.BlockSpec((None, HALO, d), lambda b, i: (b, jnp.minimum((i + 1) * hb, last), 0)),
    ]


def _mask_seq_ends(p, tm):
    i, ni = pl.program_id(1), pl.num_programs(1)
    r = lax.broadcasted_iota(jnp.int32, p.shape, 0)
    lo = jnp.where(i == 0, HALO, 0)
    hi = jnp.where(i == ni - 1, tm + HALO, tm + 2 * HALO)
    return jnp.where(r < lo, 0.0, jnp.where(r >= hi, 0.0, p))


def _conv3(p, cw_ref, tm):
    rows = p.shape[0]
    prev = pltpu.roll(p, 1, 0)[HALO:HALO + tm]
    nxt = pltpu.roll(p, rows - 1, 0)[HALO:HALO + tm]
    return prev * cw_ref[0:1, :] + p[HALO:HALO + tm] * cw_ref[1:2, :] + nxt * cw_ref[2:3, :]


def _hyena_in_kernel(hp_ref, h_ref, hn_ref, mod_ref, g_ref, w_ref, b_ref, cw_ref, cb_ref, o_ref):
    tm = h_ref.shape[0]
    rows = jnp.concatenate([hp_ref[...], h_ref[...], hn_ref[...]], axis=0)
    u = _rms_mod(rows, g_ref[...], mod_ref[0:1, :], mod_ref[1:2, :])
    p = _mask_seq_ends(_bdot(u, w_ref[...]) + b_ref[...], tm)
    o_ref[...] = _conv3(p, cw_ref, tm) + cb_ref[...]


def hyena_in_proj(h, mods, g, w, b, conv_w, conv_b, *, tm=512):
    bsz, n, d = h.shape
    nout = w.shape[1]
    tm = _tile(n, tm)
    const = lambda shape: pl.BlockSpec(shape, lambda b_, i: (0,) * len(shape))
    return pl.pallas_call(
        _hyena_in_kernel,
        grid=(bsz, n // tm),
        in_specs=_halo_specs(n, tm, d) + [
            pl.BlockSpec((None, 3, d), lambda b_, i: (b_, 0, 0)),
            const((1, d)), _resident((d, nout), lambda b_, i: (0, 0)), const((1, nout)), const((3, nout)),
            const((1, nout)),
        ],
        out_specs=pl.BlockSpec((None, tm, nout), lambda b_, i: (b_, i, 0)),
        out_shape=jax.ShapeDtypeStruct((bsz, n, nout), F32),
        compiler_params=_cparams("parallel", "parallel"),
        name="hyena_in_proj",
    )(h, h, h, mods, g.reshape(1, d), w, b.reshape(1, nout), conv_w, conv_b.reshape(1, nout))


def _hdot(a, b):
    return jnp.dot(a, b, preferred_element_type=F32, precision=lax.Precision.HIGHEST)


def _filter_kernel(ft_ref, w1_ref, b1_ref, w2_ref, b2_ref, w3_ref, b3_ref, fr_ref, ld_ref, k_ref, ss_ref):
    i = pl.program_id(0)
    tt = ft_ref.shape[0]
    c = k_ref.shape[-1]
    ft = ft_ref[...]
    h = jnp.sin(fr_ref[0:1, :] * (_hdot(ft, w1_ref[...]) + b1_ref[...]))
    h = jnp.sin(fr_ref[1:2, :] * (_hdot(h, w2_ref[...]) + b2_ref[...]))
    k = (_hdot(h, w3_ref[...]) + b3_ref[...]) * jnp.exp(-ft[:, 0:1] * jnp.exp(ld_ref[...]))
    first = (lax.broadcasted_iota(jnp.int32, (tt, c), 0) + i * tt) == 0

    @pl.when(i == 0)
    def _():
        ss_ref[...] = jnp.zeros_like(ss_ref)

    for q in range(k_ref.shape[0]):
        kq = k[:, q * c:(q + 1) * c]
        if q % 2 == 1:
            kq = jnp.where(first, 0.0, kq)
        k_ref[q] = kq
        ss_ref[q] += jnp.broadcast_to(jnp.sum(kq * kq, axis=0, keepdims=True), ss_ref.shape[1:])


def hyena_filters_time(n, f_w1, f_b1, f_w2, f_b2, f_w3, f_b3, f_freq, log_decay, *, tt=512):
    nq, d = log_decay.shape[0] * log_decay.shape[1], log_decay.shape[-1]
    fh = f_w2.shape[0]
    emb_pad = 64
    t = np.arange(n, dtype=np.float32)
    t_norm = t / np.float32(max(n - 1, 1))
    bands = np.linspace(1e-4, HY_BANDS - 1, HY_BANDS, dtype=np.float32)
    ang = (np.float32(2.0 * math.pi / n) * t)[:, None] * bands[None, :]
    feats = np.zeros((n, emb_pad), np.float32)
    feats[:, :HY_EMB] = np.concatenate([t_norm[:, None], np.cos(ang), -np.sin(ang)], axis=-1)
    w1 = jnp.zeros((emb_pad, fh), F32).at[:HY_EMB].set(f_w1)
    tt = _tile(n, tt)
    const = lambda shape: pl.BlockSpec(shape, lambda i: (0,) * len(shape))
    return pl.pallas_call(
        _filter_kernel,
        grid=(n // tt,),
        in_specs=[
            pl.BlockSpec((tt, emb_pad), lambda i: (i, 0)),
            const((emb_pad, fh)), const((1, fh)), const((fh, fh)), const((1, fh)),
            const((fh, nq * d)), const((1, nq * d)), const((2, fh)), const((1, nq * d)),
        ],
        out_specs=[
            pl.BlockSpec((nq, tt, d), lambda i: (0, i, 0)),
            pl.BlockSpec((nq, 8, d), lambda i: (0, 0, 0)),
        ],
        out_shape=[jax.ShapeDtypeStruct((nq, n, d), F32), jax.ShapeDtypeStruct((nq, 8, d), F32)],
        compiler_params=_cparams("arbitrary"),
        name="hyena_filter_mlp",
    )(jnp.asarray(feats), w1, f_b1.reshape(1, fh), f_w2, f_b2.reshape(1, fh), f_w3,
      f_b3.reshape(1, nq * d), f_freq, log_decay.reshape(1, nq * d))


def _fft_dims(n):
    nb = 1 << int(math.floor(math.log2(math.sqrt(2 * n))))
    na = 2 * n // nb
    assert na * nb == 2 * n and na % 16 == 0 and nb % 8 == 0, (n, na, nb)
    return na, nb


@functools.lru_cache(maxsize=None)
def _fft_tables(n):
    na, nb = _fft_dims(n)
    na2, nn = na // 2, 2 * n
    ka = np.arange(na, dtype=np.int64)[None, :, None]
    a = np.arange(na2, dtype=np.int64)[None, None, :]
    b = np.arange(nb, dtype=np.int64)[:, None, None]
    ang = (2.0 * np.pi / nn) * ((ka * (a * nb + b)) % nn)
    mr, mi = np.cos(ang), -np.sin(ang)
    f1 = np.concatenate([np.concatenate([mr, -mi], 2), np.concatenate([mi, mr], 2)], 1)
    mrt, mit = np.swapaxes(mr, 1, 2) / nn, np.swapaxes(mi, 1, 2) / nn
    g1 = np.concatenate([np.concatenate([mrt, mit], 2), np.concatenate([-mit, mrt], 2)], 1)
    kb = np.arange(nb, dtype=np.int64)
    ang2 = (2.0 * np.pi / nb) * ((kb[:, None] * kb[None, :]) % nb)
    er, ei = np.cos(ang2), -np.sin(ang2)
    f2 = np.block([[er, -ei], [ei, er]])
    g2 = np.block([[er, ei], [-ei, er]])
    as_bf16 = lambda x: x.astype(np.float32).astype(BF16)
    return dict(f1=as_bf16(f1), f1_real=as_bf16(f1[:, :, :na2]), g1=as_bf16(g1), f2=as_bf16(f2), g2=as_bf16(g2))


def _fft_s1_kernel(x_ref, f_ref, o_ref):
    na = o_ref.shape[1]
    for j in range(x_ref.shape[2]):
        xs = [x_ref[r, :, j, :] for r in range(x_ref.shape[0])]
        xs = xs[0] if len(xs) == 1 else jnp.concatenate(xs, axis=0)
        res = jnp.dot(f_ref[j], xs.astype(BF16), preferred_element_type=F32)
        o_ref[0, :, j, :] = res[:na]
        o_ref[1, :, j, :] = res[na:]


def fft_stage1(x, n, c, c_off, table, rows_per_seq, *, jb=8, cb=1024):
    na, nb = _fft_dims(n)
    na2 = na // 2
    r = rows_per_seq
    p = x.shape[0] // r
    ctot = x.shape[-1]
    cb = min(cb, c)
    off = c_off // cb
    xv = x.reshape(x.shape[0], na2, nb, ctot)
    return pl.pallas_call(
        _fft_s1_kernel,
        grid=(nb // jb, p, c // cb),
        in_specs=[
            pl.BlockSpec((r, na2, jb, cb), lambda bb, q, ci: (q, 0, bb, ci + off)),
            pl.BlockSpec((jb, 2 * na, r * na2), lambda bb, q, ci: (bb, 0, 0)),
        ],
        out_specs=pl.BlockSpec((None, 2, na, jb, cb), lambda bb, q, ci: (q, 0, 0, bb, ci)),
        out_shape=jax.ShapeDtypeStruct((p, 2, na, nb, c), F32),
        compiler_params=_cparams("parallel", "parallel", "parallel"),
        name="fft_stage1",
    )(xv, table)


def _fft_s2_kernel(a_ref, k_ref, f2_ref, g2_ref, o_ref):
    nb = a_ref.shape[2]
    for k in range(a_ref.shape[1]):
        a = jnp.concatenate([a_ref[0, k], a_ref[1, k]], axis=0)
        x = jnp.dot(f2_ref[...], a.astype(BF16), preferred_element_type=F32)
        xr, xi = x[:nb], x[nb:]
        kr, ki = k_ref[0, k], k_ref[1, k]
        y = jnp.concatenate([xr * kr - xi * ki, xr * ki + xi * kr], axis=0)
        bv = jnp.dot(g2_ref[...], y.astype(BF16), preferred_element_type=F32)
        o_ref[0, k] = bv[:nb]
        o_ref[1, k] = bv[nb:]


def fft_stage2(a, kf, order, tables, *, ka=8, cb=512):
    p, _, na, nb, c = a.shape
    cb = min(cb, c)
    blk = (None, 2, ka, nb, cb)
    return pl.pallas_call(
        _fft_s2_kernel,
        grid=(na // ka, c // cb, p),
        in_specs=[
            pl.BlockSpec(blk, lambda kk, ci, q: (q, 0, kk, 0, ci)),
            pl.BlockSpec(blk, lambda kk, ci, q: (order, 0, kk, 0, ci)),
            pl.BlockSpec((2 * nb, 2 * nb), lambda kk, ci, q: (0, 0)),
            pl.BlockSpec((2 * nb, 2 * nb), lambda kk, ci, q: (0, 0)),
        ],
        out_specs=pl.BlockSpec(blk, lambda kk, ci, q: (q, 0, kk, 0, ci)),
        out_shape=jax.ShapeDtypeStruct(a.shape, F32),
        compiler_params=_cparams("parallel", "parallel", "parallel"),
        name="fft_stage2",
    )(a, kf, tables["f2"], tables["g2"])


def _fft_s2_filter_kernel(a_ref, ss_ref, f2_ref, o_ref):
    nb = a_ref.shape[3]
    for o in range(o_ref.shape[0]):
        scale = lax.rsqrt(ss_ref[2 * o, 0:1, :] + ss_ref[2 * o + 1, 0:1, :] + 1e-12)
        for k in range(a_ref.shape[2]):
            hf = jnp.dot(f2_ref[...], jnp.concatenate([a_ref[2 * o, 0, k], a_ref[2 * o, 1, k]], axis=0).astype(BF16),
                         preferred_element_type=F32)
            hb = jnp.dot(f2_ref[...], jnp.concatenate([a_ref[2 * o + 1, 0, k], a_ref[2 * o + 1, 1, k]], axis=0).astype(BF16),
                         preferred_element_type=F32)
            o_ref[o, 0, k] = (hf[:nb] + hb[:nb]) * scale
            o_ref[o, 1, k] = (hf[nb:] - hb[nb:]) * scale


def fft_filter_spectrum(a, ss, tables, *, ka=8, cb=512):
    nq, _, na, nb, c = a.shape
    cb = min(cb, c)
    return pl.pallas_call(
        _fft_s2_filter_kernel,
        grid=(na // ka, c // cb),
        in_specs=[
            pl.BlockSpec((nq, 2, ka, nb, cb), lambda kk, ci: (0, 0, kk, 0, ci)),
            pl.BlockSpec((nq, 8, cb), lambda kk, ci: (0, 0, ci)),
            pl.BlockSpec((2 * nb, 2 * nb), lambda kk, ci: (0, 0)),
        ],
        out_specs=pl.BlockSpec((nq // 2, 2, ka, nb, cb), lambda kk, ci: (0, 0, kk, 0, ci)),
        out_shape=jax.ShapeDtypeStruct((nq // 2, 2, na, nb, c), F32),
        compiler_params=_cparams("parallel", "parallel"),
        name="fft_filter_spectrum",
    )(a, ss, tables["f2"])


def _fft_s3_kernel(b_ref, g_ref, z_ref, m_ref, skip_ref, o_ref):
    na2 = z_ref.shape[1]
    skip = skip_ref[...]
    for j in range(z_ref.shape[2]):
        bs = jnp.concatenate([b_ref[0, :, j, :], b_ref[1, :, j, :]], axis=0)
        y = jnp.dot(g_ref[j], bs.astype(BF16), preferred_element_type=F32)
        for r in range(2):
            o_ref[r, :, j, :] = m_ref[r, :, j, :] * (y[r * na2:(r + 1) * na2] + skip * z_ref[r, :, j, :])


def fft_stage3(bsp, z, z_off, m, m_off, skip, n, tables, *, jb=8, cb=512):
    na, nb = _fft_dims(n)
    na2 = na // 2
    p, c = bsp.shape[0], bsp.shape[-1]
    cb = min(cb, c)
    zv = z.reshape(z.shape[0], na2, nb, z.shape[-1])
    mv = m.reshape(m.shape[0], na2, nb, m.shape[-1])
    zo, mo = z_off // cb, m_off // cb
    out = pl.pallas_call(
        _fft_s3_kernel,
        grid=(nb // jb, p, c // cb),
        in_specs=[
            pl.BlockSpec((None, 2, na, jb, cb), lambda bb, q, ci: (q, 0, 0, bb, ci)),
            pl.BlockSpec((jb, na, 2 * na), lambda bb, q, ci: (bb, 0, 0)),
            pl.BlockSpec((2, na2, jb, cb), lambda bb, q, ci: (q, 0, bb, ci + zo)),
            pl.BlockSpec((2, na2, jb, cb), lambda bb, q, ci: (q, 0, bb, ci + mo)),
            pl.BlockSpec((1, cb), lambda bb, q, ci: (0, ci)),
        ],
        out_specs=pl.BlockSpec((2, na2, jb, cb), lambda bb, q, ci: (q, 0, bb, ci)),
        out_shape=jax.ShapeDtypeStruct((2 * p, na2, nb, c), F32),
        compiler_params=_cparams("parallel", "parallel", "parallel"),
        name="fft_stage3",
    )(bsp, tables["g1"], zv, mv, skip.reshape(1, c))
    return out.reshape(2 * p, n, c)


def hyena_mixer_block(h, mods, g, prm):
    bsz, n, d = h.shape
    tables = _fft_tables(n)
    proj = hyena_in_proj(h, mods, g, prm["w_in"], prm["b_in"], prm["conv_w"], prm["conv_b"])
    kt, ss = hyena_filters_time(n, prm["f_w1"], prm["f_b1"], prm["f_w2"], prm["f_b2"], prm["f_w3"],
                                prm["f_b3"], prm["f_freq"], prm["log_decay"])
    kf = fft_filter_spectrum(fft_stage1(kt, n, d, 0, tables["f1_real"], 1), ss, tables)
    a = fft_stage1(proj, n, d, 2 * d, tables["f1"], 2)
    z = fft_stage3(fft_stage2(a, kf, 0, tables), proj, 2 * d, proj, 0, prm["skip"][0], n, tables)
    a = fft_stage1(z, n, d, 0, tables["f1"], 2)
    y = fft_stage3(fft_stage2(a, kf, 1, tables), z, 0, proj, d, prm["skip"][1], n, tables)
    return linear_residual(y, h, mods[:, 2:3, :], prm["w_out"], prm["b_out"])


def _mlstm_pre_kernel(hp_ref, h_ref, hn_ref, mod_ref, g_ref, wup_ref, cw_ref, cb_ref, wq_ref, wk_ref, wv_ref,
                      wg_ref, bg_ref, q_ref, k_ref, v_ref, xc_ref, z_ref, gt_ref, *, k_scale):
    tm = h_ref.shape[0]
    inner = q_ref.shape[-1]
    grp = wq_ref.shape[-1]
    rows = jnp.concatenate([hp_ref[...], h_ref[...], hn_ref[...]], axis=0)
    u = _rms_mod(rows, g_ref[...], mod_ref[0:1, :], mod_ref[1:2, :])
    up = _bdot(u, wup_ref[...])
    xm_h = _mask_seq_ends(up[:, :inner], tm)
    xm = xm_h[HALO:HALO + tm]
    z_ref[...] = up[HALO:HALO + tm, inner:]
    xc = _silu(_conv3(xm_h, cw_ref, tm) + cb_ref[...])
    xc_ref[...] = xc
    gates = jnp.zeros(gt_ref.shape, F32) + bg_ref[...]
    for gi in range(inner // grp):
        lo, hi = gi * grp, (gi + 1) * grp
        q = _bdot(xc[:, lo:hi], wq_ref[gi])
        k = _bdot(xc[:, lo:hi], wk_ref[gi])
        v = _bdot(xm[:, lo:hi], wv_ref[gi])
        q_ref[:, lo:hi] = q.astype(q_ref.dtype)
        k_ref[:, lo:hi] = (k * k_scale).astype(k_ref.dtype)
        v_ref[:, lo:hi] = v.astype(v_ref.dtype)
        gates += (_bdot(q, wg_ref[lo:hi, :]) + _bdot(k, wg_ref[inner + lo:inner + hi, :])
                  + _bdot(v, wg_ref[2 * inner + lo:2 * inner + hi, :]))
    gt_ref[...] = gates


def mlstm_pre(h, mods, g, prm, *, tm=256):
    bsz, n, d = h.shape
    inner = prm["w_up"].shape[1] // 2
    ngate = prm["w_gate"].shape[1]
    grp = prm["wq_dense"].shape[-1]
    tm = _tile(n, tm)
    const = lambda shape: pl.BlockSpec(shape, lambda b_, i: (0,) * len(shape))
    row_spec = lambda w: pl.BlockSpec((None, tm, w), lambda b_, i: (b_, i, 0))
    big = jax.ShapeDtypeStruct((bsz, n, inner), F32)
    mm = jax.ShapeDtypeStruct((bsz, n, inner), BF16)
    return pl.pallas_call(
        functools.partial(_mlstm_pre_kernel, k_scale=float((inner // ML_HEADS) ** -0.5)),
        grid=(bsz, n // tm),
        in_specs=_halo_specs(n, tm, d) + [
            pl.BlockSpec((None, 3, d), lambda b_, i: (b_, 0, 0)),
            const((1, d)), _resident((d, 2 * inner), lambda b_, i: (0, 0)), const((3, inner)), const((1, inner)),
            const((inner // grp, grp, grp)), const((inner // grp, grp, grp)), const((inner // grp, grp, grp)),
            const((3 * inner, ngate)), const((1, ngate)),
        ],
        out_specs=[row_spec(inner)] * 5 + [row_spec(ngate)],
        out_shape=[mm] * 3 + [big] * 2 + [jax.ShapeDtypeStruct((bsz, n, ngate), F32)],
        compiler_params=_cparams("parallel", "parallel"),
        name="mlstm_pre",
    )(h, h, h, mods, g.reshape(1, d), prm["w_up"], prm["conv_w"], prm["conv_b"].reshape(1, inner),
      prm["wq_dense"], prm["wk_dense"], prm["wv_dense"], prm["w_gate"], prm["b_gate"].reshape(1, ngate))


def _dense_blockdiag(w, grp):
    nb, bs, _ = w.shape
    per = grp // bs
    eye = jnp.eye(per, dtype=w.dtype)
    dense = jnp.einsum("gmde,mn->gmdne", w.reshape(nb // per, per, bs, bs), eye)
    return dense.reshape(nb // per, grp, grp).astype(BF16)


def _split3(x):
    p1 = x.astype(BF16)
    r1 = x - p1.astype(F32)
    p2 = r1.astype(BF16)
    p3 = (r1 - p2.astype(F32)).astype(BF16)
    return p1, p2, p3


def _nt(a, b):
    return lax.dot_general(a, b, (((1,), (1,)), ((), ())), preferred_element_type=F32)


def _scan_kernel(*refs, has_init, emit_state, aug, heads_per_step):
    it = iter(refs)
    q_ref, k_ref, v_ref, gt_ref = next(it), next(it), next(it), next(it)
    allow_ref, allow_t_ref, eye_ref, neg_ref = next(it), next(it), next(it), next(it)
    c0_ref, m0_ref = (next(it), next(it)) if has_init else (None, None)
    h_ref = next(it)
    cf_ref, mf_ref = (next(it), next(it)) if emit_state else (None, None)
    c_sc, m_sc = next(it), next(it)
    c = pl.program_id(3)
    ch = q_ref.shape[0]
    dh = q_ref.shape[1] // heads_per_step

    @pl.when(c == 0)
    def _():
        if has_init:
            c_sc[...] = c0_ref[...]
            m_sc[...] = m0_ref[...]
        else:
            c_sc[...] = jnp.zeros_like(c_sc)
            m_sc[...] = jnp.zeros_like(m_sc)

    allow, allow_t, eye, neg = allow_ref[...], allow_t_ref[...], eye_ref[...], neg_ref[...]
    ones_col = (lax.broadcasted_iota(jnp.int32, (ch, aug), 1) == 0).astype(BF16)

    for hh in range(heads_per_step):
        lo, hi = hh * dh, (hh + 1) * dh
        i_row = gt_ref[hh, 0:1, :]
        f_row = gt_ref[hh, 1:2, :]
        lf_row = jnp.minimum(f_row, 0.0) - jnp.log1p(jnp.exp(-jnp.abs(f_row)))
        pieces = jnp.concatenate(list(_split3(lf_row)) + list(_split3(i_row)) + [jnp.zeros((2, ch), BF16)], axis=0)
        cum_col = _nt(allow, pieces)
        tr_col = _nt(eye, pieces)
        cum_row = jnp.dot(pieces, allow_t, preferred_element_type=F32)
        b_col = cum_col[:, 0:1] + cum_col[:, 1:2] + cum_col[:, 2:3]
        i_col = tr_col[:, 3:4] + tr_col[:, 4:5] + tr_col[:, 5:6]
        b_row = cum_row[0:1, :] + cum_row[1:2, :] + cum_row[2:3, :]
        total = jnp.sum(lf_row, axis=1, keepdims=True)
        e_row, e_col = i_row - b_row, i_col - b_col

        m_st = m_sc[hh, 0:1, 0:1]
        dmat = (b_col + e_row) + neg
        a_col = b_col + m_st
        m_j = jnp.maximum(a_col, jnp.max(dmat, axis=1, keepdims=True))
        w_inter = jnp.exp(a_col - m_j)

        q = q_ref[:, lo:hi]
        k = k_ref[:, lo:hi]
        v_aug = jnp.concatenate([v_ref[:, lo:hi], ones_col], axis=1)
        s = _nt(q, k) * jnp.exp(dmat - m_j)
        c_st = c_sc[hh]
        num = w_inter * jnp.dot(q, c_st.astype(BF16), preferred_element_type=F32) + _bdot(s, v_aug)
        qn = num[:, dh:dh + 1]
        h_ref[:, lo:hi] = num[:, :dh] / jnp.maximum(jnp.abs(qn), jnp.exp(-m_j))

        m_new = jnp.maximum(total + m_st, jnp.max(total + e_row, axis=1, keepdims=True))
        decay = jnp.exp(total + m_st - m_new)
        kw = (k.astype(F32) * jnp.exp(total + e_col - m_new)).astype(BF16)
        c_new = decay * c_st + lax.dot_general(kw, v_aug, (((0,), (0,)), ((), ())), preferred_element_type=F32)
        c_sc[hh] = c_new
        m_sc[hh] = jnp.broadcast_to(m_new, m_sc.shape[1:])

    if emit_state:
        @pl.when(c == pl.num_programs(3) - 1)
        def _():
            cf_ref[...] = c_sc[...]
            mf_ref[...] = m_sc[...]


@functools.lru_cache(maxsize=None)
def _scan_masks(ch):
    r = np.arange(ch)
    causal = (r[None, :] <= r[:, None]).astype(np.float32)
    allow = np.stack([causal, causal.T])
    neg = np.where(allow > 0, 0.0, -np.inf).astype(np.float32)
    return (allow.astype(BF16), np.transpose(allow, (0, 2, 1)).astype(BF16),
            np.eye(ch, dtype=np.float32).astype(BF16), neg)


def mlstm_scan(q, k, v, gates_t, init=None, *, emit_state=False):
    bsz, n, inner = q.shape
    nh = gates_t.shape[2]
    dh = inner // nh
    aug = 128
    hp = 2 if nh % 2 == 0 else 1
    ch = min(SCAN_CHUNK, n)
    nc = n // ch
    ceff = lambda d, c: c + d * (nc - 1 - 2 * c)
    seq = pl.BlockSpec((None, ch, hp * dh), lambda d, b, hd, c: (b, ceff(d, c), hd))
    st_c = pl.BlockSpec((None, None, hp, dh, dh + aug), lambda d, b, hd, c: (d, b, hd, 0, 0))
    st_m = pl.BlockSpec((None, None, hp, 8, 128), lambda d, b, hd, c: (d, b, hd, 0, 0))
    by_dir = pl.BlockSpec((None, ch, ch), lambda d, b, hd, c: (d, 0, 0))
    in_specs = [seq, seq, seq,
                pl.BlockSpec((None, None, hp, 2, ch), lambda d, b, hd, c: (b, d, hd, 0, ceff(d, c))),
                by_dir, by_dir, pl.BlockSpec((ch, ch), lambda d, b, hd, c: (0, 0)), by_dir]
    args = [q, k, v, gates_t, *_scan_masks(ch)]
    if init is not None:
        in_specs += [st_c, st_m]
        args += list(init)
    out_specs = [pl.BlockSpec((None, None, ch, hp * dh), lambda d, b, hd, c: (d, b, ceff(d, c), hd))]
    out_shape = [jax.ShapeDtypeStruct((2, bsz, n, inner), F32)]
    if emit_state:
        out_specs += [st_c, st_m]
        out_shape += [jax.ShapeDtypeStruct((2, bsz, nh, dh, dh + aug), F32),
                      jax.ShapeDtypeStruct((2, bsz, nh, 8, 128), F32)]
    res = pl.pallas_call(
        functools.partial(_scan_kernel, has_init=init is not None, emit_state=emit_state, aug=aug,
                          heads_per_step=hp),
        grid=(2, bsz, nh // hp, nc),
        in_specs=in_specs,
        out_specs=out_specs,
        out_shape=out_shape,
        scratch_shapes=[pltpu.VMEM((hp, dh, dh + aug), F32), pltpu.VMEM((hp, 8, 128), F32)],
        compiler_params=_cparams("parallel", "parallel", "parallel", "arbitrary"),
        name="mlstm_scan",
    )(*args)
    return (res[0], (res[1], res[2])) if emit_state else (res[0], None)


def _mlstm_post_kernel(hf_ref, hb_ref, xc_ref, z_ref, res_ref, gate_ref, ng_ref, sk_ref, wd_ref, o_ref):
    inner = xc_ref.shape[-1]
    dh = inner // ML_HEADS
    acc = jnp.zeros(o_ref.shape, F32)
    for hd in range(ML_HEADS):
        lo, hi = hd * dh, (hd + 1) * dh
        x = hf_ref[:, lo:hi] + hb_ref[:, lo:hi]
        xz = x - jnp.mean(x, axis=-1, keepdims=True)
        var = jnp.mean(xz * xz, axis=-1, keepdims=True)
        hn = xz * lax.rsqrt(var + ML_NORM_EPS) * ng_ref[:, lo:hi]
        t = (hn + sk_ref[:, lo:hi] * xc_ref[:, lo:hi]) * _silu(z_ref[:, lo:hi])
        acc += _bdot(t, wd_ref[lo:hi, :])
    o_ref[...] = res_ref[...] + gate_ref[...] * acc


def mlstm_post(hs, xc, z, h_res, gate, prm, *, tm=256):
    bsz, n, inner = xc.shape
    d = h_res.shape[-1]
    tm = _tile(n, tm)
    const = lambda shape: pl.BlockSpec(shape, lambda b_, i: (0,) * len(shape))
    return pl.pallas_call(
        _mlstm_post_kernel,
        grid=(bsz, n // tm),
        in_specs=[
            pl.BlockSpec((None, None, tm, inner), lambda b_, i: (0, b_, i, 0)),
            pl.BlockSpec((None, None, tm, inner), lambda b_, i: (1, b_, i, 0)),
            pl.BlockSpec((None, tm, inner), lambda b_, i: (b_, i, 0)),
            pl.BlockSpec((None, tm, inner), lambda b_, i: (b_, i, 0)),
            pl.BlockSpec((None, tm, d), lambda b_, i: (b_, i, 0)),
            pl.BlockSpec((None, 1, d), lambda b_, i: (b_, 0, 0)),
            const((1, inner)), const((1, inner)), _resident((inner, d), lambda b_, i: (0, 0)),
        ],
        out_specs=pl.BlockSpec((None, tm, d), lambda b_, i: (b_, i, 0)),
        out_shape=jax.ShapeDtypeStruct((bsz, n, d), F32),
        compiler_params=_cparams("parallel", "parallel"),
        name="mlstm_post",
    )(hs, hs, xc, z, h_res, gate, prm["norm_g"].reshape(1, inner), prm["skip"].reshape(1, inner), prm["w_down"])


def _gates_by_head(gates, nh):
    bsz, n, _ = gates.shape
    return jnp.transpose(gates.reshape(bsz, n, 2, 2, nh), (0, 2, 4, 3, 1))


def mlstm_mixer_block(h_ctx, h_lat, mods_ctx, mods_lat, g, prm, *, with_ctx_out):
    pc = mlstm_pre(h_ctx, mods_ctx, g, prm)
    pq = mlstm_pre(h_lat, mods_lat, g, prm)
    hs_ctx, state = mlstm_scan(pc[0], pc[1], pc[2], _gates_by_head(pc[5], ML_HEADS), emit_state=True)
    hs_lat, _ = mlstm_scan(pq[0], pq[1], pq[2], _gates_by_head(pq[5], ML_HEADS), init=state)
    out_lat = mlstm_post(hs_lat, pq[3], pq[4], h_lat, mods_lat[:, 2:3, :], prm)
    out_ctx = mlstm_post(hs_ctx, pc[3], pc[4], h_ctx, mods_ctx[:, 2:3, :], prm) if with_ctx_out else None
    return out_ctx, out_lat


def _pos_embed_2d(n_tokens, d):
    rows = n_tokens // GRID_W
    quarter = d // 4
    omega = 1.0 / (POS_BASE ** (jnp.arange(quarter, dtype=F32) / quarter))
    ang_r = jnp.arange(rows, dtype=F32)[:, None] * omega
    ang_c = jnp.arange(GRID_W, dtype=F32)[:, None] * omega
    row_emb = jnp.concatenate([jnp.sin(ang_r), jnp.cos(ang_r)], axis=-1)
    col_emb = jnp.concatenate([jnp.sin(ang_c), jnp.cos(ang_c)], axis=-1)
    emb = jnp.concatenate([
        jnp.broadcast_to(row_emb[:, None, :], (rows, GRID_W, d // 2)),
        jnp.broadcast_to(col_emb[None, :, :], (rows, GRID_W, d // 2))], axis=-1)
    return emb.reshape(rows * GRID_W, d)


def kernel(x, c, ctx, c_ctx, ada_w, ada_b, norm_g, final_g, ffn_w_in, ffn_w_out, hy_w_in, hy_b_in, hy_conv_w, hy_conv_b, hy_f_w1, hy_f_b1, hy_f_w2, hy_f_b2, hy_f_w3, hy_f_b3, hy_f_freq, hy_log_decay, hy_skip, hy_w_out, hy_b_out, ml_w_up, ml_conv_w, ml_conv_b, ml_w_q, ml_w_k, ml_w_v, ml_w_gate, ml_b_gate, ml_norm_g, ml_skip, ml_w_down):
    bsz, n_lat, d = x.shape
    n_ctx = ctx.shape[1]
    depth = ada_w.shape[0]
    n_mixers = 2

    cond_rows = 8
    cond = jnp.zeros((cond_rows, d), F32).at[:bsz].set(c).at[bsz].set(c_ctx)
    mods_all = ada_modulation(cond, ada_w, ada_b).reshape(depth, cond_rows, N_MOD, d)

    ffn_w_in = ffn_w_in.astype(BF16)
    ffn_w_out = ffn_w_out.astype(BF16)
    pos = _pos_embed_2d(n_lat, d)

    h_lat = x
    h_ctx = ctx.reshape(1, bsz * n_ctx, d)
    for l in range(depth):
        last = l == depth - 1
        j = l // n_mixers
        m_lat = mods_all[l, :bsz]
        m_ctx = mods_all[l, bsz:bsz + 1]
        m_ctx_b = jnp.broadcast_to(m_ctx, (bsz, N_MOD, d))

        h_lat = ffn_block(h_lat, m_lat[:, 0:3], norm_g[l, 0], ffn_w_in, ffn_w_out, widx=(l, 0),
                          pos=pos if l == 0 else None)
        h_ctx = ffn_block(h_ctx, m_ctx[:, 0:3], norm_g[l, 0], ffn_w_in, ffn_w_out, widx=(l, 0))

        h_ctx = h_ctx.reshape(bsz, n_ctx, d)
        if l % n_mixers == 0:
            prm = dict(w_in=hy_w_in[j].astype(BF16), b_in=hy_b_in[j], conv_w=hy_conv_w[j], conv_b=hy_conv_b[j],
                       f_w1=hy_f_w1[j], f_b1=hy_f_b1[j], f_w2=hy_f_w2[j], f_b2=hy_f_b2[j], f_w3=hy_f_w3[j],
                       f_b3=hy_f_b3[j], f_freq=hy_f_freq[j], log_decay=hy_log_decay[j], skip=hy_skip[j],
                       w_out=hy_w_out[j].astype(BF16), b_out=hy_b_out[j])
            h_lat = hyena_mixer_block(h_lat, m_lat[:, 3:6], norm_g[l, 1], prm)
            if not last:
                h_ctx = hyena_mixer_block(h_ctx, m_ctx_b[:, 3:6], norm_g[l, 1], prm)
        else:
            prm = dict(w_up=ml_w_up[j].astype(BF16), conv_w=ml_conv_w[j], conv_b=ml_conv_b[j],
                       wq_dense=_dense_blockdiag(ml_w_q[j], ML_QKV_GROUP),
                       wk_dense=_dense_blockdiag(ml_w_k[j], ML_QKV_GROUP),
                       wv_dense=_dense_blockdiag(ml_w_v[j], ML_QKV_GROUP),
                       w_gate=ml_w_gate[j].astype(BF16), b_gate=ml_b_gate[j], norm_g=ml_norm_g[j],
                       skip=ml_skip[j], w_down=ml_w_down[j].astype(BF16))
            new_ctx, h_lat = mlstm_mixer_block(h_ctx, h_lat, m_ctx_b[:, 3:6], m_lat[:, 3:6], norm_g[l, 1], prm,
                                               with_ctx_out=not last)
            h_ctx = h_ctx if last else new_ctx
        h_ctx = h_ctx.reshape(1, bsz * n_ctx, d)

        h_lat = ffn_block(h_lat, m_lat[:, 6:9], norm_g[l, 2], ffn_w_in, ffn_w_out, widx=(l, 1),
                          final_g=final_g if last else None)
        if not last:
            h_ctx = ffn_block(h_ctx, m_ctx[:, 6:9], norm_g[l, 2], ffn_w_in, ffn_w_out, widx=(l, 1))
    return h_lat
```

```python
import functools
import math

import jax
import jax.numpy as jnp
import numpy as np
from jax import lax
from jax.experimental import pallas as pl
from jax.experimental.pallas import tpu as pltpu

F32 = jnp.float32
BF16 = jnp.bfloat16

NORM_EPS = 1e-6
GRID_W = 64
POS_BASE = 10000.0
N_MOD = 9
HY_EMB = 33
HY_BANDS = (HY_EMB - 1) // 2
ML_HEADS = 4
ML_NORM_EPS = 1e-5
ML_QKV_GROUP = 256
SCAN_CHUNK = 256
HALO = 8

V7X_VMEM_LIMIT = 56 * 1024 * 1024


def _cparams(*sem):
    return pltpu.CompilerParams(dimension_semantics=sem, vmem_limit_bytes=V7X_VMEM_LIMIT)


def _bdot(a, b):
    return jnp.dot(a.astype(BF16), b.astype(BF16), preferred_element_type=F32)


def _rms_mod(h, g, shift, scale):
    y = h * lax.rsqrt(jnp.mean(h * h, axis=-1, keepdims=True) + NORM_EPS)
    return (y * g) * (1.0 + scale) + shift


def _silu(x):
    return x * (1.0 / (1.0 + jnp.exp(-x)))


def _tile(n, pref):
    if n <= pref:
        return n
    for t in range(pref, 7, -1):
        if n % t == 0 and t % 8 == 0:
            return t
    return n


def _mod_kernel(c_ref, w_ref, b_ref, o_ref):
    o_ref[...] = _bdot(_silu(c_ref[...]), w_ref[...]) + b_ref[...]


def ada_modulation(cond, ada_w, ada_b):
    n_layers, d, n_out = ada_w.shape
    r = cond.shape[0]
    tn = _tile(n_out, 2304) if n_out % 128 == 0 else n_out
    return pl.pallas_call(
        _mod_kernel,
        grid=(n_layers, n_out // tn),
        in_specs=[
            pl.BlockSpec((r, d), lambda l, j: (0, 0)),
            pl.BlockSpec((None, d, tn), lambda l, j: (l, 0, j)),
            pl.BlockSpec((None, 1, tn), lambda l, j: (l, 0, j)),
        ],
        out_specs=pl.BlockSpec((None, r, tn), lambda l, j: (l, 0, j)),
        out_shape=jax.ShapeDtypeStruct((n_layers, r, n_out), F32),
        compiler_params=_cparams("parallel", "parallel"),
        name="ada_modulation",
    )(cond, ada_w, ada_b.reshape(n_layers, 1, n_out))


def _ffn_kernel(*refs, has_pos, final):
    it = iter(refs)
    h_ref = next(it)
    pos_ref = next(it) if has_pos else None
    mod_ref, g_ref, wg_ref, wv_ref, wo_ref = next(it), next(it), next(it), next(it), next(it)
    fg_ref = next(it) if final else None
    o_ref = next(it)
    h = h_ref[...]
    if has_pos:
        h = h + pos_ref[...]
    u = _rms_mod(h, g_ref[...], mod_ref[0:1, :], mod_ref[1:2, :]).astype(BF16)
    gate = jnp.dot(u, wg_ref[...], preferred_element_type=F32)
    val = jnp.dot(u, wv_ref[...], preferred_element_type=F32)
    out = h + (0.5 * mod_ref[2:3, :]) * _bdot(_silu(gate) * val, wo_ref[...])
    if final:
        out = out * lax.rsqrt(jnp.mean(out * out, axis=-1, keepdims=True) + NORM_EPS) * fg_ref[...]
    o_ref[...] = out


def _resident(shape, index_map):
    return pl.BlockSpec(shape, index_map, pipeline_mode=pl.Buffered(1))


def ffn_block(h, mods, g, w_in, w_out, *, widx=(), pos=None, final_g=None, tm=512):
    bsz, n, d = h.shape
    ff = w_out.shape[-2]
    tm = _tile(n, tm)
    has_pos, final = pos is not None, final_g is not None
    lead = (None,) * len(widx)
    in_specs = [pl.BlockSpec((None, tm, d), lambda b, i: (b, i, 0))]
    args = [h]
    if has_pos:
        in_specs.append(pl.BlockSpec((tm, d), lambda b, i: (i, 0)))
        args.append(pos)
    in_specs += [
        pl.BlockSpec((None, 3, d), lambda b, i: (b, 0, 0)),
        pl.BlockSpec((1, d), lambda b, i: (0, 0)),
        _resident(lead + (d, ff), lambda b, i: widx + (0, 0)),
        _resident(lead + (d, ff), lambda b, i: widx + (0, 1)),
        _resident(lead + (ff, d), lambda b, i: widx + (0, 0)),
    ]
    args += [mods, g.reshape(1, d), w_in, w_in, w_out]
    if final:
        in_specs.append(pl.BlockSpec((1, d), lambda b, i: (0, 0)))
        args.append(final_g.reshape(1, d))
    return pl.pallas_call(
        functools.partial(_ffn_kernel, has_pos=has_pos, final=final),
        grid=(bsz, n // tm),
        in_specs=in_specs,
        out_specs=pl.BlockSpec((None, tm, d), lambda b, i: (b, i, 0)),
        out_shape=jax.ShapeDtypeStruct((bsz, n, d), F32),
        compiler_params=_cparams("parallel", "parallel"),
        name="ffn_block",
    )(*args)


def _linres_kernel(y_ref, h_ref, gate_ref, w_ref, b_ref, o_ref):
    o_ref[...] = h_ref[...] + gate_ref[...] * (_bdot(y_ref[...], w_ref[...]) + b_ref[...])


def linear_residual(y, h, gate, w, b, *, tm=512):
    bsz, n, kdim = y.shape
    d = h.shape[-1]
    tm = _tile(n, tm)
    return pl.pallas_call(
        _linres_kernel,
        grid=(bsz, n // tm),
        in_specs=[
            pl.BlockSpec((None, tm, kdim), lambda b_, i: (b_, i, 0)),
            pl.BlockSpec((None, tm, d), lambda b_, i: (b_, i, 0)),
            pl.BlockSpec((None, 1, d), lambda b_, i: (b_, 0, 0)),
            _resident((kdim, d), lambda b_, i: (0, 0)),
            pl.BlockSpec((1, d), lambda b_, i: (0, 0)),
        ],
        out_specs=pl.BlockSpec((None, tm, d), lambda b_, i: (b_, i, 0)),
        out_shape=jax.ShapeDtypeStruct((bsz, n, d), F32),
        compiler_params=_cparams("parallel", "parallel"),
        name="linear_residual",
    )(y, h, gate, w, b.reshape(1, d))


def _halo_specs(n, tm, d):
    hb, last = tm // HALO, n // HALO - 1
    return [
        pl.BlockSpec((None, HALO, d), lambda b, i: (b, jnp.maximum(i * hb - 1, 0), 0)),
        pl.BlockSpec((None, tm, d), lambda b, i: (b, i, 0)),
        pl.BlockSpec((None, HALO, d), lambda b, i: (b, jnp.minimum((i + 1) * hb, last), 0)),
    ]


def _mask_seq_ends(p, tm):
    i, ni = pl.program_id(1), pl.num_programs(1)
    r = lax.broadcasted_iota(jnp.int32, p.shape, 0)
    lo = jnp.where(i == 0, HALO, 0)
    hi = jnp.where(i == ni - 1, tm + HALO, tm + 2 * HALO)
    return jnp.where(r < lo, 0.0, jnp.where(r >= hi, 0.0, p))


def _conv3(p, cw_ref, tm):
    rows = p.shape[0]
    prev = pltpu.roll(p, 1, 0)[HALO:HALO + tm]
    nxt = pltpu.roll(p, rows - 1, 0)[HALO:HALO + tm]
    return prev * cw_ref[0:1, :] + p[HALO:HALO + tm] * cw_ref[1:2, :] + nxt * cw_ref[2:3, :]


def _hyena_in_kernel(hp_ref, h_ref, hn_ref, mod_ref, g_ref, w_ref, b_ref, cw_ref, cb_ref, o_ref):
    tm = h_ref.shape[0]
    rows = jnp.concatenate([hp_ref[...], h_ref[...], hn_ref[...]], axis=0)
    u = _rms_mod(rows, g_ref[...], mod_ref[0:1, :], mod_ref[1:2, :])
    p = _mask_seq_ends(_bdot(u, w_ref[...]) + b_ref[...], tm)
    o_ref[...] = _conv3(p, cw_ref, tm) + cb_ref[...]


def hyena_in_proj(h, mods, g, w, b, conv_w, conv_b, *, tm=512):
    bsz, n, d = h.shape
    nout = w.shape[1]
    tm = _tile(n, tm)
    const = lambda shape: pl.BlockSpec(shape, lambda b_, i: (0,) * len(shape))
    return pl.pallas_call(
        _hyena_in_kernel,
        grid=(bsz, n // tm),
        in_specs=_halo_specs(n, tm, d) + [
            pl.BlockSpec((None, 3, d), lambda b_, i: (b_, 0, 0)),
            const((1, d)), _resident((d, nout), lambda b_, i: (0, 0)), const((1, nout)), const((3, nout)),
            const((1, nout)),
        ],
        out_specs=pl.BlockSpec((None, tm, nout), lambda b_, i: (b_, i, 0)),
        out_shape=jax.ShapeDtypeStruct((bsz, n, nout), F32),
        compiler_params=_cparams("parallel", "parallel"),
        name="hyena_in_proj",
    )(h, h, h, mods, g.reshape(1, d), w, b.reshape(1, nout), conv_w, conv_b.reshape(1, nout))


def _hdot(a, b):
    return jnp.dot(a, b, preferred_element_type=F32, precision=lax.Precision.HIGHEST)


def _filter_kernel(ft_ref, w1_ref, b1_ref, w2_ref, b2_ref, w3_ref, b3_ref, fr_ref, ld_ref, k_ref, ss_ref):
    half, i = pl.program_id(0), pl.program_id(1)
    tt = ft_ref.shape[0]
    c = k_ref.shape[-1]
    ft = ft_ref[...]
    h = jnp.sin(fr_ref[0:1, :] * (_hdot(ft, w1_ref[...]) + b1_ref[...]))
    h = jnp.sin(fr_ref[1:2, :] * (_hdot(h, w2_ref[...]) + b2_ref[...]))
    no_lag = (lax.broadcasted_iota(jnp.int32, (tt, c), 0) + (1 - half) + i) == 0

    @pl.when((half == 0) & (i == 0))
    def _():
        ss_ref[...] = jnp.zeros_like(ss_ref)

    for o in range(k_ref.shape[0]):
        k = (_hdot(h, w3_ref[o]) + b3_ref[o]) * jnp.exp(-ft[:, 0:1] * jnp.exp(ld_ref[o]))
        k = jnp.where(no_lag, 0.0, k)
        k_ref[o] = k
        ss_ref[o] += jnp.broadcast_to(jnp.sum(k * k, axis=0, keepdims=True), ss_ref.shape[1:])


def hyena_filters_time(n, f_w1, f_b1, f_w2, f_b2, f_w3, f_b3, f_freq, log_decay, *, tt=512):
    order, _, d = log_decay.shape
    fh = f_w2.shape[0]
    emb_pad = 64
    lag = np.arange(2 * n)
    lag = np.where(lag < n, lag, 2 * n - lag).astype(np.float32)
    t_norm = lag / np.float32(max(n - 1, 1))
    bands = np.linspace(1e-4, HY_BANDS - 1, HY_BANDS, dtype=np.float32)
    ang = (np.float32(2.0 * math.pi / n) * lag)[:, None] * bands[None, :]
    feats = np.zeros((2 * n, emb_pad), np.float32)
    feats[:, :HY_EMB] = np.concatenate([t_norm[:, None], np.cos(ang), -np.sin(ang)], axis=-1)
    w1 = jnp.zeros((emb_pad, fh), F32).at[:HY_EMB].set(f_w1)
    tt = _tile(n, tt)
    nt = n // tt
    const = lambda shape: pl.BlockSpec(shape, lambda hf, i: (0,) * len(shape))
    return pl.pallas_call(
        _filter_kernel,
        grid=(2, nt),
        in_specs=[
            pl.BlockSpec((tt, emb_pad), lambda hf, i: (hf * nt + i, 0)),
            const((emb_pad, fh)), const((1, fh)), const((fh, fh)), const((1, fh)),
            pl.BlockSpec((None, order, fh, d), lambda hf, i: (hf, 0, 0, 0)),
            pl.BlockSpec((order, None, 1, d), lambda hf, i: (0, hf, 0, 0)),
            const((2, fh)),
            pl.BlockSpec((order, None, 1, d), lambda hf, i: (0, hf, 0, 0)),
        ],
        out_specs=[
            pl.BlockSpec((order, tt, d), lambda hf, i: (0, hf * nt + i, 0)),
            pl.BlockSpec((order, 8, d), lambda hf, i: (0, 0, 0)),
        ],
        out_shape=[jax.ShapeDtypeStruct((order, 2 * n, d), F32), jax.ShapeDtypeStruct((order, 8, d), F32)],
        compiler_params=_cparams("arbitrary", "arbitrary"),
        name="hyena_filter_mlp",
    )(jnp.asarray(feats), w1, f_b1.reshape(1, fh), f_w2, f_b2.reshape(1, fh),
      jnp.transpose(f_w3.reshape(fh, order, 2, d), (2, 1, 0, 3)),
      f_b3.reshape(order, 2, 1, d), f_freq, log_decay.reshape(order, 2, 1, d))


def _fft_dims(n):
    nb = 1 << int(math.floor(math.log2(math.sqrt(2 * n))))
    na = 2 * n // nb
    assert na * nb == 2 * n and na % 16 == 0 and nb % 8 == 0, (n, na, nb)
    return na, nb


@functools.lru_cache(maxsize=None)
def _fft_tables(n):
    na, nb = _fft_dims(n)
    na2, nn = na // 2, 2 * n
    ka = np.arange(na, dtype=np.int64)[None, :, None]
    a = np.arange(na2, dtype=np.int64)[None, None, :]
    b = np.arange(nb, dtype=np.int64)[:, None, None]
    ang = (2.0 * np.pi / nn) * ((ka * (a * nb + b)) % nn)
    mr, mi = np.cos(ang), -np.sin(ang)
    f1 = np.concatenate([np.concatenate([mr, -mi], 2), np.concatenate([mi, mr], 2)], 1)
    mrt, mit = np.swapaxes(mr, 1, 2) / nn, np.swapaxes(mi, 1, 2) / nn
    g1 = np.concatenate([np.concatenate([mrt, mit], 2), np.concatenate([-mit, mrt], 2)], 1)
    kb = np.arange(nb, dtype=np.int64)
    ang2 = (2.0 * np.pi / nb) * ((kb[:, None] * kb[None, :]) % nb)
    er, ei = np.cos(ang2), -np.sin(ang2)
    f2 = np.block([[er, -ei], [ei, er]])
    g2 = np.block([[er, ei], [-ei, er]])
    a_all = np.arange(na, dtype=np.int64)[None, None, :]
    ang_f = (2.0 * np.pi / nn) * ((ka * (a_all * nb + b)) % nn)
    f1_real = np.concatenate([np.cos(ang_f), -np.sin(ang_f)], 1)
    as_bf16 = lambda x: x.astype(np.float32).astype(BF16)
    return dict(f1=as_bf16(f1), f1_real=as_bf16(f1_real), g1=as_bf16(g1), f2=as_bf16(f2), g2=as_bf16(g2))


def _pack_c(re, im):
    rb = lax.bitcast_convert_type(re.astype(BF16).astype(F32), jnp.uint32)
    ib = lax.bitcast_convert_type(im.astype(BF16).astype(F32), jnp.uint32)
    return rb | lax.shift_right_logical(ib, jnp.uint32(16))


def _unpack_c(w):
    re = lax.bitcast_convert_type(w & jnp.uint32(0xFFFF0000), F32)
    im = lax.bitcast_convert_type(lax.shift_left(w, jnp.uint32(16)), F32)
    return re, im


def _stack_bf16(re, im):
    return jnp.concatenate([re, im], axis=0).astype(BF16)


LANES = 128
SUB = 8


def _lane_cat(parts):
    return parts[0] if len(parts) == 1 else jnp.concatenate(parts, axis=1)


def _flat_rows(ref):
    return ref.reshape(math.prod(ref.shape[:-1]), ref.shape[-1])


def _fft_conv_kernel(x_ref, f1_ref, k_ref, f2_ref, g2_ref, g1_ref, z_ref, m_ref, skip_ref, o_ref, w_ref, *, n1, n2):
    t = pl.program_id(1)
    bsz, na2, jb, _ = x_ref.shape
    npair, na, kblk, nb = bsz // 2, 2 * na2, k_ref.shape[1], k_ref.shape[2]
    pitch = w_ref.shape[1] // na
    x2, z2, m2, o2 = _flat_rows(x_ref), _flat_rows(z_ref), _flat_rows(m_ref), _flat_rows(o_ref)
    col = lambda bi, j: pl.ds(bi * na2 * jb + j, na2, stride=jb)
    lane = lambda v, q: v[:, q * LANES:(q + 1) * LANES]

    @pl.when(t < n1)
    def _():
        for j in range(jb):
            xs = _lane_cat([jnp.concatenate([x2[col(2 * q, j), :], x2[col(2 * q + 1, j), :]], axis=0)
                            for q in range(npair)])
            res = jnp.dot(f1_ref[j], xs.astype(BF16), preferred_element_type=F32)
            packed = _pack_c(res[:na], res[na:])
            for q in range(npair):
                w_ref[q, pl.ds(t * jb + j, na, stride=pitch), :] = lane(packed, q)

    @pl.when((t >= n1) & (t < n1 + n2))
    def _():
        for k in range(kblk):
            rows = pl.ds(pl.multiple_of(((t - n1) * kblk + k) * pitch, SUB), nb)
            ar, ai = _unpack_c(_lane_cat([w_ref[q, rows, :] for q in range(npair)]))
            x = jnp.dot(f2_ref[...], _stack_bf16(ar, ai), preferred_element_type=F32)
            xr, xi = x[:nb], x[nb:]
            kr, ki = _lane_cat([k_ref[0, k]] * npair), _lane_cat([k_ref[1, k]] * npair)
            bv = jnp.dot(g2_ref[...], _stack_bf16(xr * kr - xi * ki, xr * ki + xi * kr), preferred_element_type=F32)
            packed = _pack_c(bv[:nb], bv[nb:])
            for q in range(npair):
                w_ref[q, rows, :] = lane(packed, q)

    @pl.when(t >= n1 + n2)
    def _():
        bb = t - (n1 + n2)
        skip = skip_ref[...]
        for j in range(jb):
            br, bi = _unpack_c(_lane_cat([w_ref[q, pl.ds(bb * jb + j, na, stride=pitch), :] for q in range(npair)]))
            y = jnp.dot(g1_ref[j], _stack_bf16(br, bi), preferred_element_type=F32)
            for q in range(npair):
                for r in range(2):
                    rows = col(2 * q + r, j)
                    o2[rows, :] = m2[rows, :] * (lane(y[r * na2:(r + 1) * na2], q) + skip * z2[rows, :])


def fft_long_conv(x, x_off, kf, order, z, z_off, m, m_off, skip, n, c, tables, *, jb=SUB, kblk=8):
    na, nb = _fft_dims(n)
    na2 = na // 2
    jb, kblk = min(jb, nb), min(kblk, na)
    n1, n2 = nb // jb, na // kblk
    bsz = x.shape[0]
    view = lambda a: a.reshape(a.shape[0], na2, nb, a.shape[-1])
    xo, zo, mo = x_off // LANES, z_off // LANES, m_off // LANES
    col1 = lambda t: jnp.minimum(t, n1 - 1)
    col3 = lambda t: jnp.clip(t - (n1 + n2), 0, n1 - 1)
    seq = lambda off, col: pl.BlockSpec((bsz, na2, jb, LANES), lambda ci, t: (0, 0, col(t), ci + off))
    const = lambda shape: pl.BlockSpec(shape, lambda ci, t: (0,) * len(shape))
    out = pl.pallas_call(
        functools.partial(_fft_conv_kernel, n1=n1, n2=n2),
        grid=(c // LANES, n1 + n2 + n1),
        in_specs=[
            seq(xo, col1),
            pl.BlockSpec((jb, 2 * na, na), lambda ci, t: (col1(t), 0, 0)),
            pl.BlockSpec((None, 2, kblk, nb, LANES), lambda ci, t: (order, 0, jnp.clip(t - n1, 0, n2 - 1), 0, ci)),
            const((2 * nb, 2 * nb)), const((2 * nb, 2 * nb)),
            pl.BlockSpec((jb, na, 2 * na), lambda ci, t: (col3(t), 0, 0)),
            seq(zo, col3), seq(mo, col3),
            pl.BlockSpec((1, LANES), lambda ci, t: (0, ci)),
        ],
        out_specs=seq(0, col3),
        out_shape=jax.ShapeDtypeStruct((bsz, na2, nb, c), F32),
        scratch_shapes=[pltpu.VMEM((bsz // 2, na * (nb + SUB), LANES), jnp.uint32)],
        compiler_params=_cparams("parallel", "arbitrary"),
        name="fft_long_conv",
    )(view(x), tables["f1"], kf, tables["f2"], tables["g2"], tables["g1"], view(z), view(m), skip.reshape(1, c))
    return out.reshape(bsz, n, c)


def _fft_filter_kernel(x_ref, f1_ref, f2_ref, ss_ref, o_ref, w_ref, *, n1):
    t = pl.program_id(1)
    order, na, jb, _ = x_ref.shape
    kblk, nb = o_ref.shape[2], o_ref.shape[3]
    pitch = w_ref.shape[1] // na
    x2 = _flat_rows(x_ref)
    lane = lambda v, q: v[:, q * LANES:(q + 1) * LANES]

    @pl.when(t < n1)
    def _():
        for j in range(jb):
            xs = _lane_cat([x2[pl.ds(o * na * jb + j, na, stride=jb), :] for o in range(order)])
            res = jnp.dot(f1_ref[j], xs.astype(BF16), preferred_element_type=F32)
            packed = _pack_c(res[:na], res[na:])
            for o in range(order):
                w_ref[o, pl.ds(t * jb + j, na, stride=pitch), :] = lane(packed, o)

    @pl.when(t >= n1)
    def _():
        scale = _lane_cat([lax.rsqrt(ss_ref[o, 0:1, :] + 1e-12) for o in range(order)])
        for k in range(kblk):
            rows = pl.ds(pl.multiple_of(((t - n1) * kblk + k) * pitch, SUB), nb)
            ar, ai = _unpack_c(_lane_cat([w_ref[o, rows, :] for o in range(order)]))
            x = jnp.dot(f2_ref[...], _stack_bf16(ar, ai), preferred_element_type=F32) * scale
            for o in range(order):
                o_ref[o, 0, k] = lane(x[:nb], o)
                o_ref[o, 1, k] = lane(x[nb:], o)


def fft_filter_spectrum(kt, ss, n, tables, *, jb=SUB, kblk=8):
    na, nb = _fft_dims(n)
    order, _, c = kt.shape
    jb, kblk = min(jb, nb), min(kblk, na)
    n1, n2 = nb // jb, na // kblk
    return pl.pallas_call(
        functools.partial(_fft_filter_kernel, n1=n1),
        grid=(c // LANES, n1 + n2),
        in_specs=[
            pl.BlockSpec((order, na, jb, LANES), lambda ci, t: (0, 0, jnp.minimum(t, n1 - 1), ci)),
            pl.BlockSpec((jb, 2 * na, na), lambda ci, t: (jnp.minimum(t, n1 - 1), 0, 0)),
            pl.BlockSpec((2 * nb, 2 * nb), lambda ci, t: (0, 0)),
            pl.BlockSpec((order, 8, LANES), lambda ci, t: (0, 0, ci)),
        ],
        out_specs=pl.BlockSpec((order, 2, kblk, nb, LANES), lambda ci, t: (0, 0, jnp.maximum(t - n1, 0), 0, ci)),
        out_shape=jax.ShapeDtypeStruct((order, 2, na, nb, c), F32),
        scratch_shapes=[pltpu.VMEM((order, na * (nb + SUB), LANES), jnp.uint32)],
        compiler_params=_cparams("parallel", "arbitrary"),
        name="fft_filter_spectrum",
    )(kt.reshape(order, na, nb, c), tables["f1_real"], tables["f2"], ss)


def hyena_mixer_block(h, mods, g, prm):
    bsz, n, d = h.shape
    tables = _fft_tables(n)
    proj = hyena_in_proj(h, mods, g, prm["w_in"], prm["b_in"], prm["conv_w"], prm["conv_b"])
    kt, ss = hyena_filters_time(n, prm["f_w1"], prm["f_b1"], prm["f_w2"], prm["f_b2"], prm["f_w3"],
                                prm["f_b3"], prm["f_freq"], prm["log_decay"])
    kf = fft_filter_spectrum(kt, ss, n, tables)
    z = fft_long_conv(proj, 2 * d, kf, 0, proj, 2 * d, proj, 0, prm["skip"][0], n, d, tables)
    y = fft_long_conv(z, 0, kf, 1, z, 0, proj, d, prm["skip"][1], n, d, tables)
    return linear_residual(y, h, mods[:, 2:3, :], prm["w_out"], prm["b_out"])


def _mlstm_pre_kernel(hp_ref, h_ref, hn_ref, mod_ref, g_ref, wup_ref, cw_ref, cb_ref, wq_ref, wk_ref, wv_ref,
                      wg_ref, bg_ref, q_ref, k_ref, v_ref, xc_ref, z_ref, gt_ref, *, k_scale):
    tm = h_ref.shape[0]
    inner = q_ref.shape[-1]
    grp = wq_ref.shape[-1]
    rows = jnp.concatenate([hp_ref[...], h_ref[...], hn_ref[...]], axis=0)
    u = _rms_mod(rows, g_ref[...], mod_ref[0:1, :], mod_ref[1:2, :])
    up = _bdot(u, wup_ref[...])
    xm_h = _mask_seq_ends(up[:, :inner], tm)
    xm = xm_h[HALO:HALO + tm]
    z_ref[...] = up[HALO:HALO + tm, inner:]
    xc = _silu(_conv3(xm_h, cw_ref, tm) + cb_ref[...])
    xc_ref[...] = xc
    gates = jnp.zeros(gt_ref.shape, F32) + bg_ref[...]
    for gi in range(inner // grp):
        lo, hi = gi * grp, (gi + 1) * grp
        q = _bdot(xc[:, lo:hi], wq_ref[gi])
        k = _bdot(xc[:, lo:hi], wk_ref[gi])
        v = _bdot(xm[:, lo:hi], wv_ref[gi])
        q_ref[:, lo:hi] = q.astype(q_ref.dtype)
        k_ref[:, lo:hi] = (k * k_scale).astype(k_ref.dtype)
        v_ref[:, lo:hi] = v.astype(v_ref.dtype)
        gates += (_bdot(q, wg_ref[lo:hi, :]) + _bdot(k, wg_ref[inner + lo:inner + hi, :])
                  + _bdot(v, wg_ref[2 * inner + lo:2 * inner + hi, :]))
    gt_ref[...] = gates


def mlstm_pre(h, mods, g, prm, *, tm=256):
    bsz, n, d = h.shape
    inner = prm["w_up"].shape[1] // 2
    ngate = prm["w_gate"].shape[1]
    grp = prm["wq_dense"].shape[-1]
    tm = _tile(n, tm)
    const = lambda shape: pl.BlockSpec(shape, lambda b_, i: (0,) * len(shape))
    row_spec = lambda w: pl.BlockSpec((None, tm, w), lambda b_, i: (b_, i, 0))
    big = jax.ShapeDtypeStruct((bsz, n, inner), F32)
    mm = jax.ShapeDtypeStruct((bsz, n, inner), BF16)
    return pl.pallas_call(
        functools.partial(_mlstm_pre_kernel, k_scale=float((inner // ML_HEADS) ** -0.5)),
        grid=(bsz, n // tm),
        in_specs=_halo_specs(n, tm, d) + [
            pl.BlockSpec((None, 3, d), lambda b_, i: (b_, 0, 0)),
            const((1, d)), _resident((d, 2 * inner), lambda b_, i: (0, 0)), const((3, inner)), const((1, inner)),
            const((inner // grp, grp, grp)), const((inner // grp, grp, grp)), const((inner // grp, grp, grp)),
            const((3 * inner, ngate)), const((1, ngate)),
        ],
        out_specs=[row_spec(inner)] * 5 + [row_spec(ngate)],
        out_shape=[mm] * 3 + [big] * 2 + [jax.ShapeDtypeStruct((bsz, n, ngate), F32)],
        compiler_params=_cparams("parallel", "parallel"),
        name="mlstm_pre",
    )(h, h, h, mods, g.reshape(1, d), prm["w_up"], prm["conv_w"], prm["conv_b"].reshape(1, inner),
      prm["wq_dense"], prm["wk_dense"], prm["wv_dense"], prm["w_gate"], prm["b_gate"].reshape(1, ngate))


def _dense_blockdiag(w, grp):
    nb, bs, _ = w.shape
    per = grp // bs
    eye = jnp.eye(per, dtype=w.dtype)
    dense = jnp.einsum("gmde,mn->gmdne", w.reshape(nb // per, per, bs, bs), eye)
    return dense.reshape(nb // per, grp, grp).astype(BF16)


def _split3(x):
    p1 = x.astype(BF16)
    r1 = x - p1.astype(F32)
    p2 = r1.astype(BF16)
    p3 = (r1 - p2.astype(F32)).astype(BF16)
    return p1, p2, p3


def _nt(a, b):
    return lax.dot_general(a, b, (((1,), (1,)), ((), ())), preferred_element_type=F32)


def _scan_kernel(*refs, has_init, emit_state, aug, heads_per_step):
    it = iter(refs)
    q_ref, k_ref, v_ref, gt_ref = next(it), next(it), next(it), next(it)
    allow_ref, allow_t_ref, eye_ref, neg_ref = next(it), next(it), next(it), next(it)
    c0_ref, m0_ref = (next(it), next(it)) if has_init else (None, None)
    h_ref = next(it)
    cf_ref, mf_ref = (next(it), next(it)) if emit_state else (None, None)
    c_sc, m_sc = next(it), next(it)
    c = pl.program_id(3)
    ch = q_ref.shape[0]
    dh = q_ref.shape[1] // heads_per_step

    @pl.when(c == 0)
    def _():
        if has_init:
            c_sc[...] = c0_ref[...]
            m_sc[...] = m0_ref[...]
        else:
            c_sc[...] = jnp.zeros_like(c_sc)
            m_sc[...] = jnp.zeros_like(m_sc)

    allow, allow_t, eye, neg = allow_ref[...], allow_t_ref[...], eye_ref[...], neg_ref[...]
    ones_col = (lax.broadcasted_iota(jnp.int32, (ch, aug), 1) == 0).astype(BF16)

    for hh in range(heads_per_step):
        lo, hi = hh * dh, (hh + 1) * dh
        i_row = gt_ref[hh, 0:1, :]
        f_row = gt_ref[hh, 1:2, :]
        lf_row = jnp.minimum(f_row, 0.0) - jnp.log1p(jnp.exp(-jnp.abs(f_row)))
        pieces = jnp.concatenate(list(_split3(lf_row)) + list(_split3(i_row)) + [jnp.zeros((2, ch), BF16)], axis=0)
        cum_col = _nt(allow, pieces)
        tr_col = _nt(eye, pieces)
        cum_row = jnp.dot(pieces, allow_t, preferred_element_type=F32)
        b_col = cum_col[:, 0:1] + cum_col[:, 1:2] + cum_col[:, 2:3]
        i_col = tr_col[:, 3:4] + tr_col[:, 4:5] + tr_col[:, 5:6]
        b_row = cum_row[0:1, :] + cum_row[1:2, :] + cum_row[2:3, :]
        total = jnp.sum(lf_row, axis=1, keepdims=True)
        e_row, e_col = i_row - b_row, i_col - b_col

        m_st = m_sc[hh, 0:1, 0:1]
        dmat = (b_col + e_row) + neg
        a_col = b_col + m_st
        m_j = jnp.maximum(a_col, jnp.max(dmat, axis=1, keepdims=True))
        w_inter = jnp.exp(a_col - m_j)

        q = q_ref[:, lo:hi]
        k = k_ref[:, lo:hi]
        v_aug = jnp.concatenate([v_ref[:, lo:hi], ones_col], axis=1)
        s = _nt(q, k) * jnp.exp(dmat - m_j)
        c_st = c_sc[hh]
        num = w_inter * jnp.dot(q, c_st.astype(BF16), preferred_element_type=F32) + _bdot(s, v_aug)
        qn = num[:, dh:dh + 1]
        h_ref[:, lo:hi] = num[:, :dh] / jnp.maximum(jnp.abs(qn), jnp.exp(-m_j))

        m_new = jnp.maximum(total + m_st, jnp.max(total + e_row, axis=1, keepdims=True))
        decay = jnp.exp(total + m_st - m_new)
        kw = (k.astype(F32) * jnp.exp(total + e_col - m_new)).astype(BF16)
        c_new = decay * c_st + lax.dot_general(kw, v_aug, (((0,), (0,)), ((), ())), preferred_element_type=F32)
        c_sc[hh] = c_new
        m_sc[hh] = jnp.broadcast_to(m_new, m_sc.shape[1:])

    if emit_state:
        @pl.when(c == pl.num_programs(3) - 1)
        def _():
            cf_ref[...] = c_sc[...]
            mf_ref[...] = m_sc[...]


@functools.lru_cache(maxsize=None)
def _scan_masks(ch):
    r = np.arange(ch)
    causal = (r[None, :] <= r[:, None]).astype(np.float32)
    allow = np.stack([causal, causal.T])
    neg = np.where(allow > 0, 0.0, -np.inf).astype(np.float32)
    return (allow.astype(BF16), np.transpose(allow, (0, 2, 1)).astype(BF16),
            np.eye(ch, dtype=np.float32).astype(BF16), neg)


def mlstm_scan(q, k, v, gates_t, init=None, *, emit_state=False):
    bsz, n, inner = q.shape
    nh = gates_t.shape[2]
    dh = inner // nh
    aug = 128
    hp = 2 if nh % 2 == 0 else 1
    ch = min(SCAN_CHUNK, n)
    nc = n // ch
    ceff = lambda d, c: c + d * (nc - 1 - 2 * c)
    seq = pl.BlockSpec((None, ch, hp * dh), lambda d, b, hd, c: (b, ceff(d, c), hd))
    st_c = pl.BlockSpec((None, None, hp, dh, dh + aug), lambda d, b, hd, c: (d, b, hd, 0, 0))
    st_m = pl.BlockSpec((None, None, hp, 8, 128), lambda d, b, hd, c: (d, b, hd, 0, 0))
    by_dir = pl.BlockSpec((None, ch, ch), lambda d, b, hd, c: (d, 0, 0))
    in_specs = [seq, seq, seq,
                pl.BlockSpec((None, None, hp, 2, ch), lambda d, b, hd, c: (b, d, hd, 0, ceff(d, c))),
                by_dir, by_dir, pl.BlockSpec((ch, ch), lambda d, b, hd, c: (0, 0)), by_dir]
    args = [q, k, v, gates_t, *_scan_masks(ch)]
    if init is not None:
        in_specs += [st_c, st_m]
        args += list(init)
    out_specs = [pl.BlockSpec((None, None, ch, hp * dh), lambda d, b, hd, c: (d, b, ceff(d, c), hd))]
    out_shape = [jax.ShapeDtypeStruct((2, bsz, n, inner), F32)]
    if emit_state:
        out_specs += [st_c, st_m]
        out_shape += [jax.ShapeDtypeStruct((2, bsz, nh, dh, dh + aug), F32),
                      jax.ShapeDtypeStruct((2, bsz, nh, 8, 128), F32)]
    res = pl.pallas_call(
        functools.partial(_scan_kernel, has_init=init is not None, emit_state=emit_state, aug=aug,
                          heads_per_step=hp),
        grid=(2, bsz, nh // hp, nc),
        in_specs=in_specs,
        out_specs=out_specs,
        out_shape=out_shape,
        scratch_shapes=[pltpu.VMEM((hp, dh, dh + aug), F32), pltpu.VMEM((hp, 8, 128), F32)],
        compiler_params=_cparams("parallel", "parallel", "parallel", "arbitrary"),
        name="mlstm_scan",
    )(*args)
    return (res[0], (res[1], res[2])) if emit_state else (res[0], None)


def _mlstm_post_kernel(hf_ref, hb_ref, xc_ref, z_ref, res_ref, gate_ref, ng_ref, sk_ref, wd_ref, o_ref):
    inner = xc_ref.shape[-1]
    dh = inner // ML_HEADS
    acc = jnp.zeros(o_ref.shape, F32)
    for hd in range(ML_HEADS):
        lo, hi = hd * dh, (hd + 1) * dh
        x = hf_ref[:, lo:hi] + hb_ref[:, lo:hi]
        xz = x - jnp.mean(x, axis=-1, keepdims=True)
        var = jnp.mean(xz * xz, axis=-1, keepdims=True)
        hn = xz * lax.rsqrt(var + ML_NORM_EPS) * ng_ref[:, lo:hi]
        t = (hn + sk_ref[:, lo:hi] * xc_ref[:, lo:hi]) * _silu(z_ref[:, lo:hi])
        acc += _bdot(t, wd_ref[lo:hi, :])
    o_ref[...] = res_ref[...] + gate_ref[...] * acc


def mlstm_post(hs, xc, z, h_res, gate, prm, *, tm=256):
    bsz, n, inner = xc.shape
    d = h_res.shape[-1]
    tm = _tile(n, tm)
    const = lambda shape: pl.BlockSpec(shape, lambda b_, i: (0,) * len(shape))
    return pl.pallas_call(
        _mlstm_post_kernel,
        grid=(bsz, n // tm),
        in_specs=[
            pl.BlockSpec((None, None, tm, inner), lambda b_, i: (0, b_, i, 0)),
            pl.BlockSpec((None, None, tm, inner), lambda b_, i: (1, b_, i, 0)),
            pl.BlockSpec((None, tm, inner), lambda b_, i: (b_, i, 0)),
            pl.BlockSpec((None, tm, inner), lambda b_, i: (b_, i, 0)),
            pl.BlockSpec((None, tm, d), lambda b_, i: (b_, i, 0)),
            pl.BlockSpec((None, 1, d), lambda b_, i: (b_, 0, 0)),
            const((1, inner)), const((1, inner)), _resident((inner, d), lambda b_, i: (0, 0)),
        ],
        out_specs=pl.BlockSpec((None, tm, d), lambda b_, i: (b_, i, 0)),
        out_shape=jax.ShapeDtypeStruct((bsz, n, d), F32),
        compiler_params=_cparams("parallel", "parallel"),
        name="mlstm_post",
    )(hs, hs, xc, z, h_res, gate, prm["norm_g"].reshape(1, inner), prm["skip"].reshape(1, inner), prm["w_down"])


def _gates_by_head(gates, nh):
    bsz, n, _ = gates.shape
    return jnp.transpose(gates.reshape(bsz, n, 2, 2, nh), (0, 2, 4, 3, 1))


def mlstm_mixer_block(h_ctx, h_lat, mods_ctx, mods_lat, g, prm, *, with_ctx_out):
    pc = mlstm_pre(h_ctx, mods_ctx, g, prm)
    pq = mlstm_pre(h_lat, mods_lat, g, prm)
    hs_ctx, state = mlstm_scan(pc[0], pc[1], pc[2], _gates_by_head(pc[5], ML_HEADS), emit_state=True)
    hs_lat, _ = mlstm_scan(pq[0], pq[1], pq[2], _gates_by_head(pq[5], ML_HEADS), init=state)
    out_lat = mlstm_post(hs_lat, pq[3], pq[4], h_lat, mods_lat[:, 2:3, :], prm)
    out_ctx = mlstm_post(hs_ctx, pc[3], pc[4], h_ctx, mods_ctx[:, 2:3, :], prm) if with_ctx_out else None
    return out_ctx, out_lat


def _pos_embed_2d(n_tokens, d):
    rows = n_tokens // GRID_W
    quarter = d // 4
    omega = 1.0 / (POS_BASE ** (jnp.arange(quarter, dtype=F32) / quarter))
    ang_r = jnp.arange(rows, dtype=F32)[:, None] * omega
    ang_c = jnp.arange(GRID_W, dtype=F32)[:, None] * omega
    row_emb = jnp.concatenate([jnp.sin(ang_r), jnp.cos(ang_r)], axis=-1)
    col_emb = jnp.concatenate([jnp.sin(ang_c), jnp.cos(ang_c)], axis=-1)
    emb = jnp.concatenate([
        jnp.broadcast_to(row_emb[:, None, :], (rows, GRID_W, d // 2)),
        jnp.broadcast_to(col_emb[None, :, :], (rows, GRID_W, d // 2))], axis=-1)
    return emb.reshape(rows * GRID_W, d)


def kernel(x, c, ctx, c_ctx, ada_w, ada_b, norm_g, final_g, ffn_w_in, ffn_w_out, hy_w_in, hy_b_in, hy_conv_w, hy_conv_b, hy_f_w1, hy_f_b1, hy_f_w2, hy_f_b2, hy_f_w3, hy_f_b3, hy_f_freq, hy_log_decay, hy_skip, hy_w_out, hy_b_out, ml_w_up, ml_conv_w, ml_conv_b, ml_w_q, ml_w_k, ml_w_v, ml_w_gate, ml_b_gate, ml_norm_g, ml_skip, ml_w_down):
    bsz, n_lat, d = x.shape
    n_ctx = ctx.shape[1]
    depth = ada_w.shape[0]
    n_mixers = 2

    cond_rows = 8
    cond = jnp.zeros((cond_rows, d), F32).at[:bsz].set(c).at[bsz].set(c_ctx)
    mods_all = ada_modulation(cond, ada_w, ada_b).reshape(depth, cond_rows, N_MOD, d)

    ffn_w_in = ffn_w_in.astype(BF16)
    ffn_w_out = ffn_w_out.astype(BF16)
    pos = _pos_embed_2d(n_lat, d)

    h_lat = x
    h_ctx = ctx.reshape(1, bsz * n_ctx, d)
    for l in range(depth):
        last = l == depth - 1
        j = l // n_mixers
        m_lat = mods_all[l, :bsz]
        m_ctx = mods_all[l, bsz:bsz + 1]
        m_ctx_b = jnp.broadcast_to(m_ctx, (bsz, N_MOD, d))

        h_lat = ffn_block(h_lat, m_lat[:, 0:3], norm_g[l, 0], ffn_w_in, ffn_w_out, widx=(l, 0),
                          pos=pos if l == 0 else None)
        h_ctx = ffn_block(h_ctx, m_ctx[:, 0:3], norm_g[l, 0], ffn_w_in, ffn_w_out, widx=(l, 0))

        h_ctx = h_ctx.reshape(bsz, n_ctx, d)
        if l % n_mixers == 0:
            prm = dict(w_in=hy_w_in[j].astype(BF16), b_in=hy_b_in[j], conv_w=hy_conv_w[j], conv_b=hy_conv_b[j],
                       f_w1=hy_f_w1[j], f_b1=hy_f_b1[j], f_w2=hy_f_w2[j], f_b2=hy_f_b2[j], f_w3=hy_f_w3[j],
                       f_b3=hy_f_b3[j], f_freq=hy_f_freq[j], log_decay=hy_log_decay[j], skip=hy_skip[j],
                       w_out=hy_w_out[j].astype(BF16), b_out=hy_b_out[j])
            h_lat = hyena_mixer_block(h_lat, m_lat[:, 3:6], norm_g[l, 1], prm)
            if not last:
                h_ctx = hyena_mixer_block(h_ctx, m_ctx_b[:, 3:6], norm_g[l, 1], prm)
        else:
            prm = dict(w_up=ml_w_up[j].astype(BF16), conv_w=ml_conv_w[j], conv_b=ml_conv_b[j],
                       wq_dense=_dense_blockdiag(ml_w_q[j], ML_QKV_GROUP),
                       wk_dense=_dense_blockdiag(ml_w_k[j], ML_QKV_GROUP),
                       wv_dense=_dense_blockdiag(ml_w_v[j], ML_QKV_GROUP),
                       w_gate=ml_w_gate[j].astype(BF16), b_gate=ml_b_gate[j], norm_g=ml_norm_g[j],
                       skip=ml_skip[j], w_down=ml_w_down[j].astype(BF16))
            new_ctx, h_lat = mlstm_mixer_block(h_ctx, h_lat, m_ctx_b[:, 3:6], m_lat[:, 3:6], norm_g[l, 1], prm,
                                               with_ctx_out=not last)
            h_ctx = h_ctx if last else new_ctx
        h_ctx = h_ctx.reshape(1, bsz * n_ctx, d)

        h_lat = ffn_block(h_lat, m_lat[:, 6:9], norm_g[l, 2], ffn_w_in, ffn_w_out, widx=(l, 1),
                          final_g=final_g if last else None)
        if not last:
            h_ctx = ffn_block(h_ctx, m_ctx[:, 6:9], norm_g[l, 2], ffn_w_in, ffn_w_out, widx=(l, 1))
    return h_lat
```

```python
import functools
import math

import jax
import jax.numpy as jnp
import numpy as np
from jax import lax
from jax.experimental import pallas as pl
from jax.experimental.pallas import tpu as pltpu

F32 = jnp.float32
BF16 = jnp.bfloat16

NORM_EPS = 1e-6
GRID_W = 64
POS_BASE = 10000.0
N_MOD = 9
HY_EMB = 33
HY_BANDS = (HY_EMB - 1) // 2
ML_HEADS = 4
ML_NORM_EPS = 1e-5
ML_QKV_GROUP = 256
SCAN_CHUNK = 256
HALO = 8
CONV_COLS = 256

V7X_VMEM_LIMIT = 56 * 1024 * 1024


def _cparams(*sem):
    return pltpu.CompilerParams(dimension_semantics=sem, vmem_limit_bytes=V7X_VMEM_LIMIT)


def _bdot(a, b):
    return jnp.dot(a.astype(BF16), b.astype(BF16), preferred_element_type=F32)


def _rms_mod(h, g, shift, scale):
    y = h * lax.rsqrt(jnp.mean(h * h, axis=-1, keepdims=True) + NORM_EPS)
    return (y * g) * (1.0 + scale) + shift


def _silu(x):
    return x * (1.0 / (1.0 + jnp.exp(-x)))


def _tile(n, pref):
    if n <= pref:
        return n
    for t in range(pref, 7, -1):
        if n % t == 0 and t % 8 == 0:
            return t
    return n


def _mod_kernel(c_ref, w_ref, b_ref, o_ref):
    o_ref[...] = _bdot(_silu(c_ref[...]), w_ref[...]) + b_ref[...]


def ada_modulation(cond, ada_w, ada_b):
    n_layers, d, n_out = ada_w.shape
    r = cond.shape[0]
    tn = _tile(n_out, 2304) if n_out % 128 == 0 else n_out
    return pl.pallas_call(
        _mod_kernel,
        grid=(n_layers, n_out // tn),
        in_specs=[
            pl.BlockSpec((r, d), lambda l, j: (0, 0)),
            pl.BlockSpec((None, d, tn), lambda l, j: (l, 0, j)),
            pl.BlockSpec((None, 1, tn), lambda l, j: (l, 0, j)),
        ],
        out_specs=pl.BlockSpec((None, r, tn), lambda l, j: (l, 0, j)),
        out_shape=jax.ShapeDtypeStruct((n_layers, r, n_out), F32),
        compiler_params=_cparams("parallel", "parallel"),
        name="ada_modulation",
    )(cond, ada_w, ada_b.reshape(n_layers, 1, n_out))


def _ffn_kernel(*refs, has_pos, final):
    it = iter(refs)
    h_ref = next(it)
    pos_ref = next(it) if has_pos else None
    mod_ref, g_ref, wg_ref, wv_ref, wo_ref = next(it), next(it), next(it), next(it), next(it)
    fg_ref = next(it) if final else None
    o_ref = next(it)
    h = h_ref[...]
    if has_pos:
        h = h + pos_ref[...]
    u = _rms_mod(h, g_ref[...], mod_ref[0:1, :], mod_ref[1:2, :]).astype(BF16)
    gate = jnp.dot(u, wg_ref[...], preferred_element_type=F32)
    val = jnp.dot(u, wv_ref[...], preferred_element_type=F32)
    out = h + (0.5 * mod_ref[2:3, :]) * _bdot(_silu(gate) * val, wo_ref[...])
    if final:
        out = out * lax.rsqrt(jnp.mean(out * out, axis=-1, keepdims=True) + NORM_EPS) * fg_ref[...]
    o_ref[...] = out


def _resident(shape, index_map):
    return pl.BlockSpec(shape, index_map, pipeline_mode=pl.Buffered(1))


def ffn_block(h, mods, g, w_in, w_out, *, widx=(), pos=None, final_g=None, tm=512):
    bsz, n, d = h.shape
    ff = w_out.shape[-2]
    tm = _tile(n, tm)
    has_pos, final = pos is not None, final_g is not None
    lead = (None,) * len(widx)
    in_specs = [pl.BlockSpec((None, tm, d), lambda b, i: (b, i, 0))]
    args = [h]
    if has_pos:
        in_specs.append(pl.BlockSpec((tm, d), lambda b, i: (i, 0)))
        args.append(pos)
    in_specs += [
        pl.BlockSpec((None, 3, d), lambda b, i: (b, 0, 0)),
        pl.BlockSpec((1, d), lambda b, i: (0, 0)),
        _resident(lead + (d, ff), lambda b, i: widx + (0, 0)),
        _resident(lead + (d, ff), lambda b, i: widx + (0, 1)),
        _resident(lead + (ff, d), lambda b, i: widx + (0, 0)),
    ]
    args += [mods, g.reshape(1, d), w_in, w_in, w_out]
    if final:
        in_specs.append(pl.BlockSpec((1, d), lambda b, i: (0, 0)))
        args.append(final_g.reshape(1, d))
    return pl.pallas_call(
        functools.partial(_ffn_kernel, has_pos=has_pos, final=final),
        grid=(bsz, n // tm),
        in_specs=in_specs,
        out_specs=pl.BlockSpec((None, tm, d), lambda b, i: (b, i, 0)),
        out_shape=jax.ShapeDtypeStruct((bsz, n, d), F32),
        compiler_params=_cparams("parallel", "parallel"),
        name="ffn_block",
    )(*args)


def _linres_kernel(y_ref, h_ref, gate_ref, w_ref, b_ref, o_ref):
    o_ref[...] = h_ref[...] + gate_ref[...] * (_bdot(y_ref[...], w_ref[...]) + b_ref[...])


def linear_residual(y, h, gate, w, b, *, tm=512):
    bsz, n, kdim = y.shape
    d = h.shape[-1]
    tm = _tile(n, tm)
    return pl.pallas_call(
        _linres_kernel,
        grid=(bsz, n // tm),
        in_specs=[
            pl.BlockSpec((None, tm, kdim), lambda b_, i: (b_, i, 0)),
            pl.BlockSpec((None, tm, d), lambda b_, i: (b_, i, 0)),
            pl.BlockSpec((None, 1, d), lambda b_, i: (b_, 0, 0)),
            _resident((kdim, d), lambda b_, i: (0, 0)),
            pl.BlockSpec((1, d), lambda b_, i: (0, 0)),
        ],
        out_specs=pl.BlockSpec((None, tm, d), lambda b_, i: (b_, i, 0)),
        out_shape=jax.ShapeDtypeStruct((bsz, n, d), F32),
        compiler_params=_cparams("parallel", "parallel"),
        name="linear_residual",
    )(y, h, gate, w, b.reshape(1, d))


def _halo_specs(n, tm, d):
    hb, last = tm // HALO, n // HALO - 1
    return [
        pl.BlockSpec((None, HALO, d), lambda b, i: (b, jnp.maximum(i * hb - 1, 0), 0)),
        pl.BlockSpec((None, tm, d), lambda b, i: (b, i, 0)),
        pl.BlockSpec((None, HALO, d), lambda b, i: (b, jnp.minimum((i + 1) * hb, last), 0)),
    ]


def _mask_seq_ends(p, tm):
    i, ni = pl.program_id(1), pl.num_programs(1)
    head = jnp.where(i == 0, 0.0, p[:HALO])
    tail = jnp.where(i == ni - 1, 0.0, p[tm + HALO:])
    return jnp.concatenate([head, p[HALO:tm + HALO], tail], axis=0)


def _conv3(p, cw_ref, tm):
    rows = p.shape[0]
    prev = pltpu.roll(p, 1, 0)[HALO:HALO + tm]
    nxt = pltpu.roll(p, rows - 1, 0)[HALO:HALO + tm]
    return prev * cw_ref[0:1, :] + p[HALO:HALO + tm] * cw_ref[1:2, :] + nxt * cw_ref[2:3, :]


def _hyena_in_kernel(hp_ref, h_ref, hn_ref, mod_ref, g_ref, w_ref, b_ref, cw_ref, cb_ref, o_ref):
    tm = h_ref.shape[0]
    rows = jnp.concatenate([hp_ref[...], h_ref[...], hn_ref[...]], axis=0)
    u = _rms_mod(rows, g_ref[...], mod_ref[0:1, :], mod_ref[1:2, :]).astype(BF16)
    nout = o_ref.shape[-1]
    step = CONV_COLS if nout % CONV_COLS == 0 else nout
    for lo in range(0, nout, step):
        cs = slice(lo, lo + step)
        p = _mask_seq_ends(jnp.dot(u, w_ref[:, cs], preferred_element_type=F32) + b_ref[:, cs], tm)
        o_ref[:, cs] = _conv3(p, cw_ref.at[:, cs], tm) + cb_ref[:, cs]


def hyena_in_proj(h, mods, g, w, b, conv_w, conv_b, *, tm=512):
    bsz, n, d = h.shape
    nout = w.shape[1]
    tm = _tile(n, tm)
    const = lambda shape: pl.BlockSpec(shape, lambda b_, i: (0,) * len(shape))
    return pl.pallas_call(
        _hyena_in_kernel,
        grid=(bsz, n // tm),
        in_specs=_halo_specs(n, tm, d) + [
            pl.BlockSpec((None, 3, d), lambda b_, i: (b_, 0, 0)),
            const((1, d)), _resident((d, nout), lambda b_, i: (0, 0)), const((1, nout)), const((3, nout)),
            const((1, nout)),
        ],
        out_specs=pl.BlockSpec((None, tm, nout), lambda b_, i: (b_, i, 0)),
        out_shape=jax.ShapeDtypeStruct((bsz, n, nout), F32),
        compiler_params=_cparams("parallel", "parallel"),
        name="hyena_in_proj",
    )(h, h, h, mods, g.reshape(1, d), w, b.reshape(1, nout), conv_w, conv_b.reshape(1, nout))


def _hdot(a, b):
    return jnp.dot(a, b, preferred_element_type=F32, precision=lax.Precision.HIGHEST)


def _filter_kernel(ft_ref, w1_ref, b1_ref, w2_ref, b2_ref, w3_ref, b3_ref, fr_ref, ld_ref, k_ref, ss_ref):
    half, i = pl.program_id(0), pl.program_id(1)
    tt = ft_ref.shape[0]
    c = k_ref.shape[-1]
    ft = ft_ref[...]
    h = jnp.sin(fr_ref[0:1, :] * (_hdot(ft, w1_ref[...]) + b1_ref[...]))
    h = jnp.sin(fr_ref[1:2, :] * (_hdot(h, w2_ref[...]) + b2_ref[...]))
    no_lag = (lax.broadcasted_iota(jnp.int32, (tt, c), 0) + (1 - half) + i) == 0

    @pl.when((half == 0) & (i == 0))
    def _():
        ss_ref[...] = jnp.zeros_like(ss_ref)

    for o in range(k_ref.shape[0]):
        k = (_hdot(h, w3_ref[o]) + b3_ref[o]) * jnp.exp(-ft[:, 0:1] * jnp.exp(ld_ref[o]))
        k = jnp.where(no_lag, 0.0, k)
        k_ref[o] = k
        ss_ref[o] += jnp.broadcast_to(jnp.sum(k * k, axis=0, keepdims=True), ss_ref.shape[1:])


def hyena_filters_time(n, f_w1, f_b1, f_w2, f_b2, f_w3, f_b3, f_freq, log_decay, *, tt=512):
    order, _, d = log_decay.shape
    fh = f_w2.shape[0]
    emb_pad = 64
    lag = np.arange(2 * n)
    lag = np.where(lag < n, lag, 2 * n - lag).astype(np.float32)
    t_norm = lag / np.float32(max(n - 1, 1))
    bands = np.linspace(1e-4, HY_BANDS - 1, HY_BANDS, dtype=np.float32)
    ang = (np.float32(2.0 * math.pi / n) * lag)[:, None] * bands[None, :]
    feats = np.zeros((2 * n, emb_pad), np.float32)
    feats[:, :HY_EMB] = np.concatenate([t_norm[:, None], np.cos(ang), -np.sin(ang)], axis=-1)
    w1 = jnp.zeros((emb_pad, fh), F32).at[:HY_EMB].set(f_w1)
    tt = _tile(n, tt)
    nt = n // tt
    const = lambda shape: pl.BlockSpec(shape, lambda hf, i: (0,) * len(shape))
    return pl.pallas_call(
        _filter_kernel,
        grid=(2, nt),
        in_specs=[
            pl.BlockSpec((tt, emb_pad), lambda hf, i: (hf * nt + i, 0)),
            const((emb_pad, fh)), const((1, fh)), const((fh, fh)), const((1, fh)),
            pl.BlockSpec((None, order, fh, d), lambda hf, i: (hf, 0, 0, 0)),
            pl.BlockSpec((order, None, 1, d), lambda hf, i: (0, hf, 0, 0)),
            const((2, fh)),
            pl.BlockSpec((order, None, 1, d), lambda hf, i: (0, hf, 0, 0)),
        ],
        out_specs=[
            pl.BlockSpec((order, tt, d), lambda hf, i: (0, hf * nt + i, 0)),
            pl.BlockSpec((order, 8, d), lambda hf, i: (0, 0, 0)),
        ],
        out_shape=[jax.ShapeDtypeStruct((order, 2 * n, d), F32), jax.ShapeDtypeStruct((order, 8, d), F32)],
        compiler_params=_cparams("arbitrary", "arbitrary"),
        name="hyena_filter_mlp",
    )(jnp.asarray(feats), w1, f_b1.reshape(1, fh), f_w2, f_b2.reshape(1, fh),
      jnp.transpose(f_w3.reshape(fh, order, 2, d), (2, 1, 0, 3)),
      f_b3.reshape(order, 2, 1, d), f_freq, log_decay.reshape(order, 2, 1, d))


def _fft_dims(n):
    nb = 1 << int(math.floor(math.log2(math.sqrt(2 * n))))
    na = 2 * n // nb
    assert na * nb == 2 * n and na % 16 == 0 and nb % 8 == 0, (n, na, nb)
    return na, nb


@functools.lru_cache(maxsize=None)
def _fft_tables(n):
    na, nb = _fft_dims(n)
    na2, nn = na // 2, 2 * n
    ka = np.arange(na, dtype=np.int64)[None, :, None]
    a = np.arange(na2, dtype=np.int64)[None, None, :]
    b = np.arange(nb, dtype=np.int64)[:, None, None]
    ang = (2.0 * np.pi / nn) * ((ka * (a * nb + b)) % nn)
    mr, mi = np.cos(ang), -np.sin(ang)
    f1 = np.concatenate([np.concatenate([mr, -mi], 2), np.concatenate([mi, mr], 2)], 1)
    mrt, mit = np.swapaxes(mr, 1, 2) / nn, np.swapaxes(mi, 1, 2) / nn
    g1 = np.concatenate([np.concatenate([mrt, mit], 2), np.concatenate([-mit, mrt], 2)], 1)
    kb = np.arange(nb, dtype=np.int64)
    ang2 = (2.0 * np.pi / nb) * ((kb[:, None] * kb[None, :]) % nb)
    er, ei = np.cos(ang2), -np.sin(ang2)
    f2 = np.block([[er, -ei], [ei, er]])
    g2 = np.block([[er, ei], [-ei, er]])
    a_all = np.arange(na, dtype=np.int64)[None, None, :]
    ang_f = (2.0 * np.pi / nn) * ((ka * (a_all * nb + b)) % nn)
    f1_real = np.concatenate([np.cos(ang_f), -np.sin(ang_f)], 1)
    as_bf16 = lambda x: x.astype(np.float32).astype(BF16)
    return dict(f1=as_bf16(f1), f1_real=as_bf16(f1_real), g1=as_bf16(g1), f2=as_bf16(f2), g2=as_bf16(g2))


def _pack_c(re, im):
    rb = lax.bitcast_convert_type(re.astype(BF16).astype(F32), jnp.uint32)
    ib = lax.bitcast_convert_type(im.astype(BF16).astype(F32), jnp.uint32)
    return rb | lax.shift_right_logical(ib, jnp.uint32(16))


def _unpack_c(w):
    re = lax.bitcast_convert_type(w & jnp.uint32(0xFFFF0000), F32)
    im = lax.bitcast_convert_type(lax.shift_left(w, jnp.uint32(16)), F32)
    return re, im


def _stack_bf16(re, im):
    return jnp.concatenate([re, im], axis=0).astype(BF16)


LANES = 128
SUB = 8


def _lane_cat(parts):
    return parts[0] if len(parts) == 1 else jnp.concatenate(parts, axis=1)


def _flat_rows(ref):
    return ref.reshape(math.prod(ref.shape[:-1]), ref.shape[-1])


def _fft_conv_kernel(x_ref, f1_ref, k_ref, f2_ref, g2_ref, g1_ref, z_ref, m_ref, skip_ref, o_ref, w_ref, *, n1, n2):
    t = pl.program_id(1)
    bsz, na2, jb, _ = x_ref.shape
    npair, na, kblk, nb = bsz // 2, 2 * na2, k_ref.shape[1], k_ref.shape[2]
    pitch = w_ref.shape[1] // na
    x2, z2, m2, o2 = _flat_rows(x_ref), _flat_rows(z_ref), _flat_rows(m_ref), _flat_rows(o_ref)
    col = lambda bi, j: pl.ds(bi * na2 * jb + j, na2, stride=jb)
    lane = lambda v, q: v[:, q * LANES:(q + 1) * LANES]

    @pl.when(t < n1)
    def _():
        for j in range(jb):
            xs = _lane_cat([jnp.concatenate([x2[col(2 * q, j), :], x2[col(2 * q + 1, j), :]], axis=0)
                            for q in range(npair)])
            res = jnp.dot(f1_ref[j], xs.astype(BF16), preferred_element_type=F32)
            packed = _pack_c(res[:na], res[na:])
            for q in range(npair):
                w_ref[q, pl.ds(t * jb + j, na, stride=pitch), :] = lane(packed, q)

    @pl.when((t >= n1) & (t < n1 + n2))
    def _():
        for k in range(kblk):
            rows = pl.ds(pl.multiple_of(((t - n1) * kblk + k) * pitch, SUB), nb)
            ar, ai = _unpack_c(_lane_cat([w_ref[q, rows, :] for q in range(npair)]))
            x = jnp.dot(f2_ref[...], _stack_bf16(ar, ai), preferred_element_type=F32)
            xr, xi = x[:nb], x[nb:]
            kr, ki = _lane_cat([k_ref[0, k]] * npair), _lane_cat([k_ref[1, k]] * npair)
            bv = jnp.dot(g2_ref[...], _stack_bf16(xr * kr - xi * ki, xr * ki + xi * kr), preferred_element_type=F32)
            packed = _pack_c(bv[:nb], bv[nb:])
            for q in range(npair):
                w_ref[q, rows, :] = lane(packed, q)

    @pl.when(t >= n1 + n2)
    def _():
        bb = t - (n1 + n2)
        skip = skip_ref[...]
        for j in range(jb):
            br, bi = _unpack_c(_lane_cat([w_ref[q, pl.ds(bb * jb + j, na, stride=pitch), :] for q in range(npair)]))
            y = jnp.dot(g1_ref[j], _stack_bf16(br, bi), preferred_element_type=F32)
            for q in range(npair):
                for r in range(2):
                    rows = col(2 * q + r, j)
                    o2[rows, :] = m2[rows, :] * (lane(y[r * na2:(r + 1) * na2], q) + skip * z2[rows, :])


def fft_long_conv(x, x_off, kf, order, z, z_off, m, m_off, skip, n, c, tables, *, jb=SUB, kblk=8):
    na, nb = _fft_dims(n)
    na2 = na // 2
    jb, kblk = min(jb, nb), min(kblk, na)
    n1, n2 = nb // jb, na // kblk
    bsz = x.shape[0]
    view = lambda a: a.reshape(a.shape[0], na2, nb, a.shape[-1])
    xo, zo, mo = x_off // LANES, z_off // LANES, m_off // LANES
    col1 = lambda t: jnp.minimum(t, n1 - 1)
    col3 = lambda t: jnp.clip(t - (n1 + n2), 0, n1 - 1)
    seq = lambda off, col: pl.BlockSpec((bsz, na2, jb, LANES), lambda ci, t: (0, 0, col(t), ci + off))
    const = lambda shape: pl.BlockSpec(shape, lambda ci, t: (0,) * len(shape))
    out = pl.pallas_call(
        functools.partial(_fft_conv_kernel, n1=n1, n2=n2),
        grid=(c // LANES, n1 + n2 + n1),
        in_specs=[
            seq(xo, col1),
            pl.BlockSpec((jb, 2 * na, na), lambda ci, t: (col1(t), 0, 0)),
            pl.BlockSpec((None, 2, kblk, nb, LANES), lambda ci, t: (order, 0, jnp.clip(t - n1, 0, n2 - 1), 0, ci)),
            const((2 * nb, 2 * nb)), const((2 * nb, 2 * nb)),
            pl.BlockSpec((jb, na, 2 * na), lambda ci, t: (col3(t), 0, 0)),
            seq(zo, col3), seq(mo, col3),
            pl.BlockSpec((1, LANES), lambda ci, t: (0, ci)),
        ],
        out_specs=seq(0, col3),
        out_shape=jax.ShapeDtypeStruct((bsz, na2, nb, c), F32),
        scratch_shapes=[pltpu.VMEM((bsz // 2, na * (nb + SUB), LANES), jnp.uint32)],
        compiler_params=_cparams("parallel", "arbitrary"),
        name="fft_long_conv",
    )(view(x), tables["f1"], kf, tables["f2"], tables["g2"], tables["g1"], view(z), view(m), skip.reshape(1, c))
    return out.reshape(bsz, n, c)


def _fft_filter_kernel(x_ref, f1_ref, f2_ref, ss_ref, o_ref, w_ref, *, n1):
    t = pl.program_id(1)
    order, na, jb, _ = x_ref.shape
    kblk, nb = o_ref.shape[2], o_ref.shape[3]
    pitch = w_ref.shape[1] // na
    x2 = _flat_rows(x_ref)
    lane = lambda v, q: v[:, q * LANES:(q + 1) * LANES]

    @pl.when(t < n1)
    def _():
        for j in range(jb):
            xs = _lane_cat([x2[pl.ds(o * na * jb + j, na, stride=jb), :] for o in range(order)])
            res = jnp.dot(f1_ref[j], xs.astype(BF16), preferred_element_type=F32)
            packed = _pack_c(res[:na], res[na:])
            for o in range(order):
                w_ref[o, pl.ds(t * jb + j, na, stride=pitch), :] = lane(packed, o)

    @pl.when(t >= n1)
    def _():
        scale = _lane_cat([lax.rsqrt(ss_ref[o, 0:1, :] + 1e-12) for o in range(order)])
        for k in range(kblk):
            rows = pl.ds(pl.multiple_of(((t - n1) * kblk + k) * pitch, SUB), nb)
            ar, ai = _unpack_c(_lane_cat([w_ref[o, rows, :] for o in range(order)]))
            x = jnp.dot(f2_ref[...], _stack_bf16(ar, ai), preferred_element_type=F32) * scale
            for o in range(order):
                o_ref[o, 0, k] = lane(x[:nb], o)
                o_ref[o, 1, k] = lane(x[nb:], o)


def fft_filter_spectrum(kt, ss, n, tables, *, jb=SUB, kblk=8):
    na, nb = _fft_dims(n)
    order, _, c = kt.shape
    jb, kblk = min(jb, nb), min(kblk, na)
    n1, n2 = nb // jb, na // kblk
    return pl.pallas_call(
        functools.partial(_fft_filter_kernel, n1=n1),
        grid=(c // LANES, n1 + n2),
        in_specs=[
            pl.BlockSpec((order, na, jb, LANES), lambda ci, t: (0, 0, jnp.minimum(t, n1 - 1), ci)),
            pl.BlockSpec((jb, 2 * na, na), lambda ci, t: (jnp.minimum(t, n1 - 1), 0, 0)),
            pl.BlockSpec((2 * nb, 2 * nb), lambda ci, t: (0, 0)),
            pl.BlockSpec((order, 8, LANES), lambda ci, t: (0, 0, ci)),
        ],
        out_specs=pl.BlockSpec((order, 2, kblk, nb, LANES), lambda ci, t: (0, 0, jnp.maximum(t - n1, 0), 0, ci)),
        out_shape=jax.ShapeDtypeStruct((order, 2, na, nb, c), F32),
        scratch_shapes=[pltpu.VMEM((order, na * (nb + SUB), LANES), jnp.uint32)],
        compiler_params=_cparams("parallel", "arbitrary"),
        name="fft_filter_spectrum",
    )(kt.reshape(order, na, nb, c), tables["f1_real"], tables["f2"], ss)


def hyena_mixer_block(h, mods, g, prm):
    bsz, n, d = h.shape
    tables = _fft_tables(n)
    proj = hyena_in_proj(h, mods, g, prm["w_in"], prm["b_in"], prm["conv_w"], prm["conv_b"])
    kt, ss = hyena_filters_time(n, prm["f_w1"], prm["f_b1"], prm["f_w2"], prm["f_b2"], prm["f_w3"],
                                prm["f_b3"], prm["f_freq"], prm["log_decay"])
    kf = fft_filter_spectrum(kt, ss, n, tables)
    z = fft_long_conv(proj, 2 * d, kf, 0, proj, 2 * d, proj, 0, prm["skip"][0], n, d, tables)
    y = fft_long_conv(z, 0, kf, 1, z, 0, proj, d, prm["skip"][1], n, d, tables)
    return linear_residual(y, h, mods[:, 2:3, :], prm["w_out"], prm["b_out"])


def _mlstm_pre_kernel(hp_ref, h_ref, hn_ref, mod_ref, g_ref, wup_ref, cw_ref, cb_ref, wq_ref, wk_ref, wkt_ref, wv_ref,
                      wg_ref, bg_ref, q_ref, kt_ref, v_ref, xc_ref, z_ref, gt_ref, *, k_scale):
    tm = h_ref.shape[0]
    inner = q_ref.shape[-1]
    grp = wq_ref.shape[-1]
    rows = jnp.concatenate([hp_ref[...], h_ref[...], hn_ref[...]], axis=0)
    u = _rms_mod(rows, g_ref[...], mod_ref[0:1, :], mod_ref[1:2, :])
    up = _bdot(u, wup_ref[...])
    xm_h = _mask_seq_ends(up[:, :inner], tm)
    z_ref[...] = up[HALO:HALO + tm, inner:].astype(z_ref.dtype)
    xc_f = _silu(_conv3(xm_h, cw_ref, tm) + cb_ref[...])
    xc_ref[...] = xc_f.astype(xc_ref.dtype)
    xc_all, xm_all = xc_f.astype(BF16), xm_h[HALO:HALO + tm].astype(BF16)
    gates = jnp.zeros(gt_ref.shape, F32) + bg_ref[...]
    for gi in range(inner // grp):
        lo, hi = gi * grp, (gi + 1) * grp
        xc, xm = xc_all[:, lo:hi], xm_all[:, lo:hi]
        q = jnp.dot(xc, wq_ref[gi], preferred_element_type=F32)
        k = jnp.dot(xc, wk_ref[gi], preferred_element_type=F32)
        v = jnp.dot(xm, wv_ref[gi], preferred_element_type=F32)
        q_ref[:, lo:hi] = q.astype(q_ref.dtype)
        kt_ref[lo:hi, :] = (_nt(wkt_ref[gi], xc) * k_scale).astype(kt_ref.dtype)
        v_ref[:, lo:hi] = v.astype(v_ref.dtype)
        gates += (_bdot(q, wg_ref[lo:hi, :]) + _bdot(k, wg_ref[inner + lo:inner + hi, :])
                  + _bdot(v, wg_ref[2 * inner + lo:2 * inner + hi, :]))
    gt_ref[...] = gates


def mlstm_pre(h, mods, g, prm, *, tm=256):
    bsz, n, d = h.shape
    inner = prm["w_up"].shape[1] // 2
    ngate = prm["w_gate"].shape[1]
    grp = prm["wq_dense"].shape[-1]
    tm = _tile(n, tm)
    const = lambda shape: pl.BlockSpec(shape, lambda b_, i: (0,) * len(shape))
    row_spec = lambda w: pl.BlockSpec((None, tm, w), lambda b_, i: (b_, i, 0))
    mm = jax.ShapeDtypeStruct((bsz, n, inner), BF16)
    return pl.pallas_call(
        functools.partial(_mlstm_pre_kernel, k_scale=float((inner // ML_HEADS) ** -0.5)),
        grid=(bsz, n // tm),
        in_specs=_halo_specs(n, tm, d) + [
            pl.BlockSpec((None, 3, d), lambda b_, i: (b_, 0, 0)),
            const((1, d)), _resident((d, 2 * inner), lambda b_, i: (0, 0)), const((3, inner)), const((1, inner)),
            const((inner // grp, grp, grp)), const((inner // grp, grp, grp)), const((inner // grp, grp, grp)),
            const((inner // grp, grp, grp)), const((3 * inner, ngate)), const((1, ngate)),
        ],
        out_specs=[row_spec(inner), pl.BlockSpec((None, inner, tm), lambda b_, i: (b_, 0, i))]
                  + [row_spec(inner)] * 3 + [row_spec(ngate)],
        out_shape=[mm, jax.ShapeDtypeStruct((bsz, inner, n), BF16), mm, mm, mm,
                   jax.ShapeDtypeStruct((bsz, n, ngate), F32)],
        compiler_params=_cparams("parallel", "parallel"),
        name="mlstm_pre",
    )(h, h, h, mods, g.reshape(1, d), prm["w_up"], prm["conv_w"], prm["conv_b"].reshape(1, inner),
      prm["wq_dense"], prm["wk_dense"], jnp.swapaxes(prm["wk_dense"], 1, 2), prm["wv_dense"], prm["w_gate"],
      prm["b_gate"].reshape(1, ngate))


def _dense_blockdiag(w, grp):
    nb, bs, _ = w.shape
    per = grp // bs
    eye = jnp.eye(per, dtype=w.dtype)
    dense = jnp.einsum("gmde,mn->gmdne", w.reshape(nb // per, per, bs, bs), eye)
    return dense.reshape(nb // per, grp, grp).astype(BF16)


def _split3(x):
    p1 = x.astype(BF16)
    r1 = x - p1.astype(F32)
    p2 = r1.astype(BF16)
    p3 = (r1 - p2.astype(F32)).astype(BF16)
    return p1, p2, p3


def _nt(a, b):
    return lax.dot_general(a, b, (((1,), (1,)), ((), ())), preferred_element_type=F32)


def _scan_kernel(*refs, has_init, emit_state, heads_per_step):
    it = iter(refs)
    q_ref, kt_ref, v_ref, gt_ref = next(it), next(it), next(it), next(it)
    allow_ref, allow_t_ref, neg_ref = next(it), next(it), next(it)
    init_refs = [next(it) for _ in range(3)] if has_init else None
    h_ref = next(it)
    final_refs = [next(it) for _ in range(3)] if emit_state else None
    state = c_sc, n_sc, m_sc = next(it), next(it), next(it)
    c = pl.program_id(3)
    ch = q_ref.shape[0]
    dh = q_ref.shape[1] // heads_per_step

    @pl.when(c == 0)
    def _():
        for sc, src in zip(state, init_refs or [None] * 3):
            sc[...] = jnp.zeros_like(sc) if src is None else src[...]

    allow, allow_t, neg = allow_ref[...], allow_t_ref[...], neg_ref[...]
    ones_rows = jnp.ones((SUB, ch), BF16)

    heads = range(heads_per_step)
    cols = [slice(hh * dh, (hh + 1) * dh) for hh in heads]

    def gate_terms(hh):
        i_row = gt_ref[hh, 0:1, :]
        f_row = gt_ref[hh, 1:2, :]
        lf_row = jnp.minimum(f_row, 0.0) - jnp.log1p(jnp.exp(-jnp.abs(f_row)))
        pieces = jnp.concatenate(list(_split3(lf_row)) + [jnp.zeros((SUB - 3, ch), BF16)], axis=0)
        cum_col = _nt(allow, pieces)
        cum_row = jnp.dot(pieces, allow_t, preferred_element_type=F32)
        b_col = cum_col[:, 0:1] + cum_col[:, 1:2] + cum_col[:, 2:3]
        b_row = cum_row[0:1, :] + cum_row[1:2, :] + cum_row[2:3, :]
        total = jnp.sum(lf_row, axis=1, keepdims=True)
        e_row = i_row - b_row
        m_st = m_sc[hh, 0:1, 0:1]
        dmat = (b_col + e_row) + neg
        a_col = b_col + m_st
        m_j = jnp.maximum(a_col, jnp.max(dmat, axis=1, keepdims=True))
        g_row = total + e_row
        m_new = jnp.maximum(total + m_st, jnp.max(g_row, axis=1, keepdims=True))
        return dict(w_inter=jnp.exp(a_col - m_j), dexp=jnp.exp(dmat - m_j), floor=jnp.exp(-m_j), m_new=m_new,
                    decay=jnp.exp(total + m_st - m_new), w_row=jnp.exp(g_row - m_new))

    def read_out(hh, t):
        q, kt, v = q_ref[:, cols[hh]], kt_ref[cols[hh], :], v_ref[:, cols[hh]]
        s = jnp.dot(q, kt, preferred_element_type=F32) * t["dexp"]
        num = t["w_inter"] * jnp.dot(q, c_sc[hh].astype(BF16), preferred_element_type=F32) + _bdot(s, v)
        qn = (t["w_inter"] * jnp.sum(q.astype(F32) * n_sc[hh, 0:1, :], axis=1, keepdims=True)
              + jnp.sum(s, axis=1, keepdims=True))
        h_ref[:, cols[hh]] = (num / jnp.maximum(jnp.abs(qn), t["floor"])).astype(h_ref.dtype)

    def update(hh, t):
        kwt = (kt_ref[cols[hh], :].astype(F32) * t["w_row"]).astype(BF16)
        c_sc[hh] = t["decay"] * c_sc[hh] + jnp.dot(kwt, v_ref[:, cols[hh]], preferred_element_type=F32)
        n_sc[hh] = t["decay"] * n_sc[hh] + _nt(ones_rows, kwt)
        m_sc[hh] = jnp.broadcast_to(t["m_new"], m_sc.shape[1:])

    terms = [gate_terms(hh) for hh in heads]
    for hh in heads:
        read_out(hh, terms[hh])
    for hh in heads:
        update(hh, terms[hh])

    if emit_state:
        @pl.when(c == pl.num_programs(3) - 1)
        def _():
            for dst, sc in zip(final_refs, state):
                dst[...] = sc[...]


@functools.lru_cache(maxsize=None)
def _scan_masks(ch):
    r = np.arange(ch)
    causal = (r[None, :] <= r[:, None]).astype(np.float32)
    allow = np.stack([causal, causal.T])
    neg = np.where(allow > 0, 0.0, -np.inf).astype(np.float32)
    return allow.astype(BF16), np.transpose(allow, (0, 2, 1)).astype(BF16), neg


def mlstm_scan(q, kt, v, gates_t, init=None, *, emit_state=False):
    bsz, n, inner = q.shape
    nh = gates_t.shape[2]
    dh = inner // nh
    hp = 2 if nh % 2 == 0 else 1
    ch = min(SCAN_CHUNK, n)
    nc = n // ch
    ceff = lambda d, c: c + d * (nc - 1 - 2 * c)
    seq = pl.BlockSpec((None, ch, hp * dh), lambda d, b, hd, c: (b, ceff(d, c), hd))
    state_shapes = [(dh, dh), (SUB, dh), (SUB, 128)]
    st_specs = [pl.BlockSpec((None, None, hp) + s, lambda d, b, hd, c: (d, b, hd, 0, 0)) for s in state_shapes]
    by_dir = pl.BlockSpec((None, ch, ch), lambda d, b, hd, c: (d, 0, 0))
    in_specs = [seq, pl.BlockSpec((None, hp * dh, ch), lambda d, b, hd, c: (b, hd, ceff(d, c))), seq,
                pl.BlockSpec((None, None, hp, 2, ch), lambda d, b, hd, c: (b, d, hd, 0, ceff(d, c))),
                by_dir, by_dir, by_dir]
    args = [q, kt, v, gates_t, *_scan_masks(ch)]
    if init is not None:
        in_specs += st_specs
        args += list(init)
    out_specs = [pl.BlockSpec((None, None, ch, hp * dh), lambda d, b, hd, c: (d, b, ceff(d, c), hd))]
    out_shape = [jax.ShapeDtypeStruct((2, bsz, n, inner), BF16)]
    if emit_state:
        out_specs += st_specs
        out_shape += [jax.ShapeDtypeStruct((2, bsz, nh) + s, F32) for s in state_shapes]
    res = pl.pallas_call(
        functools.partial(_scan_kernel, has_init=init is not None, emit_state=emit_state, heads_per_step=hp),
        grid=(2, bsz, nh // hp, nc),
        in_specs=in_specs,
        out_specs=out_specs,
        out_shape=out_shape,
        scratch_shapes=[pltpu.VMEM((hp,) + s, F32) for s in state_shapes],
        compiler_params=_cparams("parallel", "parallel", "parallel", "arbitrary"),
        name="mlstm_scan",
    )(*args)
    return (res[0], tuple(res[1:])) if emit_state else (res[0], None)


def _mlstm_post_kernel(hf_ref, hb_ref, xc_ref, z_ref, res_ref, gate_ref, ng_ref, sk_ref, wd_ref, o_ref):
    inner = xc_ref.shape[-1]
    dh = inner // ML_HEADS
    acc = jnp.zeros(o_ref.shape, F32)
    for hd in range(ML_HEADS):
        lo, hi = hd * dh, (hd + 1) * dh
        x = hf_ref[:, lo:hi].astype(F32) + hb_ref[:, lo:hi].astype(F32)
        xz = x - jnp.mean(x, axis=-1, keepdims=True)
        var = jnp.mean(xz * xz, axis=-1, keepdims=True)
        hn = xz * lax.rsqrt(var + ML_NORM_EPS) * ng_ref[:, lo:hi]
        t = (hn + sk_ref[:, lo:hi] * xc_ref[:, lo:hi].astype(F32)) * _silu(z_ref[:, lo:hi].astype(F32))
        acc += _bdot(t, wd_ref[lo:hi, :])
    o_ref[...] = res_ref[...] + gate_ref[...] * acc


def mlstm_post(hs, xc, z, h_res, gate, prm, *, tm=256):
    bsz, n, inner = xc.shape
    d = h_res.shape[-1]
    tm = _tile(n, tm)
    const = lambda shape: pl.BlockSpec(shape, lambda b_, i: (0,) * len(shape))
    return pl.pallas_call(
        _mlstm_post_kernel,
        grid=(bsz, n // tm),
        in_specs=[
            pl.BlockSpec((None, None, tm, inner), lambda b_, i: (0, b_, i, 0)),
            pl.BlockSpec((None, None, tm, inner), lambda b_, i: (1, b_, i, 0)),
            pl.BlockSpec((None, tm, inner), lambda b_, i: (b_, i, 0)),
            pl.BlockSpec((None, tm, inner), lambda b_, i: (b_, i, 0)),
            pl.BlockSpec((None, tm, d), lambda b_, i: (b_, i, 0)),
            pl.BlockSpec((None, 1, d), lambda b_, i: (b_, 0, 0)),
            const((1, inner)), const((1, inner)), _resident((inner, d), lambda b_, i: (0, 0)),
        ],
        out_specs=pl.BlockSpec((None, tm, d), lambda b_, i: (b_, i, 0)),
        out_shape=jax.ShapeDtypeStruct((bsz, n, d), F32),
        compiler_params=_cparams("parallel", "parallel"),
        name="mlstm_post",
    )(hs, hs, xc, z, h_res, gate, prm["norm_g"].reshape(1, inner), prm["skip"].reshape(1, inner), prm["w_down"])


def _gates_by_head(gates, nh):
    bsz, n, _ = gates.shape
    return jnp.transpose(gates.reshape(bsz, n, 2, 2, nh), (0, 2, 4, 3, 1))


def mlstm_mixer_block(h_ctx, h_lat, mods_ctx, mods_lat, g, prm, *, with_ctx_out):
    pc = mlstm_pre(h_ctx, mods_ctx, g, prm)
    pq = mlstm_pre(h_lat, mods_lat, g, prm)
    hs_ctx, state = mlstm_scan(pc[0], pc[1], pc[2], _gates_by_head(pc[5], ML_HEADS), emit_state=True)
    hs_lat, _ = mlstm_scan(pq[0], pq[1], pq[2], _gates_by_head(pq[5], ML_HEADS), init=state)
    out_lat = mlstm_post(hs_lat, pq[3], pq[4], h_lat, mods_lat[:, 2:3, :], prm)
    out_ctx = mlstm_post(hs_ctx, pc[3], pc[4], h_ctx, mods_ctx[:, 2:3, :], prm) if with_ctx_out else None
    return out_ctx, out_lat


def _pos_embed_2d(n_tokens, d):
    rows = n_tokens // GRID_W
    quarter = d // 4
    omega = 1.0 / (POS_BASE ** (jnp.arange(quarter, dtype=F32) / quarter))
    ang_r = jnp.arange(rows, dtype=F32)[:, None] * omega
    ang_c = jnp.arange(GRID_W, dtype=F32)[:, None] * omega
    row_emb = jnp.concatenate([jnp.sin(ang_r), jnp.cos(ang_r)], axis=-1)
    col_emb = jnp.concatenate([jnp.sin(ang_c), jnp.cos(ang_c)], axis=-1)
    emb = jnp.concatenate([
        jnp.broadcast_to(row_emb[:, None, :], (rows, GRID_W, d // 2)),
        jnp.broadcast_to(col_emb[None, :, :], (rows, GRID_W, d // 2))], axis=-1)
    return emb.reshape(rows * GRID_W, d)


def kernel(x, c, ctx, c_ctx, ada_w, ada_b, norm_g, final_g, ffn_w_in, ffn_w_out, hy_w_in, hy_b_in, hy_conv_w, hy_conv_b, hy_f_w1, hy_f_b1, hy_f_w2, hy_f_b2, hy_f_w3, hy_f_b3, hy_f_freq, hy_log_decay, hy_skip, hy_w_out, hy_b_out, ml_w_up, ml_conv_w, ml_conv_b, ml_w_q, ml_w_k, ml_w_v, ml_w_gate, ml_b_gate, ml_norm_g, ml_skip, ml_w_down):
    bsz, n_lat, d = x.shape
    n_ctx = ctx.shape[1]
    depth = ada_w.shape[0]
    n_mixers = 2

    cond_rows = 8
    cond = jnp.zeros((cond_rows, d), F32).at[:bsz].set(c).at[bsz].set(c_ctx)
    mods_all = ada_modulation(cond, ada_w, ada_b).reshape(depth, cond_rows, N_MOD, d)

    ffn_w_in = ffn_w_in.astype(BF16)
    ffn_w_out = ffn_w_out.astype(BF16)
    pos = _pos_embed_2d(n_lat, d)

    h_lat = x
    h_ctx = ctx.reshape(1, bsz * n_ctx, d)
    for l in range(depth):
        last = l == depth - 1
        j = l // n_mixers
        m_lat = mods_all[l, :bsz]
        m_ctx = mods_all[l, bsz:bsz + 1]
        m_ctx_b = jnp.broadcast_to(m_ctx, (bsz, N_MOD, d))

        h_lat = ffn_block(h_lat, m_lat[:, 0:3], norm_g[l, 0], ffn_w_in, ffn_w_out, widx=(l, 0),
                          pos=pos if l == 0 else None)
        h_ctx = ffn_block(h_ctx, m_ctx[:, 0:3], norm_g[l, 0], ffn_w_in, ffn_w_out, widx=(l, 0))

        h_ctx = h_ctx.reshape(bsz, n_ctx, d)
        if l % n_mixers == 0:
            prm = dict(w_in=hy_w_in[j].astype(BF16), b_in=hy_b_in[j], conv_w=hy_conv_w[j], conv_b=hy_conv_b[j],
                       f_w1=hy_f_w1[j], f_b1=hy_f_b1[j], f_w2=hy_f_w2[j], f_b2=hy_f_b2[j], f_w3=hy_f_w3[j],
                       f_b3=hy_f_b3[j], f_freq=hy_f_freq[j], log_decay=hy_log_decay[j], skip=hy_skip[j],
                       w_out=hy_w_out[j].astype(BF16), b_out=hy_b_out[j])
            h_lat = hyena_mixer_block(h_lat, m_lat[:, 3:6], norm_g[l, 1], prm)
            if not last:
                h_ctx = hyena_mixer_block(h_ctx, m_ctx_b[:, 3:6], norm_g[l, 1], prm)
        else:
            prm = dict(w_up=ml_w_up[j].astype(BF16), conv_w=ml_conv_w[j], conv_b=ml_conv_b[j],
                       wq_dense=_dense_blockdiag(ml_w_q[j], ML_QKV_GROUP),
                       wk_dense=_dense_blockdiag(ml_w_k[j], ML_QKV_GROUP),
                       wv_dense=_dense_blockdiag(ml_w_v[j], ML_QKV_GROUP),
                       w_gate=ml_w_gate[j].astype(BF16), b_gate=ml_b_gate[j], norm_g=ml_norm_g[j],
                       skip=ml_skip[j], w_down=ml_w_down[j].astype(BF16))
            new_ctx, h_lat = mlstm_mixer_block(h_ctx, h_lat, m_ctx_b[:, 3:6], m_lat[:, 3:6], norm_g[l, 1], prm,
                                               with_ctx_out=not last)
            h_ctx = h_ctx if last else new_ctx
        h_ctx = h_ctx.reshape(1, bsz * n_ctx, d)

        h_lat = ffn_block(h_lat, m_lat[:, 6:9], norm_g[l, 2], ffn_w_in, ffn_w_out, widx=(l, 1),
                          final_g=final_g if last else None)
        if not last:
            h_ctx = ffn_block(h_ctx, m_ctx[:, 6:9], norm_g[l, 2], ffn_w_in, ffn_w_out, widx=(l, 1))
    return h_lat
```

```python
import functools
import math

import jax
import jax.numpy as jnp
import numpy as np
from jax import lax
from jax.experimental import pallas as pl
from jax.experimental.pallas import tpu as pltpu

F32 = jnp.float32
BF16 = jnp.bfloat16

NORM_EPS = 1e-6
GRID_W = 64
POS_BASE = 10000.0
N_MOD = 9
HY_EMB = 33
HY_BANDS = (HY_EMB - 1) // 2
ML_HEADS = 4
ML_NORM_EPS = 1e-5
ML_QKV_GROUP = 256
SCAN_CHUNK = 256
HALO = 8
CONV_COLS = 256

V7X_VMEM_LIMIT = 56 * 1024 * 1024


def _cparams(*sem):
    return pltpu.CompilerParams(dimension_semantics=sem, vmem_limit_bytes=V7X_VMEM_LIMIT)


def _bdot(a, b):
    return jnp.dot(a.astype(BF16), b.astype(BF16), preferred_element_type=F32)


def _rms_mod(h, g, shift, scale):
    y = h * lax.rsqrt(jnp.mean(h * h, axis=-1, keepdims=True) + NORM_EPS)
    return (y * g) * (1.0 + scale) + shift


def _silu(x):
    return x * (1.0 / (1.0 + jnp.exp(-x)))


def _tile(n, pref):
    if n <= pref:
        return n
    for t in range(pref, 7, -1):
        if n % t == 0 and t % 8 == 0:
            return t
    return n


def _mod_kernel(c_ref, w_ref, b_ref, o_ref):
    o_ref[...] = _bdot(_silu(c_ref[...]), w_ref[...]) + b_ref[...]


def ada_modulation(cond, ada_w, ada_b):
    n_layers, d, n_out = ada_w.shape
    r = cond.shape[0]
    tn = _tile(n_out, 2304) if n_out % 128 == 0 else n_out
    return pl.pallas_call(
        _mod_kernel,
        grid=(n_layers, n_out // tn),
        in_specs=[
            pl.BlockSpec((r, d), lambda l, j: (0, 0)),
            pl.BlockSpec((None, d, tn), lambda l, j: (l, 0, j)),
            pl.BlockSpec((None, 1, tn), lambda l, j: (l, 0, j)),
        ],
        out_specs=pl.BlockSpec((None, r, tn), lambda l, j: (l, 0, j)),
        out_shape=jax.ShapeDtypeStruct((n_layers, r, n_out), F32),
        compiler_params=_cparams("parallel", "parallel"),
        name="ada_modulation",
    )(cond, ada_w, ada_b.reshape(n_layers, 1, n_out))


def _ffn_kernel(*refs, has_pos, final):
    it = iter(refs)
    h_ref = next(it)
    pos_ref = next(it) if has_pos else None
    mod_ref, g_ref, wg_ref, wv_ref, wo_ref = next(it), next(it), next(it), next(it), next(it)
    fg_ref = next(it) if final else None
    o_ref = next(it)
    h = h_ref[...]
    if has_pos:
        h = h + pos_ref[...]
    u = _rms_mod(h, g_ref[...], mod_ref[0:1, :], mod_ref[1:2, :]).astype(BF16)
    gate = jnp.dot(u, wg_ref[...], preferred_element_type=F32)
    val = jnp.dot(u, wv_ref[...], preferred_element_type=F32)
    out = h + (0.5 * mod_ref[2:3, :]) * _bdot(_silu(gate) * val, wo_ref[...])
    if final:
        out = out * lax.rsqrt(jnp.mean(out * out, axis=-1, keepdims=True) + NORM_EPS) * fg_ref[...]
    o_ref[...] = out


def _resident(shape, index_map):
    return pl.BlockSpec(shape, index_map, pipeline_mode=pl.Buffered(1))


def ffn_block(h, mods, g, w_in, w_out, *, widx=(), pos=None, final_g=None, tm=512):
    bsz, n, d = h.shape
    ff = w_out.shape[-2]
    tm = _tile(n, tm)
    has_pos, final = pos is not None, final_g is not None
    lead = (None,) * len(widx)
    in_specs = [pl.BlockSpec((None, tm, d), lambda b, i: (b, i, 0))]
    args = [h]
    if has_pos:
        in_specs.append(pl.BlockSpec((tm, d), lambda b, i: (i, 0)))
        args.append(pos)
    in_specs += [
        pl.BlockSpec((None, 3, d), lambda b, i: (b, 0, 0)),
        pl.BlockSpec((1, d), lambda b, i: (0, 0)),
        _resident(lead + (d, ff), lambda b, i: widx + (0, 0)),
        _resident(lead + (d, ff), lambda b, i: widx + (0, 1)),
        _resident(lead + (ff, d), lambda b, i: widx + (0, 0)),
    ]
    args += [mods, g.reshape(1, d), w_in, w_in, w_out]
    if final:
        in_specs.append(pl.BlockSpec((1, d), lambda b, i: (0, 0)))
        args.append(final_g.reshape(1, d))
    return pl.pallas_call(
        functools.partial(_ffn_kernel, has_pos=has_pos, final=final),
        grid=(bsz, n // tm),
        in_specs=in_specs,
        out_specs=pl.BlockSpec((None, tm, d), lambda b, i: (b, i, 0)),
        out_shape=jax.ShapeDtypeStruct((bsz, n, d), F32),
        compiler_params=_cparams("parallel", "parallel"),
        name="ffn_block",
    )(*args)


def _linres_kernel(y_ref, h_ref, gate_ref, w_ref, b_ref, o_ref):
    o_ref[...] = h_ref[...] + gate_ref[...] * (_bdot(y_ref[...], w_ref[...]) + b_ref[...])


def linear_residual(y, h, gate, w, b, *, tm=512):
    bsz, n, kdim = y.shape
    d = h.shape[-1]
    tm = _tile(n, tm)
    return pl.pallas_call(
        _linres_kernel,
        grid=(bsz, n // tm),
        in_specs=[
            pl.BlockSpec((None, tm, kdim), lambda b_, i: (b_, i, 0)),
            pl.BlockSpec((None, tm, d), lambda b_, i: (b_, i, 0)),
            pl.BlockSpec((None, 1, d), lambda b_, i: (b_, 0, 0)),
            _resident((kdim, d), lambda b_, i: (0, 0)),
            pl.BlockSpec((1, d), lambda b_, i: (0, 0)),
        ],
        out_specs=pl.BlockSpec((None, tm, d), lambda b_, i: (b_, i, 0)),
        out_shape=jax.ShapeDtypeStruct((bsz, n, d), F32),
        compiler_params=_cparams("parallel", "parallel"),
        name="linear_residual",
    )(y, h, gate, w, b.reshape(1, d))


def _halo_specs(n, tm, d):
    hb, last = tm // HALO, n // HALO - 1
    return [
        pl.BlockSpec((None, HALO, d), lambda b, i: (b, jnp.maximum(i * hb - 1, 0), 0)),
        pl.BlockSpec((None, tm, d), lambda b, i: (b, i, 0)),
        pl.BlockSpec((None, HALO, d), lambda b, i: (b, jnp.minimum((i + 1) * hb, last), 0)),
    ]


def _mask_seq_ends(p, tm):
    i, ni = pl.program_id(1), pl.num_programs(1)
    head = jnp.where(i == 0, 0.0, p[:HALO])
    tail = jnp.where(i == ni - 1, 0.0, p[tm + HALO:])
    return jnp.concatenate([head, p[HALO:tm + HALO], tail], axis=0)


def _conv3(p, cw_ref, tm):
    rows = p.shape[0]
    prev = pltpu.roll(p, 1, 0)[HALO:HALO + tm]
    nxt = pltpu.roll(p, rows - 1, 0)[HALO:HALO + tm]
    return prev * cw_ref[0:1, :] + p[HALO:HALO + tm] * cw_ref[1:2, :] + nxt * cw_ref[2:3, :]


def _hyena_in_kernel(hp_ref, h_ref, hn_ref, mod_ref, g_ref, w_ref, b_ref, cw_ref, cb_ref, o_ref):
    tm = h_ref.shape[0]
    rows = jnp.concatenate([hp_ref[...], h_ref[...], hn_ref[...]], axis=0)
    u = _rms_mod(rows, g_ref[...], mod_ref[0:1, :], mod_ref[1:2, :]).astype(BF16)
    nout = o_ref.shape[-1]
    step = CONV_COLS if nout % CONV_COLS == 0 else nout
    for lo in range(0, nout, step):
        cs = slice(lo, lo + step)
        p = _mask_seq_ends(jnp.dot(u, w_ref[:, cs], preferred_element_type=F32) + b_ref[:, cs], tm)
        o_ref[:, cs] = _conv3(p, cw_ref.at[:, cs], tm) + cb_ref[:, cs]


def hyena_in_proj(h, mods, g, w, b, conv_w, conv_b, *, tm=512):
    bsz, n, d = h.shape
    nout = w.shape[1]
    tm = _tile(n, tm)
    const = lambda shape: pl.BlockSpec(shape, lambda b_, i: (0,) * len(shape))
    return pl.pallas_call(
        _hyena_in_kernel,
        grid=(bsz, n // tm),
        in_specs=_halo_specs(n, tm, d) + [
            pl.BlockSpec((None, 3, d), lambda b_, i: (b_, 0, 0)),
            const((1, d)), _resident((d, nout), lambda b_, i: (0, 0)), const((1, nout)), const((3, nout)),
            const((1, nout)),
        ],
        out_specs=pl.BlockSpec((None, tm, nout), lambda b_, i: (b_, i, 0)),
        out_shape=jax.ShapeDtypeStruct((bsz, n, nout), F32),
        compiler_params=_cparams("parallel", "parallel"),
        name="hyena_in_proj",
    )(h, h, h, mods, g.reshape(1, d), w, b.reshape(1, nout), conv_w, conv_b.reshape(1, nout))


def _hdot(a, b):
    return jnp.dot(a, b, preferred_element_type=F32, precision=lax.Precision.HIGHEST)


def _dot3(a, b):
    ah, bh = a.astype(BF16), b.astype(BF16)
    al, bl = (a - ah.astype(F32)).astype(BF16), (b - bh.astype(F32)).astype(BF16)
    dot = functools.partial(jnp.dot, preferred_element_type=F32)
    return dot(ah, bh) + (dot(al, bh) + dot(ah, bl))


def _filter_kernel(ft_ref, w1_ref, b1_ref, w2_ref, b2_ref, w3_ref, b3_ref, fr_ref, ld_ref, k_ref, ss_ref):
    half, i = pl.program_id(0), pl.program_id(1)
    tt = ft_ref.shape[0]
    c = k_ref.shape[-1]
    ft = ft_ref[...]
    h = jnp.sin(fr_ref[0:1, :] * (_hdot(ft, w1_ref[...]) + b1_ref[...]))
    h = jnp.sin(fr_ref[1:2, :] * (_hdot(h, w2_ref[...]) + b2_ref[...]))
    no_lag = (lax.broadcasted_iota(jnp.int32, (tt, c), 0) + (1 - half) + i) == 0

    @pl.when((half == 0) & (i == 0))
    def _():
        ss_ref[...] = jnp.zeros_like(ss_ref)

    for o in range(k_ref.shape[0]):
        k = (_dot3(h, w3_ref[o]) + b3_ref[o]) * jnp.exp(-ft[:, 0:1] * jnp.exp(ld_ref[o]))
        k = jnp.where(no_lag, 0.0, k)
        k_ref[o] = k
        ss_ref[o] += jnp.broadcast_to(jnp.sum(k * k, axis=0, keepdims=True), ss_ref.shape[1:])


def hyena_filters_time(n, f_w1, f_b1, f_w2, f_b2, f_w3, f_b3, f_freq, log_decay, *, tt=512):
    order, _, d = log_decay.shape
    fh = f_w2.shape[0]
    emb_pad = 64
    lag = np.arange(2 * n)
    lag = np.where(lag < n, lag, 2 * n - lag).astype(np.float32)
    t_norm = lag / np.float32(max(n - 1, 1))
    bands = np.linspace(1e-4, HY_BANDS - 1, HY_BANDS, dtype=np.float32)
    ang = (np.float32(2.0 * math.pi / n) * lag)[:, None] * bands[None, :]
    feats = np.zeros((2 * n, emb_pad), np.float32)
    feats[:, :HY_EMB] = np.concatenate([t_norm[:, None], np.cos(ang), -np.sin(ang)], axis=-1)
    w1 = jnp.zeros((emb_pad, fh), F32).at[:HY_EMB].set(f_w1)
    tt = _tile(n, tt)
    nt = n // tt
    const = lambda shape: pl.BlockSpec(shape, lambda hf, i: (0,) * len(shape))
    return pl.pallas_call(
        _filter_kernel,
        grid=(2, nt),
        in_specs=[
            pl.BlockSpec((tt, emb_pad), lambda hf, i: (hf * nt + i, 0)),
            const((emb_pad, fh)), const((1, fh)), const((fh, fh)), const((1, fh)),
            pl.BlockSpec((None, order, fh, d), lambda hf, i: (hf, 0, 0, 0)),
            pl.BlockSpec((order, None, 1, d), lambda hf, i: (0, hf, 0, 0)),
            const((2, fh)),
            pl.BlockSpec((order, None, 1, d), lambda hf, i: (0, hf, 0, 0)),
        ],
        out_specs=[
            pl.BlockSpec((order, tt, d), lambda hf, i: (0, hf * nt + i, 0)),
            pl.BlockSpec((order, 8, d), lambda hf, i: (0, 0, 0)),
        ],
        out_shape=[jax.ShapeDtypeStruct((order, 2 * n, d), F32), jax.ShapeDtypeStruct((order, 8, d), F32)],
        compiler_params=_cparams("arbitrary", "arbitrary"),
        name="hyena_filter_mlp",
    )(jnp.asarray(feats), w1, f_b1.reshape(1, fh), f_w2, f_b2.reshape(1, fh),
      jnp.transpose(f_w3.reshape(fh, order, 2, d), (2, 1, 0, 3)),
      f_b3.reshape(order, 2, 1, d), f_freq, log_decay.reshape(order, 2, 1, d))


def _fft_dims(n):
    nb = 1 << int(math.floor(math.log2(math.sqrt(2 * n))))
    na = 2 * n // nb
    assert na * nb == 2 * n and na % 16 == 0 and nb % 8 == 0, (n, na, nb)
    return na, nb


@functools.lru_cache(maxsize=None)
def _fft_tables(n):
    na, nb = _fft_dims(n)
    na2, nn = na // 2, 2 * n
    ka = np.arange(na, dtype=np.int64)[None, :, None]
    a = np.arange(na2, dtype=np.int64)[None, None, :]
    b = np.arange(nb, dtype=np.int64)[:, None, None]
    ang = (2.0 * np.pi / nn) * ((ka * (a * nb + b)) % nn)
    mr, mi = np.cos(ang), -np.sin(ang)
    f1 = np.concatenate([np.concatenate([mr, -mi], 2), np.concatenate([mi, mr], 2)], 1)
    mrt, mit = np.swapaxes(mr, 1, 2) / nn, np.swapaxes(mi, 1, 2) / nn
    g1 = np.concatenate([np.concatenate([mrt, mit], 2), np.concatenate([-mit, mrt], 2)], 1)
    kb = np.arange(nb, dtype=np.int64)
    ang2 = (2.0 * np.pi / nb) * ((kb[:, None] * kb[None, :]) % nb)
    er, ei = np.cos(ang2), -np.sin(ang2)
    f2 = np.block([[er, -ei], [ei, er]])
    g2 = np.block([[er, ei], [-ei, er]])
    a_all = np.arange(na, dtype=np.int64)[None, None, :]
    ang_f = (2.0 * np.pi / nn) * ((ka * (a_all * nb + b)) % nn)
    f1_real = np.concatenate([np.cos(ang_f), -np.sin(ang_f)], 1)
    as_bf16 = lambda x: x.astype(np.float32).astype(BF16)
    return dict(f1=as_bf16(f1), f1_real=as_bf16(f1_real), g1=as_bf16(g1), f2=as_bf16(f2), g2=as_bf16(g2))


def _pack_c(re, im):
    rb = lax.bitcast_convert_type(re.astype(BF16).astype(F32), jnp.uint32)
    ib = lax.bitcast_convert_type(im.astype(BF16).astype(F32), jnp.uint32)
    return rb | lax.shift_right_logical(ib, jnp.uint32(16))


def _unpack_c(w):
    re = lax.bitcast_convert_type(w & jnp.uint32(0xFFFF0000), F32)
    im = lax.bitcast_convert_type(lax.shift_left(w, jnp.uint32(16)), F32)
    return re, im


def _stack_bf16(re, im):
    return jnp.concatenate([re, im], axis=0).astype(BF16)


LANES = 128
SUB = 8
SHORT_SEQ = 1024


def _lane_cat(parts):
    return parts[0] if len(parts) == 1 else jnp.concatenate(parts, axis=1)


def _flat_rows(ref):
    return ref.reshape(math.prod(ref.shape[:-1]), ref.shape[-1])


def _fft_conv_kernel(x_ref, f1_ref, k_ref, f2_ref, g2_ref, g1_ref, z_ref, m_ref, skip_ref, o_ref, w_ref, *, n1, n2):
    t = pl.program_id(1)
    bsz, na2, jb, _ = x_ref.shape
    npair, na, kblk, nb = bsz // 2, 2 * na2, k_ref.shape[1], k_ref.shape[2]
    pitch = w_ref.shape[1] // na
    x2, z2, m2, o2 = _flat_rows(x_ref), _flat_rows(z_ref), _flat_rows(m_ref), _flat_rows(o_ref)
    col = lambda bi, j: pl.ds(bi * na2 * jb + j, na2, stride=jb)
    lane = lambda v, q: v[:, q * LANES:(q + 1) * LANES]

    @pl.when(t < n1)
    def _():
        for j in range(jb):
            xs = _lane_cat([jnp.concatenate([x2[col(2 * q, j), :], x2[col(2 * q + 1, j), :]], axis=0)
                            for q in range(npair)])
            res = jnp.dot(f1_ref[j], xs.astype(BF16), preferred_element_type=F32)
            packed = _pack_c(res[:na], res[na:])
            for q in range(npair):
                w_ref[q, pl.ds(t * jb + j, na, stride=pitch), :] = lane(packed, q)

    @pl.when((t >= n1) & (t < n1 + n2))
    def _():
        for k in range(kblk):
            rows = pl.ds(pl.multiple_of(((t - n1) * kblk + k) * pitch, SUB), nb)
            ar, ai = _unpack_c(_lane_cat([w_ref[q, rows, :] for q in range(npair)]))
            x = jnp.dot(f2_ref[...], _stack_bf16(ar, ai), preferred_element_type=F32)
            xr, xi = x[:nb], x[nb:]
            kr = _lane_cat([k_ref[0, k].astype(F32)] * npair)
            ki = _lane_cat([k_ref[1, k].astype(F32)] * npair)
            bv = jnp.dot(g2_ref[...], _stack_bf16(xr * kr - xi * ki, xr * ki + xi * kr), preferred_element_type=F32)
            packed = _pack_c(bv[:nb], bv[nb:])
            for q in range(npair):
                w_ref[q, rows, :] = lane(packed, q)

    @pl.when(t >= n1 + n2)
    def _():
        bb = t - (n1 + n2)
        skip = skip_ref[...]
        for j in range(jb):
            br, bi = _unpack_c(_lane_cat([w_ref[q, pl.ds(bb * jb + j, na, stride=pitch), :] for q in range(npair)]))
            y = jnp.dot(g1_ref[j], _stack_bf16(br, bi), preferred_element_type=F32)
            for q in range(npair):
                for r in range(2):
                    rows = col(2 * q + r, j)
                    o2[rows, :] = m2[rows, :] * (lane(y[r * na2:(r + 1) * na2], q) + skip * z2[rows, :])


def fft_long_conv(x, x_off, kf, order, z, z_off, m, m_off, skip, n, c, tables, *, jb=SUB, kblk=8):
    na, nb = _fft_dims(n)
    na2 = na // 2
    jb, kblk = (nb, na) if na * nb <= SHORT_SEQ else (jb, kblk)
    n1, n2 = nb // jb, na // kblk
    bsz = x.shape[0]
    view = lambda a: a.reshape(a.shape[0], na2, nb, a.shape[-1])
    xo, zo, mo = x_off // LANES, z_off // LANES, m_off // LANES
    col1 = lambda t: jnp.minimum(t, n1 - 1)
    col3 = lambda t: jnp.clip(t - (n1 + n2), 0, n1 - 1)
    seq = lambda off, col: pl.BlockSpec((bsz, na2, jb, LANES), lambda ci, t: (0, 0, col(t), ci + off))
    const = lambda shape: pl.BlockSpec(shape, lambda ci, t: (0,) * len(shape))
    out = pl.pallas_call(
        functools.partial(_fft_conv_kernel, n1=n1, n2=n2),
        grid=(c // LANES, n1 + n2 + n1),
        in_specs=[
            seq(xo, col1),
            pl.BlockSpec((jb, 2 * na, na), lambda ci, t: (col1(t), 0, 0)),
            pl.BlockSpec((None, 2, kblk, nb, LANES), lambda ci, t: (order, 0, jnp.clip(t - n1, 0, n2 - 1), 0, ci)),
            const((2 * nb, 2 * nb)), const((2 * nb, 2 * nb)),
            pl.BlockSpec((jb, na, 2 * na), lambda ci, t: (col3(t), 0, 0)),
            seq(zo, col3), seq(mo, col3),
            pl.BlockSpec((1, LANES), lambda ci, t: (0, ci)),
        ],
        out_specs=seq(0, col3),
        out_shape=jax.ShapeDtypeStruct((bsz, na2, nb, c), F32),
        scratch_shapes=[pltpu.VMEM((bsz // 2, na * (nb + SUB), LANES), jnp.uint32)],
        compiler_params=_cparams("parallel", "arbitrary"),
        name="fft_long_conv",
    )(view(x), tables["f1"], kf, tables["f2"], tables["g2"], tables["g1"], view(z), view(m), skip.reshape(1, c))
    return out.reshape(bsz, n, c)


def _fft_filter_kernel(x_ref, f1_ref, f2_ref, ss_ref, o_ref, w_ref, *, n1):
    t = pl.program_id(1)
    order, na, jb, _ = x_ref.shape
    kblk, nb = o_ref.shape[2], o_ref.shape[3]
    pitch = w_ref.shape[1] // na
    x2 = _flat_rows(x_ref)
    lane = lambda v, q: v[:, q * LANES:(q + 1) * LANES]

    @pl.when(t < n1)
    def _():
        for j in range(jb):
            xs = _lane_cat([x2[pl.ds(o * na * jb + j, na, stride=jb), :] for o in range(order)])
            res = jnp.dot(f1_ref[j], xs.astype(BF16), preferred_element_type=F32)
            packed = _pack_c(res[:na], res[na:])
            for o in range(order):
                w_ref[o, pl.ds(t * jb + j, na, stride=pitch), :] = lane(packed, o)

    @pl.when(t >= n1)
    def _():
        scale = _lane_cat([lax.rsqrt(ss_ref[o, 0:1, :] + 1e-12) for o in range(order)])
        for k in range(kblk):
            rows = pl.ds(pl.multiple_of(((t - n1) * kblk + k) * pitch, SUB), nb)
            ar, ai = _unpack_c(_lane_cat([w_ref[o, rows, :] for o in range(order)]))
            x = jnp.dot(f2_ref[...], _stack_bf16(ar, ai), preferred_element_type=F32) * scale
            for o in range(order):
                o_ref[o, 0, k] = lane(x[:nb], o).astype(o_ref.dtype)
                o_ref[o, 1, k] = lane(x[nb:], o).astype(o_ref.dtype)


def fft_filter_spectrum(kt, ss, n, tables, *, jb=SUB, kblk=8):
    na, nb = _fft_dims(n)
    order, _, c = kt.shape
    jb, kblk = (nb, na) if na * nb <= SHORT_SEQ else (jb, kblk)
    n1, n2 = nb // jb, na // kblk
    return pl.pallas_call(
        functools.partial(_fft_filter_kernel, n1=n1),
        grid=(c // LANES, n1 + n2),
        in_specs=[
            pl.BlockSpec((order, na, jb, LANES), lambda ci, t: (0, 0, jnp.minimum(t, n1 - 1), ci)),
            pl.BlockSpec((jb, 2 * na, na), lambda ci, t: (jnp.minimum(t, n1 - 1), 0, 0)),
            pl.BlockSpec((2 * nb, 2 * nb), lambda ci, t: (0, 0)),
            pl.BlockSpec((order, 8, LANES), lambda ci, t: (0, 0, ci)),
        ],
        out_specs=pl.BlockSpec((order, 2, kblk, nb, LANES), lambda ci, t: (0, 0, jnp.maximum(t - n1, 0), 0, ci)),
        out_shape=jax.ShapeDtypeStruct((order, 2, na, nb, c), BF16),
        scratch_shapes=[pltpu.VMEM((order, na * (nb + SUB), LANES), jnp.uint32)],
        compiler_params=_cparams("parallel", "arbitrary"),
        name="fft_filter_spectrum",
    )(kt.reshape(order, na, nb, c), tables["f1_real"], tables["f2"], ss)


def hyena_mixer_block(h, mods, g, prm):
    bsz, n, d = h.shape
    tables = _fft_tables(n)
    proj = hyena_in_proj(h, mods, g, prm["w_in"], prm["b_in"], prm["conv_w"], prm["conv_b"])
    kt, ss = hyena_filters_time(n, prm["f_w1"], prm["f_b1"], prm["f_w2"], prm["f_b2"], prm["f_w3"],
                                prm["f_b3"], prm["f_freq"], prm["log_decay"])
    kf = fft_filter_spectrum(kt, ss, n, tables)
    z = fft_long_conv(proj, 2 * d, kf, 0, proj, 2 * d, proj, 0, prm["skip"][0], n, d, tables)
    y = fft_long_conv(z, 0, kf, 1, z, 0, proj, d, prm["skip"][1], n, d, tables)
    return linear_residual(y, h, mods[:, 2:3, :], prm["w_out"], prm["b_out"])


def _mlstm_pre_kernel(hp_ref, h_ref, hn_ref, mod_ref, g_ref, wup_ref, cw_ref, cb_ref, wq_ref, wk_ref, wkt_ref, wv_ref,
                      wg_ref, bg_ref, q_ref, kt_ref, v_ref, xc_ref, z_ref, gt_ref, *, k_scale):
    tm = h_ref.shape[0]
    inner = q_ref.shape[-1]
    grp = wq_ref.shape[-1]
    rows = jnp.concatenate([hp_ref[...], h_ref[...], hn_ref[...]], axis=0)
    u = _rms_mod(rows, g_ref[...], mod_ref[0:1, :], mod_ref[1:2, :])
    up = _bdot(u, wup_ref[...])
    xm_h = _mask_seq_ends(up[:, :inner], tm)
    z_ref[...] = up[HALO:HALO + tm, inner:].astype(z_ref.dtype)
    xc_f = _silu(_conv3(xm_h, cw_ref, tm) + cb_ref[...])
    xc_ref[...] = xc_f.astype(xc_ref.dtype)
    xc_all, xm_all = xc_f.astype(BF16), xm_h[HALO:HALO + tm].astype(BF16)
    gates = jnp.zeros(gt_ref.shape, F32) + bg_ref[...]
    for gi in range(inner // grp):
        lo, hi = gi * grp, (gi + 1) * grp
        xc, xm = xc_all[:, lo:hi], xm_all[:, lo:hi]
        q = jnp.dot(xc, wq_ref[gi], preferred_element_type=F32)
        k = jnp.dot(xc, wk_ref[gi], preferred_element_type=F32)
        v = jnp.dot(xm, wv_ref[gi], preferred_element_type=F32)
        q_ref[:, lo:hi] = q.astype(q_ref.dtype)
        kt_ref[lo:hi, :] = (_nt(wkt_ref[gi], xc) * k_scale).astype(kt_ref.dtype)
        v_ref[:, lo:hi] = v.astype(v_ref.dtype)
        gates += (_bdot(q, wg_ref[lo:hi, :]) + _bdot(k, wg_ref[inner + lo:inner + hi, :])
                  + _bdot(v, wg_ref[2 * inner + lo:2 * inner + hi, :]))
    gt_ref[...] = gates


def mlstm_pre(h, mods, g, prm, *, tm=256):
    bsz, n, d = h.shape
    inner = prm["w_up"].shape[1] // 2
    ngate = prm["w_gate"].shape[1]
    grp = prm["wq_dense"].shape[-1]
    tm = _tile(n, tm)
    const = lambda shape: pl.BlockSpec(shape, lambda b_, i: (0,) * len(shape))
    row_spec = lambda w: pl.BlockSpec((None, tm, w), lambda b_, i: (b_, i, 0))
    mm = jax.ShapeDtypeStruct((bsz, n, inner), BF16)
    return pl.pallas_call(
        functools.partial(_mlstm_pre_kernel, k_scale=float((inner // ML_HEADS) ** -0.5)),
        grid=(bsz, n // tm),
        in_specs=_halo_specs(n, tm, d) + [
            pl.BlockSpec((None, 3, d), lambda b_, i: (b_, 0, 0)),
            const((1, d)), _resident((d, 2 * inner), lambda b_, i: (0, 0)), const((3, inner)), const((1, inner)),
            const((inner // grp, grp, grp)), const((inner // grp, grp, grp)), const((inner // grp, grp, grp)),
            const((inner // grp, grp, grp)), const((3 * inner, ngate)), const((1, ngate)),
        ],
        out_specs=[row_spec(inner), pl.BlockSpec((None, inner, tm), lambda b_, i: (b_, 0, i))]
                  + [row_spec(inner)] * 3 + [row_spec(ngate)],
        out_shape=[mm, jax.ShapeDtypeStruct((bsz, inner, n), BF16), mm, mm, mm,
                   jax.ShapeDtypeStruct((bsz, n, ngate), F32)],
        compiler_params=_cparams("parallel", "parallel"),
        name="mlstm_pre",
    )(h, h, h, mods, g.reshape(1, d), prm["w_up"], prm["conv_w"], prm["conv_b"].reshape(1, inner),
      prm["wq_dense"], prm["wk_dense"], jnp.swapaxes(prm["wk_dense"], 1, 2), prm["wv_dense"], prm["w_gate"],
      prm["b_gate"].reshape(1, ngate))


def _dense_blockdiag(w, grp):
    nb, bs, _ = w.shape
    per = grp // bs
    eye = jnp.eye(per, dtype=w.dtype)
    dense = jnp.einsum("gmde,mn->gmdne", w.reshape(nb // per, per, bs, bs), eye)
    return dense.reshape(nb // per, grp, grp).astype(BF16)


def _split3(x):
    p1 = x.astype(BF16)
    r1 = x - p1.astype(F32)
    p2 = r1.astype(BF16)
    p3 = (r1 - p2.astype(F32)).astype(BF16)
    return p1, p2, p3


def _nt(a, b):
    return lax.dot_general(a, b, (((1,), (1,)), ((), ())), preferred_element_type=F32)


def _scan_kernel(*refs, has_init, emit_state, heads_per_step):
    it = iter(refs)
    q_ref, kt_ref, v_ref, gt_ref = next(it), next(it), next(it), next(it)
    allow_ref, allow_t_ref, neg_ref = next(it), next(it), next(it)
    init_refs = [next(it) for _ in range(3)] if has_init else None
    h_ref = next(it)
    final_refs = [next(it) for _ in range(3)] if emit_state else None
    state = c_sc, n_sc, m_sc = next(it), next(it), next(it)
    c = pl.program_id(3)
    ch = q_ref.shape[0]
    dh = q_ref.shape[1] // heads_per_step

    @pl.when(c == 0)
    def _():
        for sc, src in zip(state, init_refs or [None] * 3):
            sc[...] = jnp.zeros_like(sc) if src is None else src[...]

    allow, allow_t, neg = allow_ref[...], allow_t_ref[...], neg_ref[...]
    ones_rows = jnp.ones((SUB, ch), BF16)

    heads = range(heads_per_step)
    cols = [slice(hh * dh, (hh + 1) * dh) for hh in heads]

    def gate_terms(hh):
        i_row = gt_ref[hh, 0:1, :]
        f_row = gt_ref[hh, 1:2, :]
        lf_row = jnp.minimum(f_row, 0.0) - jnp.log1p(jnp.exp(-jnp.abs(f_row)))
        pieces = jnp.concatenate(list(_split3(lf_row)) + [jnp.zeros((SUB - 3, ch), BF16)], axis=0)
        cum_col = _nt(allow, pieces)
        cum_row = jnp.dot(pieces, allow_t, preferred_element_type=F32)
        b_col = cum_col[:, 0:1] + cum_col[:, 1:2] + cum_col[:, 2:3]
        b_row = cum_row[0:1, :] + cum_row[1:2, :] + cum_row[2:3, :]
        total = jnp.sum(lf_row, axis=1, keepdims=True)
        e_row = i_row - b_row
        m_st = m_sc[hh, 0:1, 0:1]
        dmat = (b_col + e_row) + neg
        a_col = b_col + m_st
        m_j = jnp.maximum(a_col, jnp.max(dmat, axis=1, keepdims=True))
        g_row = total + e_row
        m_new = jnp.maximum(total + m_st, jnp.max(g_row, axis=1, keepdims=True))
        return dict(w_inter=jnp.exp(a_col - m_j), dexp=jnp.exp(dmat - m_j), floor=jnp.exp(-m_j), m_new=m_new,
                    decay=jnp.exp(total + m_st - m_new), w_row=jnp.exp(g_row - m_new))

    def read_out(hh, t):
        q, kt, v = q_ref[:, cols[hh]], kt_ref[cols[hh], :], v_ref[:, cols[hh]]
        s = jnp.dot(q, kt, preferred_element_type=F32) * t["dexp"]
        num = t["w_inter"] * jnp.dot(q, c_sc[hh].astype(BF16), preferred_element_type=F32) + _bdot(s, v)
        qn = (t["w_inter"] * jnp.sum(q.astype(F32) * n_sc[hh, 0:1, :], axis=1, keepdims=True)
              + jnp.sum(s, axis=1, keepdims=True))
        h_ref[:, cols[hh]] = (num / jnp.maximum(jnp.abs(qn), t["floor"])).astype(h_ref.dtype)

    def update(hh, t):
        kwt = (kt_ref[cols[hh], :].astype(F32) * t["w_row"]).astype(BF16)
        c_sc[hh] = t["decay"] * c_sc[hh] + jnp.dot(kwt, v_ref[:, cols[hh]], preferred_element_type=F32)
        n_sc[hh] = t["decay"] * n_sc[hh] + _nt(ones_rows, kwt)
        m_sc[hh] = jnp.broadcast_to(t["m_new"], m_sc.shape[1:])

    terms = [gate_terms(hh) for hh in heads]
    for hh in heads:
        read_out(hh, terms[hh])
    for hh in heads:
        update(hh, terms[hh])

    if emit_state:
        @pl.when(c == pl.num_programs(3) - 1)
        def _():
            for dst, sc in zip(final_refs, state):
                dst[...] = sc[...]


@functools.lru_cache(maxsize=None)
def _scan_masks(ch):
    r = np.arange(ch)
    causal = (r[None, :] <= r[:, None]).astype(np.float32)
    allow = np.stack([causal, causal.T])
    neg = np.where(allow > 0, 0.0, -np.inf).astype(np.float32)
    return allow.astype(BF16), np.transpose(allow, (0, 2, 1)).astype(BF16), neg


def mlstm_scan(q, kt, v, gates_t, init=None, *, emit_state=False):
    bsz, n, inner = q.shape
    nh = gates_t.shape[2]
    dh = inner // nh
    hp = nh if nh <= 4 else (2 if nh % 2 == 0 else 1)
    ch = min(SCAN_CHUNK, n)
    nc = n // ch
    ceff = lambda d, c: c + d * (nc - 1 - 2 * c)
    seq = pl.BlockSpec((None, ch, hp * dh), lambda d, b, hd, c: (b, ceff(d, c), hd))
    state_shapes = [(dh, dh), (SUB, dh), (SUB, 128)]
    st_specs = [pl.BlockSpec((None, None, hp) + s, lambda d, b, hd, c: (d, b, hd, 0, 0)) for s in state_shapes]
    by_dir = pl.BlockSpec((None, ch, ch), lambda d, b, hd, c: (d, 0, 0))
    in_specs = [seq, pl.BlockSpec((None, hp * dh, ch), lambda d, b, hd, c: (b, hd, ceff(d, c))), seq,
                pl.BlockSpec((None, None, hp, 2, ch), lambda d, b, hd, c: (b, d, hd, 0, ceff(d, c))),
                by_dir, by_dir, by_dir]
    args = [q, kt, v, gates_t, *_scan_masks(ch)]
    if init is not None:
        in_specs += st_specs
        args += list(init)
    out_specs = [pl.BlockSpec((None, None, ch, hp * dh), lambda d, b, hd, c: (d, b, ceff(d, c), hd))]
    out_shape = [jax.ShapeDtypeStruct((2, bsz, n, inner), BF16)]
    if emit_state:
        out_specs += st_specs
        out_shape += [jax.ShapeDtypeStruct((2, bsz, nh) + s, F32) for s in state_shapes]
    res = pl.pallas_call(
        functools.partial(_scan_kernel, has_init=init is not None, emit_state=emit_state, heads_per_step=hp),
        grid=(2, bsz, nh // hp, nc),
        in_specs=in_specs,
        out_specs=out_specs,
        out_shape=out_shape,
        scratch_shapes=[pltpu.VMEM((hp,) + s, F32) for s in state_shapes],
        compiler_params=_cparams("parallel", "parallel", "parallel", "arbitrary"),
        name="mlstm_scan",
    )(*args)
    return (res[0], tuple(res[1:])) if emit_state else (res[0], None)


def _mlstm_post_kernel(hf_ref, hb_ref, xc_ref, z_ref, res_ref, gate_ref, ng_ref, sk_ref, wd_ref, o_ref):
    inner = xc_ref.shape[-1]
    dh = inner // ML_HEADS
    acc = jnp.zeros(o_ref.shape, F32)
    for hd in range(ML_HEADS):
        lo, hi = hd * dh, (hd + 1) * dh
        x = hf_ref[:, lo:hi].astype(F32) + hb_ref[:, lo:hi].astype(F32)
        xz = x - jnp.mean(x, axis=-1, keepdims=True)
        var = jnp.mean(xz * xz, axis=-1, keepdims=True)
        hn = xz * lax.rsqrt(var + ML_NORM_EPS) * ng_ref[:, lo:hi]
        t = (hn + sk_ref[:, lo:hi] * xc_ref[:, lo:hi].astype(F32)) * _silu(z_ref[:, lo:hi].astype(F32))
        acc += _bdot(t, wd_ref[lo:hi, :])
    o_ref[...] = res_ref[...] + gate_ref[...] * acc


def mlstm_post(hs, xc, z, h_res, gate, prm, *, tm=256):
    bsz, n, inner = xc.shape
    d = h_res.shape[-1]
    tm = _tile(n, tm)
    const = lambda shape: pl.BlockSpec(shape, lambda b_, i: (0,) * len(shape))
    return pl.pallas_call(
        _mlstm_post_kernel,
        grid=(bsz, n // tm),
        in_specs=[
            pl.BlockSpec((None, None, tm, inner), lambda b_, i: (0, b_, i, 0)),
            pl.BlockSpec((None, None, tm, inner), lambda b_, i: (1, b_, i, 0)),
            pl.BlockSpec((None, tm, inner), lambda b_, i: (b_, i, 0)),
            pl.BlockSpec((None, tm, inner), lambda b_, i: (b_, i, 0)),
            pl.BlockSpec((None, tm, d), lambda b_, i: (b_, i, 0)),
            pl.BlockSpec((None, 1, d), lambda b_, i: (b_, 0, 0)),
            const((1, inner)), const((1, inner)), _resident((inner, d), lambda b_, i: (0, 0)),
        ],
        out_specs=pl.BlockSpec((None, tm, d), lambda b_, i: (b_, i, 0)),
        out_shape=jax.ShapeDtypeStruct((bsz, n, d), F32),
        compiler_params=_cparams("parallel", "parallel"),
        name="mlstm_post",
    )(hs, hs, xc, z, h_res, gate, prm["norm_g"].reshape(1, inner), prm["skip"].reshape(1, inner), prm["w_down"])


def _gates_by_head(gates, nh):
    bsz, n, _ = gates.shape
    return jnp.transpose(gates.reshape(bsz, n, 2, 2, nh), (0, 2, 4, 3, 1))


def mlstm_mixer_block(h_ctx, h_lat, mods_ctx, mods_lat, g, prm, *, with_ctx_out):
    pc = mlstm_pre(h_ctx, mods_ctx, g, prm)
    pq = mlstm_pre(h_lat, mods_lat, g, prm)
    hs_ctx, state = mlstm_scan(pc[0], pc[1], pc[2], _gates_by_head(pc[5], ML_HEADS), emit_state=True)
    hs_lat, _ = mlstm_scan(pq[0], pq[1], pq[2], _gates_by_head(pq[5], ML_HEADS), init=state)
    out_lat = mlstm_post(hs_lat, pq[3], pq[4], h_lat, mods_lat[:, 2:3, :], prm)
    out_ctx = mlstm_post(hs_ctx, pc[3], pc[4], h_ctx, mods_ctx[:, 2:3, :], prm) if with_ctx_out else None
    return out_ctx, out_lat


def _pos_embed_2d(n_tokens, d):
    rows = n_tokens // GRID_W
    quarter = d // 4
    omega = 1.0 / (POS_BASE ** (jnp.arange(quarter, dtype=F32) / quarter))
    ang_r = jnp.arange(rows, dtype=F32)[:, None] * omega
    ang_c = jnp.arange(GRID_W, dtype=F32)[:, None] * omega
    row_emb = jnp.concatenate([jnp.sin(ang_r), jnp.cos(ang_r)], axis=-1)
    col_emb = jnp.concatenate([jnp.sin(ang_c), jnp.cos(ang_c)], axis=-1)
    emb = jnp.concatenate([
        jnp.broadcast_to(row_emb[:, None, :], (rows, GRID_W, d // 2)),
        jnp.broadcast_to(col_emb[None, :, :], (rows, GRID_W, d // 2))], axis=-1)
    return emb.reshape(rows * GRID_W, d)


def kernel(x, c, ctx, c_ctx, ada_w, ada_b, norm_g, final_g, ffn_w_in, ffn_w_out, hy_w_in, hy_b_in, hy_conv_w, hy_conv_b, hy_f_w1, hy_f_b1, hy_f_w2, hy_f_b2, hy_f_w3, hy_f_b3, hy_f_freq, hy_log_decay, hy_skip, hy_w_out, hy_b_out, ml_w_up, ml_conv_w, ml_conv_b, ml_w_q, ml_w_k, ml_w_v, ml_w_gate, ml_b_gate, ml_norm_g, ml_skip, ml_w_down):
    bsz, n_lat, d = x.shape
    n_ctx = ctx.shape[1]
    depth = ada_w.shape[0]
    n_mixers = 2

    cond_rows = 8
    cond = jnp.zeros((cond_rows, d), F32).at[:bsz].set(c).at[bsz].set(c_ctx)
    mods_all = ada_modulation(cond, ada_w, ada_b).reshape(depth, cond_rows, N_MOD, d)

    ffn_w_in = ffn_w_in.astype(BF16)
    ffn_w_out = ffn_w_out.astype(BF16)
    pos = _pos_embed_2d(n_lat, d)

    h_lat = x
    h_ctx = ctx.reshape(1, bsz * n_ctx, d)
    for l in range(depth):
        last = l == depth - 1
        j = l // n_mixers
        m_lat = mods_all[l, :bsz]
        m_ctx = mods_all[l, bsz:bsz + 1]
        m_ctx_b = jnp.broadcast_to(m_ctx, (bsz, N_MOD, d))

        h_lat = ffn_block(h_lat, m_lat[:, 0:3], norm_g[l, 0], ffn_w_in, ffn_w_out, widx=(l, 0),
                          pos=pos if l == 0 else None)
        h_ctx = ffn_block(h_ctx, m_ctx[:, 0:3], norm_g[l, 0], ffn_w_in, ffn_w_out, widx=(l, 0))

        h_ctx = h_ctx.reshape(bsz, n_ctx, d)
        if l % n_mixers == 0:
            prm = dict(w_in=hy_w_in[j].astype(BF16), b_in=hy_b_in[j], conv_w=hy_conv_w[j], conv_b=hy_conv_b[j],
                       f_w1=hy_f_w1[j], f_b1=hy_f_b1[j], f_w2=hy_f_w2[j], f_b2=hy_f_b2[j], f_w3=hy_f_w3[j],
                       f_b3=hy_f_b3[j], f_freq=hy_f_freq[j], log_decay=hy_log_decay[j], skip=hy_skip[j],
                       w_out=hy_w_out[j].astype(BF16), b_out=hy_b_out[j])
            h_lat = hyena_mixer_block(h_lat, m_lat[:, 3:6], norm_g[l, 1], prm)
            if not last:
                h_ctx = hyena_mixer_block(h_ctx, m_ctx_b[:, 3:6], norm_g[l, 1], prm)
        else:
            prm = dict(w_up=ml_w_up[j].astype(BF16), conv_w=ml_conv_w[j], conv_b=ml_conv_b[j],
                       wq_dense=_dense_blockdiag(ml_w_q[j], ML_QKV_GROUP),
                       wk_dense=_dense_blockdiag(ml_w_k[j], ML_QKV_GROUP),
                       wv_dense=_dense_blockdiag(ml_w_v[j], ML_QKV_GROUP),
                       w_gate=ml_w_gate[j].astype(BF16), b_gate=ml_b_gate[j], norm_g=ml_norm_g[j],
                       skip=ml_skip[j], w_down=ml_w_down[j].astype(BF16))
            new_ctx, h_lat = mlstm_mixer_block(h_ctx, h_lat, m_ctx_b[:, 3:6], m_lat[:, 3:6], norm_g[l, 1], prm,
                                               with_ctx_out=not last)
            h_ctx = h_ctx if last else new_ctx
        h_ctx = h_ctx.reshape(1, bsz * n_ctx, d)

        h_lat = ffn_block(h_lat, m_lat[:, 6:9], norm_g[l, 2], ffn_w_in, ffn_w_out, widx=(l, 1),
                          final_g=final_g if last else None)
        if not last:
            h_ctx = ffn_block(h_ctx, m_ctx[:, 6:9], norm_g[l, 2], ffn_w_in, ffn_w_out, widx=(l, 1))
    return h_lat
```

```python
import functools
import math

import jax
import jax.numpy as jnp
import numpy as np
from jax import lax
from jax.experimental import pallas as pl
from jax.experimental.pallas import tpu as pltpu

F32 = jnp.float32
BF16 = jnp.bfloat16

NORM_EPS = 1e-6
GRID_W = 64
POS_BASE = 10000.0
N_MOD = 9
HY_EMB = 33
HY_BANDS = (HY_EMB - 1) // 2
ML_HEADS = 4
ML_NORM_EPS = 1e-5
ML_QKV_GROUP = 256
SCAN_CHUNK = 256
HALO = 8
CONV_COLS = 256

V7X_VMEM_LIMIT = 56 * 1024 * 1024


def _cparams(*sem):
    return pltpu.CompilerParams(dimension_semantics=sem, vmem_limit_bytes=V7X_VMEM_LIMIT)


def _bdot(a, b):
    return jnp.dot(a.astype(BF16), b.astype(BF16), preferred_element_type=F32)


def _rms_mod(h, g, shift, scale):
    y = h * lax.rsqrt(jnp.mean(h * h, axis=-1, keepdims=True) + NORM_EPS)
    return (y * g) * (1.0 + scale) + shift


def _silu(x):
    return x * (1.0 / (1.0 + jnp.exp(-x)))


def _tile(n, pref):
    if n <= pref:
        return n
    for t in range(pref, 7, -1):
        if n % t == 0 and t % 8 == 0:
            return t
    return n


def _mod_kernel(c_ref, w_ref, b_ref, o_ref):
    o_ref[...] = _bdot(_silu(c_ref[...]), w_ref[...]) + b_ref[...]


def ada_modulation(cond, ada_w, ada_b):
    n_layers, d, n_out = ada_w.shape
    r = cond.shape[0]
    tn = _tile(n_out, 2304) if n_out % 128 == 0 else n_out
    return pl.pallas_call(
        _mod_kernel,
        grid=(n_layers, n_out // tn),
        in_specs=[
            pl.BlockSpec((r, d), lambda l, j: (0, 0)),
            pl.BlockSpec((None, d, tn), lambda l, j: (l, 0, j)),
            pl.BlockSpec((None, 1, tn), lambda l, j: (l, 0, j)),
        ],
        out_specs=pl.BlockSpec((None, r, tn), lambda l, j: (l, 0, j)),
        out_shape=jax.ShapeDtypeStruct((n_layers, r, n_out), F32),
        compiler_params=_cparams("parallel", "parallel"),
        name="ada_modulation",
    )(cond, ada_w, ada_b.reshape(n_layers, 1, n_out))


def _ffn_kernel(*refs, has_pos, final):
    it = iter(refs)
    h_ref = next(it)
    pos_ref = next(it) if has_pos else None
    mod_ref, g_ref, wg_ref, wv_ref, wo_ref = next(it), next(it), next(it), next(it), next(it)
    fg_ref = next(it) if final else None
    o_ref = next(it)
    h = h_ref[...]
    if has_pos:
        h = h + pos_ref[...]
    u = _rms_mod(h, g_ref[...], mod_ref[0:1, :], mod_ref[1:2, :]).astype(BF16)
    gate = jnp.dot(u, wg_ref[...], preferred_element_type=F32)
    val = jnp.dot(u, wv_ref[...], preferred_element_type=F32)
    out = h + (0.5 * mod_ref[2:3, :]) * _bdot(_silu(gate) * val, wo_ref[...])
    if final:
        out = out * lax.rsqrt(jnp.mean(out * out, axis=-1, keepdims=True) + NORM_EPS) * fg_ref[...]
    o_ref[...] = out


def _resident(shape, index_map):
    return pl.BlockSpec(shape, index_map, pipeline_mode=pl.Buffered(1))


def ffn_block(h, mods, g, w_in, w_out, *, widx=(), pos=None, final_g=None, tm=512):
    bsz, n, d = h.shape
    ff = w_out.shape[-2]
    tm = _tile(n, tm)
    has_pos, final = pos is not None, final_g is not None
    lead = (None,) * len(widx)
    in_specs = [pl.BlockSpec((None, tm, d), lambda b, i: (b, i, 0))]
    args = [h]
    if has_pos:
        in_specs.append(pl.BlockSpec((tm, d), lambda b, i: (i, 0)))
        args.append(pos)
    in_specs += [
        pl.BlockSpec((None, 3, d), lambda b, i: (b, 0, 0)),
        pl.BlockSpec((1, d), lambda b, i: (0, 0)),
        _resident(lead + (d, ff), lambda b, i: widx + (0, 0)),
        _resident(lead + (d, ff), lambda b, i: widx + (0, 1)),
        _resident(lead + (ff, d), lambda b, i: widx + (0, 0)),
    ]
    args += [mods, g.reshape(1, d), w_in, w_in, w_out]
    if final:
        in_specs.append(pl.BlockSpec((1, d), lambda b, i: (0, 0)))
        args.append(final_g.reshape(1, d))
    return pl.pallas_call(
        functools.partial(_ffn_kernel, has_pos=has_pos, final=final),
        grid=(bsz, n // tm),
        in_specs=in_specs,
        out_specs=pl.BlockSpec((None, tm, d), lambda b, i: (b, i, 0)),
        out_shape=jax.ShapeDtypeStruct((bsz, n, d), F32),
        compiler_params=_cparams("parallel", "parallel"),
        name="ffn_block",
    )(*args)


def _linres_kernel(y_ref, h_ref, gate_ref, w_ref, b_ref, o_ref):
    for r, y in enumerate(_unpack_c(y_ref[...])):
        o_ref[r] = h_ref[r] + gate_ref[r] * (_bdot(y, w_ref[...]) + b_ref[...])


def linear_residual(y, h, gate, w, b, *, tm=512):
    npair, n, kdim = y.shape
    d = h.shape[-1]
    tm = _tile(n, tm)
    return pl.pallas_call(
        _linres_kernel,
        grid=(npair, n // tm),
        in_specs=[
            pl.BlockSpec((None, tm, kdim), lambda q, i: (q, i, 0)),
            pl.BlockSpec((2, tm, d), lambda q, i: (q, i, 0)),
            pl.BlockSpec((2, 1, d), lambda q, i: (q, 0, 0)),
            _resident((kdim, d), lambda q, i: (0, 0)),
            pl.BlockSpec((1, d), lambda q, i: (0, 0)),
        ],
        out_specs=pl.BlockSpec((2, tm, d), lambda q, i: (q, i, 0)),
        out_shape=jax.ShapeDtypeStruct(h.shape, F32),
        compiler_params=_cparams("parallel", "parallel"),
        name="linear_residual",
    )(y, h, gate, w, b.reshape(1, d))


def _halo_specs(n, tm, d, nbatch=None):
    hb, last = tm // HALO, n // HALO - 1
    return [
        pl.BlockSpec((nbatch, HALO, d), lambda b, i: (b, jnp.maximum(i * hb - 1, 0), 0)),
        pl.BlockSpec((nbatch, tm, d), lambda b, i: (b, i, 0)),
        pl.BlockSpec((nbatch, HALO, d), lambda b, i: (b, jnp.minimum((i + 1) * hb, last), 0)),
    ]


def _mask_seq_ends(p, tm):
    i, ni = pl.program_id(1), pl.num_programs(1)
    head = jnp.where(i == 0, 0.0, p[:HALO])
    tail = jnp.where(i == ni - 1, 0.0, p[tm + HALO:])
    return jnp.concatenate([head, p[HALO:tm + HALO], tail], axis=0)


def _conv3(p, cw_ref, tm):
    rows = p.shape[0]
    prev = pltpu.roll(p, 1, 0)[HALO:HALO + tm]
    nxt = pltpu.roll(p, rows - 1, 0)[HALO:HALO + tm]
    return prev * cw_ref[0:1, :] + p[HALO:HALO + tm] * cw_ref[1:2, :] + nxt * cw_ref[2:3, :]


def _hyena_in_kernel(hp_ref, h_ref, hn_ref, mod_ref, g_ref, w_ref, b_ref, cw_ref, cb_ref, o_ref):
    tm = h_ref.shape[1]
    us = []
    for r in range(2):
        rows = jnp.concatenate([hp_ref[r], h_ref[r], hn_ref[r]], axis=0)
        us.append(_rms_mod(rows, g_ref[...], mod_ref[r, 0:1, :], mod_ref[r, 1:2, :]).astype(BF16))
    nout = o_ref.shape[-1]
    step = CONV_COLS if nout % CONV_COLS == 0 else nout
    for lo in range(0, nout, step):
        cs = slice(lo, lo + step)
        outs = []
        for u in us:
            p = _mask_seq_ends(jnp.dot(u, w_ref[:, cs], preferred_element_type=F32) + b_ref[:, cs], tm)
            outs.append(_conv3(p, cw_ref.at[:, cs], tm) + cb_ref[:, cs])
        o_ref[:, cs] = _pack_c(outs[0], outs[1])


def hyena_in_proj(h, mods, g, w, b, conv_w, conv_b, *, tm=512):
    bsz, n, d = h.shape
    nout = w.shape[1]
    tm = _tile(n, tm)
    const = lambda shape: pl.BlockSpec(shape, lambda b_, i: (0,) * len(shape))
    return pl.pallas_call(
        _hyena_in_kernel,
        grid=(bsz // 2, n // tm),
        in_specs=_halo_specs(n, tm, d, 2) + [
            pl.BlockSpec((2, 3, d), lambda b_, i: (b_, 0, 0)),
            const((1, d)), _resident((d, nout), lambda b_, i: (0, 0)), const((1, nout)), const((3, nout)),
            const((1, nout)),
        ],
        out_specs=pl.BlockSpec((None, tm, nout), lambda b_, i: (b_, i, 0)),
        out_shape=jax.ShapeDtypeStruct((bsz // 2, n, nout), jnp.uint32),
        compiler_params=_cparams("parallel", "parallel"),
        name="hyena_in_proj",
    )(h, h, h, mods, g.reshape(1, d), w, b.reshape(1, nout), conv_w, conv_b.reshape(1, nout))


def _hdot(a, b):
    return jnp.dot(a, b, preferred_element_type=F32, precision=lax.Precision.HIGHEST)


def _dot3(a, b):
    ah, bh = a.astype(BF16), b.astype(BF16)
    al, bl = (a - ah.astype(F32)).astype(BF16), (b - bh.astype(F32)).astype(BF16)
    dot = functools.partial(jnp.dot, preferred_element_type=F32)
    return dot(ah, bh) + (dot(al, bh) + dot(ah, bl))


def _filter_kernel(ft_ref, w1_ref, b1_ref, w2_ref, b2_ref, w3_ref, b3_ref, fr_ref, ld_ref, k_ref, ss_ref):
    half, i = pl.program_id(0), pl.program_id(1)
    tt = ft_ref.shape[0]
    c = k_ref.shape[-1]
    ft = ft_ref[...]
    h = jnp.sin(fr_ref[0:1, :] * (_hdot(ft, w1_ref[...]) + b1_ref[...]))
    h = jnp.sin(fr_ref[1:2, :] * (_hdot(h, w2_ref[...]) + b2_ref[...]))
    no_lag = (lax.broadcasted_iota(jnp.int32, (tt, c), 0) + (1 - half) + i) == 0

    @pl.when((half == 0) & (i == 0))
    def _():
        ss_ref[...] = jnp.zeros_like(ss_ref)

    for o in range(k_ref.shape[0]):
        k = (_dot3(h, w3_ref[o]) + b3_ref[o]) * jnp.exp(-ft[:, 0:1] * jnp.exp(ld_ref[o]))
        k = jnp.where(no_lag, 0.0, k)
        k_ref[o] = k
        ss_ref[o] += jnp.broadcast_to(jnp.sum(k * k, axis=0, keepdims=True), ss_ref.shape[1:])


def hyena_filters_time(n, f_w1, f_b1, f_w2, f_b2, f_w3, f_b3, f_freq, log_decay, *, tt=512):
    order, _, d = log_decay.shape
    fh = f_w2.shape[0]
    emb_pad = 64
    lag = np.arange(2 * n)
    lag = np.where(lag < n, lag, 2 * n - lag).astype(np.float32)
    t_norm = lag / np.float32(max(n - 1, 1))
    bands = np.linspace(1e-4, HY_BANDS - 1, HY_BANDS, dtype=np.float32)
    ang = (np.float32(2.0 * math.pi / n) * lag)[:, None] * bands[None, :]
    feats = np.zeros((2 * n, emb_pad), np.float32)
    feats[:, :HY_EMB] = np.concatenate([t_norm[:, None], np.cos(ang), -np.sin(ang)], axis=-1)
    w1 = jnp.zeros((emb_pad, fh), F32).at[:HY_EMB].set(f_w1)
    tt = _tile(n, tt)
    nt = n // tt
    const = lambda shape: pl.BlockSpec(shape, lambda hf, i: (0,) * len(shape))
    return pl.pallas_call(
        _filter_kernel,
        grid=(2, nt),
        in_specs=[
            pl.BlockSpec((tt, emb_pad), lambda hf, i: (hf * nt + i, 0)),
            const((emb_pad, fh)), const((1, fh)), const((fh, fh)), const((1, fh)),
            pl.BlockSpec((None, order, fh, d), lambda hf, i: (hf, 0, 0, 0)),
            pl.BlockSpec((order, None, 1, d), lambda hf, i: (0, hf, 0, 0)),
            const((2, fh)),
            pl.BlockSpec((order, None, 1, d), lambda hf, i: (0, hf, 0, 0)),
        ],
        out_specs=[
            pl.BlockSpec((order, tt, d), lambda hf, i: (0, hf * nt + i, 0)),
            pl.BlockSpec((order, 8, d), lambda hf, i: (0, 0, 0)),
        ],
        out_shape=[jax.ShapeDtypeStruct((order, 2 * n, d), F32), jax.ShapeDtypeStruct((order, 8, d), F32)],
        compiler_params=_cparams("arbitrary", "arbitrary"),
        name="hyena_filter_mlp",
    )(jnp.asarray(feats), w1, f_b1.reshape(1, fh), f_w2, f_b2.reshape(1, fh),
      jnp.transpose(f_w3.reshape(fh, order, 2, d), (2, 1, 0, 3)),
      f_b3.reshape(order, 2, 1, d), f_freq, log_decay.reshape(order, 2, 1, d))


def _fft_dims(n):
    nb = 1 << int(math.floor(math.log2(math.sqrt(2 * n))))
    na = 2 * n // nb
    assert na * nb == 2 * n and na % 16 == 0 and nb % 8 == 0, (n, na, nb)
    return na, nb


@functools.lru_cache(maxsize=None)
def _fft_tables(n):
    na, nb = _fft_dims(n)
    na2, nn = na // 2, 2 * n
    ka = np.arange(na, dtype=np.int64)[None, :, None]
    a = np.arange(na2, dtype=np.int64)[None, None, :]
    b = np.arange(nb, dtype=np.int64)[:, None, None]
    ang = (2.0 * np.pi / nn) * ((ka * (a * nb + b)) % nn)
    mr, mi = np.cos(ang), -np.sin(ang)
    f1 = np.concatenate([np.concatenate([mr, -mi], 2), np.concatenate([mi, mr], 2)], 1)
    mrt, mit = np.swapaxes(mr, 1, 2) / nn, np.swapaxes(mi, 1, 2) / nn
    g1 = np.concatenate([np.concatenate([mrt, mit], 2), np.concatenate([-mit, mrt], 2)], 1)
    kb = np.arange(nb, dtype=np.int64)
    ang2 = (2.0 * np.pi / nb) * ((kb[:, None] * kb[None, :]) % nb)
    er, ei = np.cos(ang2), -np.sin(ang2)
    f2 = np.block([[er, -ei], [ei, er]])
    g2 = np.block([[er, ei], [-ei, er]])
    a_all = np.arange(na, dtype=np.int64)[None, None, :]
    ang_f = (2.0 * np.pi / nn) * ((ka * (a_all * nb + b)) % nn)
    f1_real = np.concatenate([np.cos(ang_f), -np.sin(ang_f)], 1)
    as_bf16 = lambda x: x.astype(np.float32).astype(BF16)
    return dict(f1=as_bf16(f1), f1_real=as_bf16(f1_real), g1=as_bf16(g1), f2=as_bf16(f2), g2=as_bf16(g2))


def _pack_c(re, im):
    rb = lax.bitcast_convert_type(re.astype(BF16).astype(F32), jnp.uint32)
    ib = lax.bitcast_convert_type(im.astype(BF16).astype(F32), jnp.uint32)
    return rb | lax.shift_right_logical(ib, jnp.uint32(16))


def _unpack_c(w):
    re = lax.bitcast_convert_type(w & jnp.uint32(0xFFFF0000), F32)
    im = lax.bitcast_convert_type(lax.shift_left(w, jnp.uint32(16)), F32)
    return re, im


def _stack_bf16(re, im):
    return jnp.concatenate([re, im], axis=0).astype(BF16)


LANES = 128
SUB = 8
SHORT_SEQ = 1024


def _lane_cat(parts):
    return parts[0] if len(parts) == 1 else jnp.concatenate(parts, axis=1)


def _flat_rows(ref):
    return ref.reshape(math.prod(ref.shape[:-1]), ref.shape[-1])


def _fft_conv_kernel(x_ref, f1_ref, k_ref, f2_ref, g2_ref, g1_ref, z_ref, m_ref, skip_ref, o_ref, w_ref, *, n1, n2):
    t = pl.program_id(1)
    npair, na2, jb, _ = x_ref.shape
    na, kblk, nb = 2 * na2, k_ref.shape[1], k_ref.shape[2]
    pitch = w_ref.shape[1] // na
    x2, z2, m2, o2 = _flat_rows(x_ref), _flat_rows(z_ref), _flat_rows(m_ref), _flat_rows(o_ref)
    col = lambda q, j: pl.ds(q * na2 * jb + j, na2, stride=jb)
    lane = lambda v, q: v[:, q * LANES:(q + 1) * LANES]

    @pl.when(t < n1)
    def _():
        for j in range(jb):
            xs = _lane_cat([_stack_bf16(*_unpack_c(x2[col(q, j), :])) for q in range(npair)])
            res = jnp.dot(f1_ref[t * jb + j], xs, preferred_element_type=F32)
            packed = _pack_c(res[:na], res[na:])
            for q in range(npair):
                w_ref[q, pl.ds(t * jb + j, na, stride=pitch), :] = lane(packed, q)

    @pl.when((t >= n1) & (t < n1 + n2))
    def _():
        for k in range(kblk):
            rows = pl.ds(pl.multiple_of(((t - n1) * kblk + k) * pitch, SUB), nb)
            ar, ai = _unpack_c(_lane_cat([w_ref[q, rows, :] for q in range(npair)]))
            x = jnp.dot(f2_ref[...], _stack_bf16(ar, ai), preferred_element_type=F32)
            xr, xi = x[:nb], x[nb:]
            kr = _lane_cat([k_ref[0, k].astype(F32)] * npair)
            ki = _lane_cat([k_ref[1, k].astype(F32)] * npair)
            bv = jnp.dot(g2_ref[...], _stack_bf16(xr * kr - xi * ki, xr * ki + xi * kr), preferred_element_type=F32)
            packed = _pack_c(bv[:nb], bv[nb:])
            for q in range(npair):
                w_ref[q, rows, :] = lane(packed, q)

    @pl.when(t >= n1 + n2)
    def _():
        bb = t - (n1 + n2)
        skip = skip_ref[...]
        for j in range(jb):
            br, bi = _unpack_c(_lane_cat([w_ref[q, pl.ds(bb * jb + j, na, stride=pitch), :] for q in range(npair)]))
            y = jnp.dot(g1_ref[bb * jb + j], _stack_bf16(br, bi), preferred_element_type=F32)
            for q in range(npair):
                rows = col(q, j)
                (zr, zi), (mr, mi) = _unpack_c(z2[rows, :]), _unpack_c(m2[rows, :])
                o2[rows, :] = _pack_c(mr * (lane(y[:na2], q) + skip * zr), mi * (lane(y[na2:], q) + skip * zi))


def fft_long_conv(x, x_off, kf, order, z, z_off, m, m_off, skip, n, c, tables, *, jb=SUB, kblk=8):
    na, nb = _fft_dims(n)
    na2 = na // 2
    jb, kblk = (nb, na) if na * nb <= SHORT_SEQ else (jb, kblk)
    n1, n2 = nb // jb, na // kblk
    npair = x.shape[0]
    view = lambda a: a.reshape(a.shape[0], na2, nb, a.shape[-1])
    xo, zo, mo = x_off // LANES, z_off // LANES, m_off // LANES
    col1 = lambda t: jnp.minimum(t, n1 - 1)
    col3 = lambda t: jnp.clip(t - (n1 + n2), 0, n1 - 1)
    seq = lambda off, col: pl.BlockSpec((npair, na2, jb, LANES), lambda ci, t: (0, 0, col(t), ci + off))
    const = lambda shape: _resident(shape, lambda ci, t: (0,) * len(shape))
    out = pl.pallas_call(
        functools.partial(_fft_conv_kernel, n1=n1, n2=n2),
        grid=(c // LANES, n1 + n2 + n1),
        in_specs=[
            seq(xo, col1),
            const((nb, 2 * na, na)),
            pl.BlockSpec((None, 2, kblk, nb, LANES), lambda ci, t: (order, 0, jnp.clip(t - n1, 0, n2 - 1), 0, ci)),
            const((2 * nb, 2 * nb)), const((2 * nb, 2 * nb)),
            const((nb, na, 2 * na)),
            seq(zo, col3), seq(mo, col3),
            pl.BlockSpec((1, LANES), lambda ci, t: (0, ci)),
        ],
        out_specs=seq(0, col3),
        out_shape=jax.ShapeDtypeStruct((npair, na2, nb, c), jnp.uint32),
        scratch_shapes=[pltpu.VMEM((npair, na * (nb + SUB), LANES), jnp.uint32)],
        compiler_params=_cparams("parallel", "arbitrary"),
        name="fft_long_conv",
    )(view(x), tables["f1"], kf, tables["f2"], tables["g2"], tables["g1"], view(z), view(m), skip.reshape(1, c))
    return out.reshape(npair, n, c)


def _fft_filter_kernel(x_ref, f1_ref, f2_ref, ss_ref, o_ref, w_ref, *, n1):
    t = pl.program_id(1)
    order, na, jb, _ = x_ref.shape
    kblk, nb = o_ref.shape[2], o_ref.shape[3]
    pitch = w_ref.shape[1] // na
    x2 = _flat_rows(x_ref)
    lane = lambda v, q: v[:, q * LANES:(q + 1) * LANES]

    @pl.when(t < n1)
    def _():
        for j in range(jb):
            xs = _lane_cat([x2[pl.ds(o * na * jb + j, na, stride=jb), :] for o in range(order)])
            res = jnp.dot(f1_ref[t * jb + j], xs.astype(BF16), preferred_element_type=F32)
            packed = _pack_c(res[:na], res[na:])
            for o in range(order):
                w_ref[o, pl.ds(t * jb + j, na, stride=pitch), :] = lane(packed, o)

    @pl.when(t >= n1)
    def _():
        scale = _lane_cat([lax.rsqrt(ss_ref[o, 0:1, :] + 1e-12) for o in range(order)])
        for k in range(kblk):
            rows = pl.ds(pl.multiple_of(((t - n1) * kblk + k) * pitch, SUB), nb)
            ar, ai = _unpack_c(_lane_cat([w_ref[o, rows, :] for o in range(order)]))
            x = jnp.dot(f2_ref[...], _stack_bf16(ar, ai), preferred_element_type=F32) * scale
            for o in range(order):
                o_ref[o, 0, k] = lane(x[:nb], o).astype(o_ref.dtype)
                o_ref[o, 1, k] = lane(x[nb:], o).astype(o_ref.dtype)


def fft_filter_spectrum(kt, ss, n, tables, *, jb=SUB, kblk=8):
    na, nb = _fft_dims(n)
    order, _, c = kt.shape
    jb, kblk = (nb, na) if na * nb <= SHORT_SEQ else (jb, kblk)
    n1, n2 = nb // jb, na // kblk
    return pl.pallas_call(
        functools.partial(_fft_filter_kernel, n1=n1),
        grid=(c // LANES, n1 + n2),
        in_specs=[
            pl.BlockSpec((order, na, jb, LANES), lambda ci, t: (0, 0, jnp.minimum(t, n1 - 1), ci)),
            _resident((nb, 2 * na, na), lambda ci, t: (0, 0, 0)),
            _resident((2 * nb, 2 * nb), lambda ci, t: (0, 0)),
            pl.BlockSpec((order, 8, LANES), lambda ci, t: (0, 0, ci)),
        ],
        out_specs=pl.BlockSpec((order, 2, kblk, nb, LANES), lambda ci, t: (0, 0, jnp.maximum(t - n1, 0), 0, ci)),
        out_shape=jax.ShapeDtypeStruct((order, 2, na, nb, c), BF16),
        scratch_shapes=[pltpu.VMEM((order, na * (nb + SUB), LANES), jnp.uint32)],
        compiler_params=_cparams("parallel", "arbitrary"),
        name="fft_filter_spectrum",
    )(kt.reshape(order, na, nb, c), tables["f1_real"], tables["f2"], ss)


def hyena_mixer_block(h, mods, g, prm):
    bsz, n, d = h.shape
    assert bsz % 2 == 0 and d % LANES == 0, "batch rows travel in pairs, channels in LANES-wide columns"
    tables = _fft_tables(n)
    proj = hyena_in_proj(h, mods, g, prm["w_in"], prm["b_in"], prm["conv_w"], prm["conv_b"])
    kt, ss = hyena_filters_time(n, prm["f_w1"], prm["f_b1"], prm["f_w2"], prm["f_b2"], prm["f_w3"],
                                prm["f_b3"], prm["f_freq"], prm["log_decay"])
    kf = fft_filter_spectrum(kt, ss, n, tables)
    z = fft_long_conv(proj, 2 * d, kf, 0, proj, 2 * d, proj, 0, prm["skip"][0], n, d, tables)
    y = fft_long_conv(z, 0, kf, 1, z, 0, proj, d, prm["skip"][1], n, d, tables)
    return linear_residual(y, h, mods[:, 2:3, :], prm["w_out"], prm["b_out"])


def _mlstm_pre_kernel(hp_ref, h_ref, hn_ref, mod_ref, g_ref, wup_ref, cw_ref, cb_ref, wq_ref, wk_ref, wkt_ref, wv_ref,
                      wg_ref, bg_ref, q_ref, kt_ref, v_ref, xc_ref, z_ref, gt_ref, *, k_scale):
    tm = h_ref.shape[0]
    inner = q_ref.shape[-1]
    grp = wq_ref.shape[-1]
    rows = jnp.concatenate([hp_ref[...], h_ref[...], hn_ref[...]], axis=0)
    u = _rms_mod(rows, g_ref[...], mod_ref[0:1, :], mod_ref[1:2, :])
    up = _bdot(u, wup_ref[...])
    xm_h = _mask_seq_ends(up[:, :inner], tm)
    z_ref[...] = up[HALO:HALO + tm, inner:].astype(z_ref.dtype)
    xc_f = _silu(_conv3(xm_h, cw_ref, tm) + cb_ref[...])
    xc_ref[...] = xc_f.astype(xc_ref.dtype)
    xc_all, xm_all = xc_f.astype(BF16), xm_h[HALO:HALO + tm].astype(BF16)
    gates = jnp.zeros(gt_ref.shape, F32) + bg_ref[...]
    for gi in range(inner // grp):
        lo, hi = gi * grp, (gi + 1) * grp
        xc, xm = xc_all[:, lo:hi], xm_all[:, lo:hi]
        q = jnp.dot(xc, wq_ref[gi], preferred_element_type=F32)
        k = jnp.dot(xc, wk_ref[gi], preferred_element_type=F32)
        v = jnp.dot(xm, wv_ref[gi], preferred_element_type=F32)
        q_ref[:, lo:hi] = q.astype(q_ref.dtype)
        kt_ref[lo:hi, :] = (_nt(wkt_ref[gi], xc) * k_scale).astype(kt_ref.dtype)
        v_ref[:, lo:hi] = v.astype(v_ref.dtype)
        gates += (_bdot(q, wg_ref[lo:hi, :]) + _bdot(k, wg_ref[inner + lo:inner + hi, :])
                  + _bdot(v, wg_ref[2 * inner + lo:2 * inner + hi, :]))
    gt_ref[...] = gates


def mlstm_pre(h, mods, g, prm, *, tm=256):
    bsz, n, d = h.shape
    inner = prm["w_up"].shape[1] // 2
    ngate = prm["w_gate"].shape[1]
    grp = prm["wq_dense"].shape[-1]
    tm = _tile(n, tm)
    const = lambda shape: pl.BlockSpec(shape, lambda b_, i: (0,) * len(shape))
    row_spec = lambda w: pl.BlockSpec((None, tm, w), lambda b_, i: (b_, i, 0))
    mm = jax.ShapeDtypeStruct((bsz, n, inner), BF16)
    return pl.pallas_call(
        functools.partial(_mlstm_pre_kernel, k_scale=float((inner // ML_HEADS) ** -0.5)),
        grid=(bsz, n // tm),
        in_specs=_halo_specs(n, tm, d) + [
            pl.BlockSpec((None, 3, d), lambda b_, i: (b_, 0, 0)),
            const((1, d)), _resident((d, 2 * inner), lambda b_, i: (0, 0)), const((3, inner)), const((1, inner)),
            const((inner // grp, grp, grp)), const((inner // grp, grp, grp)), const((inner // grp, grp, grp)),
            const((inner // grp, grp, grp)), const((3 * inner, ngate)), const((1, ngate)),
        ],
        out_specs=[row_spec(inner), pl.BlockSpec((None, inner, tm), lambda b_, i: (b_, 0, i))]
                  + [row_spec(inner)] * 3 + [row_spec(ngate)],
        out_shape=[mm, jax.ShapeDtypeStruct((bsz, inner, n), BF16), mm, mm, mm,
                   jax.ShapeDtypeStruct((bsz, n, ngate), F32)],
        compiler_params=_cparams("parallel", "parallel"),
        name="mlstm_pre",
    )(h, h, h, mods, g.reshape(1, d), prm["w_up"], prm["conv_w"], prm["conv_b"].reshape(1, inner),
      prm["wq_dense"], prm["wk_dense"], jnp.swapaxes(prm["wk_dense"], 1, 2), prm["wv_dense"], prm["w_gate"],
      prm["b_gate"].reshape(1, ngate))


def _dense_blockdiag(w, grp):
    nb, bs, _ = w.shape
    per = grp // bs
    eye = jnp.eye(per, dtype=w.dtype)
    dense = jnp.einsum("gmde,mn->gmdne", w.reshape(nb // per, per, bs, bs), eye)
    return dense.reshape(nb // per, grp, grp).astype(BF16)


def _split3(x):
    p1 = x.astype(BF16)
    r1 = x - p1.astype(F32)
    p2 = r1.astype(BF16)
    p3 = (r1 - p2.astype(F32)).astype(BF16)
    return p1, p2, p3


def _nt(a, b):
    return lax.dot_general(a, b, (((1,), (1,)), ((), ())), preferred_element_type=F32)


def _scan_kernel(*refs, has_init, emit_state, heads_per_step):
    it = iter(refs)
    q_ref, kt_ref, v_ref, gt_ref = next(it), next(it), next(it), next(it)
    allow_ref, allow_t_ref, neg_ref = next(it), next(it), next(it)
    init_refs = [next(it) for _ in range(3)] if has_init else None
    h_ref = next(it)
    final_refs = [next(it) for _ in range(3)] if emit_state else None
    state = c_sc, n_sc, m_sc = next(it), next(it), next(it)
    c = pl.program_id(3)
    ch = q_ref.shape[0]
    dh = q_ref.shape[1] // heads_per_step

    @pl.when(c == 0)
    def _():
        for sc, src in zip(state, init_refs or [None] * 3):
            sc[...] = jnp.zeros_like(sc) if src is None else src[...]

    allow, allow_t, neg = allow_ref[...], allow_t_ref[...], neg_ref[...]
    ones_rows = jnp.ones((SUB, ch), BF16)

    heads = range(heads_per_step)
    cols = [slice(hh * dh, (hh + 1) * dh) for hh in heads]

    def gate_terms(hh):
        i_row = gt_ref[hh, 0:1, :]
        f_row = gt_ref[hh, 1:2, :]
        lf_row = jnp.minimum(f_row, 0.0) - jnp.log1p(jnp.exp(-jnp.abs(f_row)))
        pieces = jnp.concatenate(list(_split3(lf_row)) + [jnp.zeros((SUB - 3, ch), BF16)], axis=0)
        cum_col = _nt(allow, pieces)
        cum_row = jnp.dot(pieces, allow_t, preferred_element_type=F32)
        b_col = cum_col[:, 0:1] + cum_col[:, 1:2] + cum_col[:, 2:3]
        b_row = cum_row[0:1, :] + cum_row[1:2, :] + cum_row[2:3, :]
        total = jnp.sum(lf_row, axis=1, keepdims=True)
        e_row = i_row - b_row
        m_st = m_sc[hh, 0:1, 0:1]
        dmat = (b_col + e_row) + neg
        a_col = b_col + m_st
        m_j = jnp.maximum(a_col, jnp.max(dmat, axis=1, keepdims=True))
        g_row = total + e_row
        m_new = jnp.maximum(total + m_st, jnp.max(g_row, axis=1, keepdims=True))
        return dict(w_inter=jnp.exp(a_col - m_j), dexp=jnp.exp(dmat - m_j), floor=jnp.exp(-m_j), m_new=m_new,
                    decay=jnp.exp(total + m_st - m_new), w_row=jnp.exp(g_row - m_new))

    def read_out(hh, t):
        q, kt, v = q_ref[:, cols[hh]], kt_ref[cols[hh], :], v_ref[:, cols[hh]]
        s = jnp.dot(q, kt, preferred_element_type=F32) * t["dexp"]
        num = t["w_inter"] * jnp.dot(q, c_sc[hh].astype(BF16), preferred_element_type=F32) + _bdot(s, v)
        qn = (t["w_inter"] * jnp.sum(q.astype(F32) * n_sc[hh, 0:1, :], axis=1, keepdims=True)
              + jnp.sum(s, axis=1, keepdims=True))
        h_ref[:, cols[hh]] = (num / jnp.maximum(jnp.abs(qn), t["floor"])).astype(h_ref.dtype)

    def update(hh, t):
        kwt = (kt_ref[cols[hh], :].astype(F32) * t["w_row"]).astype(BF16)
        c_sc[hh] = t["decay"] * c_sc[hh] + jnp.dot(kwt, v_ref[:, cols[hh]], preferred_element_type=F32)
        n_sc[hh] = t["decay"] * n_sc[hh] + _nt(ones_rows, kwt)
        m_sc[hh] = jnp.broadcast_to(t["m_new"], m_sc.shape[1:])

    terms = [gate_terms(hh) for hh in heads]
    for hh in heads:
        read_out(hh, terms[hh])
    for hh in heads:
        update(hh, terms[hh])

    if emit_state:
        @pl.when(c == pl.num_programs(3) - 1)
        def _():
            for dst, sc in zip(final_refs, state):
                dst[...] = sc[...]


@functools.lru_cache(maxsize=None)
def _scan_masks(ch):
    r = np.arange(ch)
    causal = (r[None, :] <= r[:, None]).astype(np.float32)
    allow = np.stack([causal, causal.T])
    neg = np.where(allow > 0, 0.0, -np.inf).astype(np.float32)
    return allow.astype(BF16), np.transpose(allow, (0, 2, 1)).astype(BF16), neg


def mlstm_scan(q, kt, v, gates_t, init=None, *, emit_state=False):
    bsz, n, inner = q.shape
    nh = gates_t.shape[2]
    dh = inner // nh
    hp = nh if nh <= 4 else (2 if nh % 2 == 0 else 1)
    ch = min(SCAN_CHUNK, n)
    nc = n // ch
    ceff = lambda d, c: c + d * (nc - 1 - 2 * c)
    seq = pl.BlockSpec((None, ch, hp * dh), lambda d, b, hd, c: (b, ceff(d, c), hd))
    state_shapes = [(dh, dh), (SUB, dh), (SUB, 128)]
    st_specs = [pl.BlockSpec((None, None, hp) + s, lambda d, b, hd, c: (d, b, hd, 0, 0)) for s in state_shapes]
    by_dir = pl.BlockSpec((None, ch, ch), lambda d, b, hd, c: (d, 0, 0))
    in_specs = [seq, pl.BlockSpec((None, hp * dh, ch), lambda d, b, hd, c: (b, hd, ceff(d, c))), seq,
                pl.BlockSpec((None, None, hp, 2, ch), lambda d, b, hd, c: (b, d, hd, 0, ceff(d, c))),
                by_dir, by_dir, by_dir]
    args = [q, kt, v, gates_t, *_scan_masks(ch)]
    if init is not None:
        in_specs += st_specs
        args += list(init)
    out_specs = [pl.BlockSpec((None, None, ch, hp * dh), lambda d, b, hd, c: (d, b, ceff(d, c), hd))]
    out_shape = [jax.ShapeDtypeStruct((2, bsz, n, inner), BF16)]
    if emit_state:
        out_specs += st_specs
        out_shape += [jax.ShapeDtypeStruct((2, bsz, nh) + s, F32) for s in state_shapes]
    res = pl.pallas_call(
        functools.partial(_scan_kernel, has_init=init is not None, emit_state=emit_state, heads_per_step=hp),
        grid=(2, bsz, nh // hp, nc),
        in_specs=in_specs,
        out_specs=out_specs,
        out_shape=out_shape,
        scratch_shapes=[pltpu.VMEM((hp,) + s, F32) for s in state_shapes],
        compiler_params=_cparams("parallel", "parallel", "parallel", "arbitrary"),
        name="mlstm_scan",
    )(*args)
    return (res[0], tuple(res[1:])) if emit_state else (res[0], None)


def _mlstm_post_kernel(hf_ref, hb_ref, xc_ref, z_ref, res_ref, gate_ref, ng_ref, sk_ref, wd_ref, o_ref):
    inner = xc_ref.shape[-1]
    dh = inner // ML_HEADS
    acc = jnp.zeros(o_ref.shape, F32)
    for hd in range(ML_HEADS):
        lo, hi = hd * dh, (hd + 1) * dh
        x = hf_ref[:, lo:hi].astype(F32) + hb_ref[:, lo:hi].astype(F32)
        xz = x - jnp.mean(x, axis=-1, keepdims=True)
        var = jnp.mean(xz * xz, axis=-1, keepdims=True)
        hn = xz * lax.rsqrt(var + ML_NORM_EPS) * ng_ref[:, lo:hi]
        t = (hn + sk_ref[:, lo:hi] * xc_ref[:, lo:hi].astype(F32)) * _silu(z_ref[:, lo:hi].astype(F32))
        acc += _bdot(t, wd_ref[lo:hi, :])
    o_ref[...] = res_ref[...] + gate_ref[...] * acc


def mlstm_post(hs, xc, z, h_res, gate, prm, *, tm=256):
    bsz, n, inner = xc.shape
    d = h_res.shape[-1]
    tm = _tile(n, tm)
    const = lambda shape: pl.BlockSpec(shape, lambda b_, i: (0,) * len(shape))
    return pl.pallas_call(
        _mlstm_post_kernel,
        grid=(bsz, n // tm),
        in_specs=[
            pl.BlockSpec((None, None, tm, inner), lambda b_, i: (0, b_, i, 0)),
            pl.BlockSpec((None, None, tm, inner), lambda b_, i: (1, b_, i, 0)),
            pl.BlockSpec((None, tm, inner), lambda b_, i: (b_, i, 0)),
            pl.BlockSpec((None, tm, inner), lambda b_, i: (b_, i, 0)),
            pl.BlockSpec((None, tm, d), lambda b_, i: (b_, i, 0)),
            pl.BlockSpec((None, 1, d), lambda b_, i: (b_, 0, 0)),
            const((1, inner)), const((1, inner)), _resident((inner, d), lambda b_, i: (0, 0)),
        ],
        out_specs=pl.BlockSpec((None, tm, d), lambda b_, i: (b_, i, 0)),
        out_shape=jax.ShapeDtypeStruct((bsz, n, d), F32),
        compiler_params=_cparams("parallel", "parallel"),
        name="mlstm_post",
    )(hs, hs, xc, z, h_res, gate, prm["norm_g"].reshape(1, inner), prm["skip"].reshape(1, inner), prm["w_down"])


def _gates_by_head(gates, nh):
    bsz, n, _ = gates.shape
    return jnp.transpose(gates.reshape(bsz, n, 2, 2, nh), (0, 2, 4, 3, 1))


def mlstm_mixer_block(h_ctx, h_lat, mods_ctx, mods_lat, g, prm, *, with_ctx_out):
    pc = mlstm_pre(h_ctx, mods_ctx, g, prm)
    pq = mlstm_pre(h_lat, mods_lat, g, prm)
    hs_ctx, state = mlstm_scan(pc[0], pc[1], pc[2], _gates_by_head(pc[5], ML_HEADS), emit_state=True)
    hs_lat, _ = mlstm_scan(pq[0], pq[1], pq[2], _gates_by_head(pq[5], ML_HEADS), init=state)
    out_lat = mlstm_post(hs_lat, pq[3], pq[4], h_lat, mods_lat[:, 2:3, :], prm)
    out_ctx = mlstm_post(hs_ctx, pc[3], pc[4], h_ctx, mods_ctx[:, 2:3, :], prm) if with_ctx_out else None
    return out_ctx, out_lat


def _pos_embed_2d(n_tokens, d):
    rows = n_tokens // GRID_W
    quarter = d // 4
    omega = 1.0 / (POS_BASE ** (jnp.arange(quarter, dtype=F32) / quarter))
    ang_r = jnp.arange(rows, dtype=F32)[:, None] * omega
    ang_c = jnp.arange(GRID_W, dtype=F32)[:, None] * omega
    row_emb = jnp.concatenate([jnp.sin(ang_r), jnp.cos(ang_r)], axis=-1)
    col_emb = jnp.concatenate([jnp.sin(ang_c), jnp.cos(ang_c)], axis=-1)
    emb = jnp.concatenate([
        jnp.broadcast_to(row_emb[:, None, :], (rows, GRID_W, d // 2)),
        jnp.broadcast_to(col_emb[None, :, :], (rows, GRID_W, d // 2))], axis=-1)
    return emb.reshape(rows * GRID_W, d)


def kernel(x, c, ctx, c_ctx, ada_w, ada_b, norm_g, final_g, ffn_w_in, ffn_w_out, hy_w_in, hy_b_in, hy_conv_w, hy_conv_b, hy_f_w1, hy_f_b1, hy_f_w2, hy_f_b2, hy_f_w3, hy_f_b3, hy_f_freq, hy_log_decay, hy_skip, hy_w_out, hy_b_out, ml_w_up, ml_conv_w, ml_conv_b, ml_w_q, ml_w_k, ml_w_v, ml_w_gate, ml_b_gate, ml_norm_g, ml_skip, ml_w_down):
    bsz, n_lat, d = x.shape
    n_ctx = ctx.shape[1]
    depth = ada_w.shape[0]
    n_mixers = 2

    cond_rows = 8
    cond = jnp.zeros((cond_rows, d), F32).at[:bsz].set(c).at[bsz].set(c_ctx)
    mods_all = ada_modulation(cond, ada_w, ada_b).reshape(depth, cond_rows, N_MOD, d)

    ffn_w_in = ffn_w_in.astype(BF16)
    ffn_w_out = ffn_w_out.astype(BF16)
    pos = _pos_embed_2d(n_lat, d)

    h_lat = x
    h_ctx = ctx.reshape(1, bsz * n_ctx, d)
    for l in range(depth):
        last = l == depth - 1
        j = l // n_mixers
        m_lat = mods_all[l, :bsz]
        m_ctx = mods_all[l, bsz:bsz + 1]
        m_ctx_b = jnp.broadcast_to(m_ctx, (bsz, N_MOD, d))

        h_lat = ffn_block(h_lat, m_lat[:, 0:3], norm_g[l, 0], ffn_w_in, ffn_w_out, widx=(l, 0),
                          pos=pos if l == 0 else None)
        h_ctx = ffn_block(h_ctx, m_ctx[:, 0:3], norm_g[l, 0], ffn_w_in, ffn_w_out, widx=(l, 0))

        h_ctx = h_ctx.reshape(bsz, n_ctx, d)
        if l % n_mixers == 0:
            prm = dict(w_in=hy_w_in[j].astype(BF16), b_in=hy_b_in[j], conv_w=hy_conv_w[j], conv_b=hy_conv_b[j],
                       f_w1=hy_f_w1[j], f_b1=hy_f_b1[j], f_w2=hy_f_w2[j], f_b2=hy_f_b2[j], f_w3=hy_f_w3[j],
                       f_b3=hy_f_b3[j], f_freq=hy_f_freq[j], log_decay=hy_log_decay[j], skip=hy_skip[j],
                       w_out=hy_w_out[j].astype(BF16), b_out=hy_b_out[j])
            h_lat = hyena_mixer_block(h_lat, m_lat[:, 3:6], norm_g[l, 1], prm)
            if not last:
                h_ctx = hyena_mixer_block(h_ctx, m_ctx_b[:, 3:6], norm_g[l, 1], prm)
        else:
            prm = dict(w_up=ml_w_up[j].astype(BF16), conv_w=ml_conv_w[j], conv_b=ml_conv_b[j],
                       wq_dense=_dense_blockdiag(ml_w_q[j], ML_QKV_GROUP),
                       wk_dense=_dense_blockdiag(ml_w_k[j], ML_QKV_GROUP),
                       wv_dense=_dense_blockdiag(ml_w_v[j], ML_QKV_GROUP),
                       w_gate=ml_w_gate[j].astype(BF16), b_gate=ml_b_gate[j], norm_g=ml_norm_g[j],
                       skip=ml_skip[j], w_down=ml_w_down[j].astype(BF16))
            new_ctx, h_lat = mlstm_mixer_block(h_ctx, h_lat, m_ctx_b[:, 3:6], m_lat[:, 3:6], norm_g[l, 1], prm,
                                               with_ctx_out=not last)
            h_ctx = h_ctx if last else new_ctx
        h_ctx = h_ctx.reshape(1, bsz * n_ctx, d)

        h_lat = ffn_block(h_lat, m_lat[:, 6:9], norm_g[l, 2], ffn_w_in, ffn_w_out, widx=(l, 1),
                          final_g=final_g if last else None)
        if not last:
            h_ctx = ffn_block(h_ctx, m_ctx[:, 6:9], norm_g[l, 2], ffn_w_in, ffn_w_out, widx=(l, 1))
    return h_lat
```

```python
import functools
import math

import jax
import jax.numpy as jnp
import numpy as np
from jax import lax
from jax.experimental import pallas as pl
from jax.experimental.pallas import tpu as pltpu

F32 = jnp.float32
BF16 = jnp.bfloat16

NORM_EPS = 1e-6
GRID_W = 64
POS_BASE = 10000.0
N_MOD = 9
HY_EMB = 33
HY_BANDS = (HY_EMB - 1) // 2
ML_HEADS = 4
ML_NORM_EPS = 1e-5
ML_QKV_GROUP = 256
SCAN_CHUNK = 256
HALO = 8
CONV_COLS = 256

V7X_VMEM_LIMIT = 56 * 1024 * 1024


def _cparams(*sem):
    return pltpu.CompilerParams(dimension_semantics=sem, vmem_limit_bytes=V7X_VMEM_LIMIT)


def _bdot(a, b):
    return jnp.dot(a.astype(BF16), b.astype(BF16), preferred_element_type=F32)


def _rms_mod(h, g, shift, scale):
    y = h * lax.rsqrt(jnp.mean(h * h, axis=-1, keepdims=True) + NORM_EPS)
    return (y * g) * (1.0 + scale) + shift


def _silu(x):
    return x * (1.0 / (1.0 + jnp.exp(-x)))


def _tile(n, pref):
    if n <= pref:
        return n
    for t in range(pref, 7, -1):
        if n % t == 0 and t % 8 == 0:
            return t
    return n


def _mod_kernel(c_ref, w_ref, b_ref, o_ref):
    o_ref[...] = _bdot(_silu(c_ref[...]), w_ref[...]) + b_ref[...]


def ada_modulation(cond, ada_w, ada_b):
    n_layers, d, n_out = ada_w.shape
    r = cond.shape[0]
    tn = _tile(n_out, 2304) if n_out % 128 == 0 else n_out
    return pl.pallas_call(
        _mod_kernel,
        grid=(n_layers, n_out // tn),
        in_specs=[
            pl.BlockSpec((r, d), lambda l, j: (0, 0)),
            pl.BlockSpec((None, d, tn), lambda l, j: (l, 0, j)),
            pl.BlockSpec((None, 1, tn), lambda l, j: (l, 0, j)),
        ],
        out_specs=pl.BlockSpec((None, r, tn), lambda l, j: (l, 0, j)),
        out_shape=jax.ShapeDtypeStruct((n_layers, r, n_out), F32),
        compiler_params=_cparams("parallel", "parallel"),
        name="ada_modulation",
    )(cond, ada_w, ada_b.reshape(n_layers, 1, n_out))


def _ffn_kernel(*refs, has_pos, final):
    it = iter(refs)
    h_ref = next(it)
    pos_ref = next(it) if has_pos else None
    mod_ref, g_ref, wg_ref, wv_ref, wo_ref = next(it), next(it), next(it), next(it), next(it)
    fg_ref = next(it) if final else None
    o_ref = next(it)
    h = h_ref[...]
    if has_pos:
        h = h + pos_ref[...]
    u = _rms_mod(h, g_ref[...], mod_ref[0:1, :], mod_ref[1:2, :]).astype(BF16)
    gate = jnp.dot(u, wg_ref[...], preferred_element_type=F32)
    val = jnp.dot(u, wv_ref[...], preferred_element_type=F32)
    out = h + (0.5 * mod_ref[2:3, :]) * _bdot(_silu(gate) * val, wo_ref[...])
    if final:
        out = out * lax.rsqrt(jnp.mean(out * out, axis=-1, keepdims=True) + NORM_EPS) * fg_ref[...]
    o_ref[...] = out


def _resident(shape, index_map):
    return pl.BlockSpec(shape, index_map, pipeline_mode=pl.Buffered(1))


def ffn_block(h, mods, g, w_in, w_out, *, widx=(), pos=None, final_g=None, tm=1024):
    bsz, n, d = h.shape
    ff = w_out.shape[-2]
    has_pos, final = pos is not None, final_g is not None
    tm = _tile(n, tm // 2 if has_pos else tm)
    lead = (None,) * len(widx)
    in_specs = [pl.BlockSpec((None, tm, d), lambda b, i: (b, i, 0))]
    args = [h]
    if has_pos:
        in_specs.append(pl.BlockSpec((tm, d), lambda b, i: (i, 0)))
        args.append(pos)
    in_specs += [
        pl.BlockSpec((None, 3, d), lambda b, i: (b, 0, 0)),
        pl.BlockSpec((1, d), lambda b, i: (0, 0)),
        _resident(lead + (d, ff), lambda b, i: widx + (0, 0)),
        _resident(lead + (d, ff), lambda b, i: widx + (0, 1)),
        _resident(lead + (ff, d), lambda b, i: widx + (0, 0)),
    ]
    args += [mods, g.reshape(1, d), w_in, w_in, w_out]
    if final:
        in_specs.append(pl.BlockSpec((1, d), lambda b, i: (0, 0)))
        args.append(final_g.reshape(1, d))
    return pl.pallas_call(
        functools.partial(_ffn_kernel, has_pos=has_pos, final=final),
        grid=(bsz, n // tm),
        in_specs=in_specs,
        out_specs=pl.BlockSpec((None, tm, d), lambda b, i: (b, i, 0)),
        out_shape=jax.ShapeDtypeStruct((bsz, n, d), F32),
        compiler_params=_cparams("parallel", "parallel"),
        name="ffn_block",
    )(*args)


def _linres_kernel(y_ref, h_ref, gate_ref, w_ref, b_ref, o_ref):
    for r, y in enumerate(_unpack_c(y_ref[...])):
        o_ref[r] = h_ref[r] + gate_ref[r] * (_bdot(y, w_ref[...]) + b_ref[...])


def linear_residual(y, h, gate, w, b, *, tm=512):
    npair, n, kdim = y.shape
    d = h.shape[-1]
    tm = _tile(n, tm)
    return pl.pallas_call(
        _linres_kernel,
        grid=(npair, n // tm),
        in_specs=[
            pl.BlockSpec((None, tm, kdim), lambda q, i: (q, i, 0)),
            pl.BlockSpec((2, tm, d), lambda q, i: (q, i, 0)),
            pl.BlockSpec((2, 1, d), lambda q, i: (q, 0, 0)),
            _resident((kdim, d), lambda q, i: (0, 0)),
            pl.BlockSpec((1, d), lambda q, i: (0, 0)),
        ],
        out_specs=pl.BlockSpec((2, tm, d), lambda q, i: (q, i, 0)),
        out_shape=jax.ShapeDtypeStruct(h.shape, F32),
        compiler_params=_cparams("parallel", "parallel"),
        name="linear_residual",
    )(y, h, gate, w, b.reshape(1, d))


def _halo_specs(n, tm, d, nbatch=None):
    hb, last = tm // HALO, n // HALO - 1
    return [
        pl.BlockSpec((nbatch, HALO, d), lambda b, i: (b, jnp.maximum(i * hb - 1, 0), 0)),
        pl.BlockSpec((nbatch, tm, d), lambda b, i: (b, i, 0)),
        pl.BlockSpec((nbatch, HALO, d), lambda b, i: (b, jnp.minimum((i + 1) * hb, last), 0)),
    ]


def _mask_seq_ends(p, tm):
    i, ni = pl.program_id(1), pl.num_programs(1)
    head = jnp.where(i == 0, 0.0, p[:HALO])
    tail = jnp.where(i == ni - 1, 0.0, p[tm + HALO:])
    return jnp.concatenate([head, p[HALO:tm + HALO], tail], axis=0)


def _conv3(p, cw_ref, tm):
    rows = p.shape[0]
    prev = pltpu.roll(p, 1, 0)[HALO:HALO + tm]
    nxt = pltpu.roll(p, rows - 1, 0)[HALO:HALO + tm]
    return prev * cw_ref[0:1, :] + p[HALO:HALO + tm] * cw_ref[1:2, :] + nxt * cw_ref[2:3, :]


def _hyena_in_kernel(hp_ref, h_ref, hn_ref, mod_ref, g_ref, w_ref, b_ref, cw_ref, cb_ref, o_ref):
    tm = h_ref.shape[1]
    us = []
    for r in range(2):
        rows = jnp.concatenate([hp_ref[r], h_ref[r], hn_ref[r]], axis=0)
        us.append(_rms_mod(rows, g_ref[...], mod_ref[r, 0:1, :], mod_ref[r, 1:2, :]).astype(BF16))
    nout = o_ref.shape[-1]
    step = CONV_COLS if nout % CONV_COLS == 0 else nout
    for lo in range(0, nout, step):
        cs = slice(lo, lo + step)
        outs = []
        for u in us:
            p = _mask_seq_ends(jnp.dot(u, w_ref[:, cs], preferred_element_type=F32) + b_ref[:, cs], tm)
            outs.append(_conv3(p, cw_ref.at[:, cs], tm) + cb_ref[:, cs])
        o_ref[:, cs] = _pack_c(outs[0], outs[1])


def hyena_in_proj(h, mods, g, w, b, conv_w, conv_b, *, tm=512):
    bsz, n, d = h.shape
    nout = w.shape[1]
    tm = _tile(n, tm)
    const = lambda shape: pl.BlockSpec(shape, lambda b_, i: (0,) * len(shape))
    return pl.pallas_call(
        _hyena_in_kernel,
        grid=(bsz // 2, n // tm),
        in_specs=_halo_specs(n, tm, d, 2) + [
            pl.BlockSpec((2, 3, d), lambda b_, i: (b_, 0, 0)),
            const((1, d)), _resident((d, nout), lambda b_, i: (0, 0)), const((1, nout)), const((3, nout)),
            const((1, nout)),
        ],
        out_specs=pl.BlockSpec((None, tm, nout), lambda b_, i: (b_, i, 0)),
        out_shape=jax.ShapeDtypeStruct((bsz // 2, n, nout), jnp.uint32),
        compiler_params=_cparams("parallel", "parallel"),
        name="hyena_in_proj",
    )(h, h, h, mods, g.reshape(1, d), w, b.reshape(1, nout), conv_w, conv_b.reshape(1, nout))


def _hdot(a, b):
    return jnp.dot(a, b, preferred_element_type=F32, precision=lax.Precision.HIGHEST)


def _dot3(a, b):
    ah, bh = a.astype(BF16), b.astype(BF16)
    al, bl = (a - ah.astype(F32)).astype(BF16), (b - bh.astype(F32)).astype(BF16)
    dot = functools.partial(jnp.dot, preferred_element_type=F32)
    return dot(ah, bh) + (dot(al, bh) + dot(ah, bl))


def _filter_kernel(ft_ref, w1_ref, b1_ref, w2_ref, b2_ref, w3_ref, b3_ref, fr_ref, ld_ref, k_ref, ss_ref):
    half, i = pl.program_id(0), pl.program_id(1)
    tt = ft_ref.shape[0]
    c = k_ref.shape[-1]
    ft = ft_ref[...]
    h = jnp.sin(fr_ref[0:1, :] * (_hdot(ft, w1_ref[...]) + b1_ref[...]))
    h = jnp.sin(fr_ref[1:2, :] * (_hdot(h, w2_ref[...]) + b2_ref[...]))
    no_lag = (lax.broadcasted_iota(jnp.int32, (tt, c), 0) + (1 - half) + i) == 0

    @pl.when((half == 0) & (i == 0))
    def _():
        ss_ref[...] = jnp.zeros_like(ss_ref)

    for o in range(k_ref.shape[0]):
        k = (_dot3(h, w3_ref[o]) + b3_ref[o]) * jnp.exp(-ft[:, 0:1] * jnp.exp(ld_ref[o]))
        k = jnp.where(no_lag, 0.0, k)
        k_ref[o] = k
        ss_ref[o] += jnp.broadcast_to(jnp.sum(k * k, axis=0, keepdims=True), ss_ref.shape[1:])


def hyena_filters_time(n, f_w1, f_b1, f_w2, f_b2, f_w3, f_b3, f_freq, log_decay, *, tt=512):
    order, _, d = log_decay.shape
    fh = f_w2.shape[0]
    emb_pad = 64
    lag = np.arange(2 * n)
    lag = np.where(lag < n, lag, 2 * n - lag).astype(np.float32)
    t_norm = lag / np.float32(max(n - 1, 1))
    bands = np.linspace(1e-4, HY_BANDS - 1, HY_BANDS, dtype=np.float32)
    ang = (np.float32(2.0 * math.pi / n) * lag)[:, None] * bands[None, :]
    feats = np.zeros((2 * n, emb_pad), np.float32)
    feats[:, :HY_EMB] = np.concatenate([t_norm[:, None], np.cos(ang), -np.sin(ang)], axis=-1)
    w1 = jnp.zeros((emb_pad, fh), F32).at[:HY_EMB].set(f_w1)
    tt = _tile(n, tt)
    nt = n // tt
    const = lambda shape: pl.BlockSpec(shape, lambda hf, i: (0,) * len(shape))
    return pl.pallas_call(
        _filter_kernel,
        grid=(2, nt),
        in_specs=[
            pl.BlockSpec((tt, emb_pad), lambda hf, i: (hf * nt + i, 0)),
            const((emb_pad, fh)), const((1, fh)), const((fh, fh)), const((1, fh)),
            pl.BlockSpec((None, order, fh, d), lambda hf, i: (hf, 0, 0, 0)),
            pl.BlockSpec((order, None, 1, d), lambda hf, i: (0, hf, 0, 0)),
            const((2, fh)),
            pl.BlockSpec((order, None, 1, d), lambda hf, i: (0, hf, 0, 0)),
        ],
        out_specs=[
            pl.BlockSpec((order, tt, d), lambda hf, i: (0, hf * nt + i, 0)),
            pl.BlockSpec((order, 8, d), lambda hf, i: (0, 0, 0)),
        ],
        out_shape=[jax.ShapeDtypeStruct((order, 2 * n, d), F32), jax.ShapeDtypeStruct((order, 8, d), F32)],
        compiler_params=_cparams("arbitrary", "arbitrary"),
        name="hyena_filter_mlp",
    )(jnp.asarray(feats), w1, f_b1.reshape(1, fh), f_w2, f_b2.reshape(1, fh),
      jnp.transpose(f_w3.reshape(fh, order, 2, d), (2, 1, 0, 3)),
      f_b3.reshape(order, 2, 1, d), f_freq, log_decay.reshape(order, 2, 1, d))


def _fft_dims(n):
    nb = 1 << int(math.floor(math.log2(math.sqrt(2 * n))))
    na = 2 * n // nb
    assert na * nb == 2 * n and na % 16 == 0 and nb % 8 == 0, (n, na, nb)
    return na, nb


@functools.lru_cache(maxsize=None)
def _fft_tables(n):
    na, nb = _fft_dims(n)
    na2, nn = na // 2, 2 * n
    ka = np.arange(na, dtype=np.int64)[None, :, None]
    a = np.arange(na2, dtype=np.int64)[None, None, :]
    b = np.arange(nb, dtype=np.int64)[:, None, None]
    ang = (2.0 * np.pi / nn) * ((ka * (a * nb + b)) % nn)
    mr, mi = np.cos(ang), -np.sin(ang)
    f1 = np.concatenate([np.concatenate([mr, -mi], 2), np.concatenate([mi, mr], 2)], 1)
    mrt, mit = np.swapaxes(mr, 1, 2) / nn, np.swapaxes(mi, 1, 2) / nn
    g1 = np.concatenate([np.concatenate([mrt, mit], 2), np.concatenate([-mit, mrt], 2)], 1)
    kb = np.arange(nb, dtype=np.int64)
    ang2 = (2.0 * np.pi / nb) * ((kb[:, None] * kb[None, :]) % nb)
    er, ei = np.cos(ang2), -np.sin(ang2)
    f2 = np.block([[er, -ei], [ei, er]])
    g2 = np.block([[er, ei], [-ei, er]])
    a_all = np.arange(na, dtype=np.int64)[None, None, :]
    ang_f = (2.0 * np.pi / nn) * ((ka * (a_all * nb + b)) % nn)
    f1_real = np.concatenate([np.cos(ang_f), -np.sin(ang_f)], 1)
    as_bf16 = lambda x: x.astype(np.float32).astype(BF16)
    return dict(f1=as_bf16(f1), f1_real=as_bf16(f1_real), g1=as_bf16(g1), f2=as_bf16(f2), g2=as_bf16(g2))


def _pack_c(re, im):
    rb = lax.bitcast_convert_type(re.astype(BF16).astype(F32), jnp.uint32)
    ib = lax.bitcast_convert_type(im.astype(BF16).astype(F32), jnp.uint32)
    return rb | lax.shift_right_logical(ib, jnp.uint32(16))


def _unpack_c(w):
    re = lax.bitcast_convert_type(w & jnp.uint32(0xFFFF0000), F32)
    im = lax.bitcast_convert_type(lax.shift_left(w, jnp.uint32(16)), F32)
    return re, im


def _stack_bf16(re, im):
    return jnp.concatenate([re, im], axis=0).astype(BF16)


LANES = 128
SUB = 8
SHORT_SEQ = 1024


def _lane_cat(parts):
    return parts[0] if len(parts) == 1 else jnp.concatenate(parts, axis=1)


def _flat_rows(ref):
    return ref.reshape(math.prod(ref.shape[:-1]), ref.shape[-1])


def _fft_conv_kernel(x_ref, f1_ref, k_ref, f2_ref, g2_ref, g1_ref, z_ref, m_ref, skip_ref, o_ref, w_ref, *, n1, n2):
    t = pl.program_id(1)
    npair, na2, jb, _ = x_ref.shape
    na, kblk, nb = 2 * na2, k_ref.shape[1], k_ref.shape[2]
    pitch = w_ref.shape[1] // na
    x2, z2, m2, o2 = _flat_rows(x_ref), _flat_rows(z_ref), _flat_rows(m_ref), _flat_rows(o_ref)
    col = lambda q, j: pl.ds(q * na2 * jb + j, na2, stride=jb)
    lane = lambda v, q: v[:, q * LANES:(q + 1) * LANES]

    @pl.when(t < n1)
    def _():
        for j in range(jb):
            xs = _lane_cat([_stack_bf16(*_unpack_c(x2[col(q, j), :])) for q in range(npair)])
            res = jnp.dot(f1_ref[t * jb + j], xs, preferred_element_type=F32)
            packed = _pack_c(res[:na], res[na:])
            for q in range(npair):
                w_ref[q, pl.ds(t * jb + j, na, stride=pitch), :] = lane(packed, q)

    @pl.when((t >= n1) & (t < n1 + n2))
    def _():
        for k in range(kblk):
            rows = pl.ds(pl.multiple_of(((t - n1) * kblk + k) * pitch, SUB), nb)
            ar, ai = _unpack_c(_lane_cat([w_ref[q, rows, :] for q in range(npair)]))
            x = jnp.dot(f2_ref[...], _stack_bf16(ar, ai), preferred_element_type=F32)
            xr, xi = x[:nb], x[nb:]
            kr = _lane_cat([k_ref[0, k].astype(F32)] * npair)
            ki = _lane_cat([k_ref[1, k].astype(F32)] * npair)
            bv = jnp.dot(g2_ref[...], _stack_bf16(xr * kr - xi * ki, xr * ki + xi * kr), preferred_element_type=F32)
            packed = _pack_c(bv[:nb], bv[nb:])
            for q in range(npair):
                w_ref[q, rows, :] = lane(packed, q)

    @pl.when(t >= n1 + n2)
    def _():
        bb = t - (n1 + n2)
        skip = skip_ref[...]
        for j in range(jb):
            br, bi = _unpack_c(_lane_cat([w_ref[q, pl.ds(bb * jb + j, na, stride=pitch), :] for q in range(npair)]))
            y = jnp.dot(g1_ref[bb * jb + j], _stack_bf16(br, bi), preferred_element_type=F32)
            for q in range(npair):
                rows = col(q, j)
                (zr, zi), (mr, mi) = _unpack_c(z2[rows, :]), _unpack_c(m2[rows, :])
                o2[rows, :] = _pack_c(mr * (lane(y[:na2], q) + skip * zr), mi * (lane(y[na2:], q) + skip * zi))


def fft_long_conv(x, x_off, kf, order, z, z_off, m, m_off, skip, n, c, tables, *, jb=16, kblk=32):
    na, nb = _fft_dims(n)
    na2 = na // 2
    jb, kblk = (nb, na) if na * nb <= SHORT_SEQ else (min(jb, nb), min(kblk, na))
    n1, n2 = nb // jb, na // kblk
    npair = x.shape[0]
    view = lambda a: a.reshape(a.shape[0], na2, nb, a.shape[-1])
    xo, zo, mo = x_off // LANES, z_off // LANES, m_off // LANES
    col1 = lambda t: jnp.minimum(t, n1 - 1)
    col3 = lambda t: jnp.clip(t - (n1 + n2), 0, n1 - 1)
    seq = lambda off, col: pl.BlockSpec((npair, na2, jb, LANES), lambda ci, t: (0, 0, col(t), ci + off))
    const = lambda shape: _resident(shape, lambda ci, t: (0,) * len(shape))
    out = pl.pallas_call(
        functools.partial(_fft_conv_kernel, n1=n1, n2=n2),
        grid=(c // LANES, n1 + n2 + n1),
        in_specs=[
            seq(xo, col1),
            const((nb, 2 * na, na)),
            pl.BlockSpec((None, 2, kblk, nb, LANES), lambda ci, t: (order, 0, jnp.clip(t - n1, 0, n2 - 1), 0, ci)),
            const((2 * nb, 2 * nb)), const((2 * nb, 2 * nb)),
            const((nb, na, 2 * na)),
            seq(zo, col3), seq(mo, col3),
            pl.BlockSpec((1, LANES), lambda ci, t: (0, ci)),
        ],
        out_specs=seq(0, col3),
        out_shape=jax.ShapeDtypeStruct((npair, na2, nb, c), jnp.uint32),
        scratch_shapes=[pltpu.VMEM((npair, na * (nb + SUB), LANES), jnp.uint32)],
        compiler_params=_cparams("parallel", "arbitrary"),
        name="fft_long_conv",
    )(view(x), tables["f1"], kf, tables["f2"], tables["g2"], tables["g1"], view(z), view(m), skip.reshape(1, c))
    return out.reshape(npair, n, c)


def _fft_filter_kernel(x_ref, f1_ref, f2_ref, ss_ref, o_ref, w_ref, *, n1):
    t = pl.program_id(1)
    order, na, jb, _ = x_ref.shape
    kblk, nb = o_ref.shape[2], o_ref.shape[3]
    pitch = w_ref.shape[1] // na
    x2 = _flat_rows(x_ref)
    lane = lambda v, q: v[:, q * LANES:(q + 1) * LANES]

    @pl.when(t < n1)
    def _():
        for j in range(jb):
            xs = _lane_cat([x2[pl.ds(o * na * jb + j, na, stride=jb), :] for o in range(order)])
            res = jnp.dot(f1_ref[t * jb + j], xs.astype(BF16), preferred_element_type=F32)
            packed = _pack_c(res[:na], res[na:])
            for o in range(order):
                w_ref[o, pl.ds(t * jb + j, na, stride=pitch), :] = lane(packed, o)

    @pl.when(t >= n1)
    def _():
        scale = _lane_cat([lax.rsqrt(ss_ref[o, 0:1, :] + 1e-12) for o in range(order)])
        for k in range(kblk):
            rows = pl.ds(pl.multiple_of(((t - n1) * kblk + k) * pitch, SUB), nb)
            ar, ai = _unpack_c(_lane_cat([w_ref[o, rows, :] for o in range(order)]))
            x = jnp.dot(f2_ref[...], _stack_bf16(ar, ai), preferred_element_type=F32) * scale
            for o in range(order):
                o_ref[o, 0, k] = lane(x[:nb], o).astype(o_ref.dtype)
                o_ref[o, 1, k] = lane(x[nb:], o).astype(o_ref.dtype)


def fft_filter_spectrum(kt, ss, n, tables, *, jb=16, kblk=32):
    na, nb = _fft_dims(n)
    order, _, c = kt.shape
    jb, kblk = (nb, na) if na * nb <= SHORT_SEQ else (min(jb, nb), min(kblk, na))
    n1, n2 = nb // jb, na // kblk
    return pl.pallas_call(
        functools.partial(_fft_filter_kernel, n1=n1),
        grid=(c // LANES, n1 + n2),
        in_specs=[
            pl.BlockSpec((order, na, jb, LANES), lambda ci, t: (0, 0, jnp.minimum(t, n1 - 1), ci)),
            _resident((nb, 2 * na, na), lambda ci, t: (0, 0, 0)),
            _resident((2 * nb, 2 * nb), lambda ci, t: (0, 0)),
            pl.BlockSpec((order, 8, LANES), lambda ci, t: (0, 0, ci)),
        ],
        out_specs=pl.BlockSpec((order, 2, kblk, nb, LANES), lambda ci, t: (0, 0, jnp.maximum(t - n1, 0), 0, ci)),
        out_shape=jax.ShapeDtypeStruct((order, 2, na, nb, c), BF16),
        scratch_shapes=[pltpu.VMEM((order, na * (nb + SUB), LANES), jnp.uint32)],
        compiler_params=_cparams("parallel", "arbitrary"),
        name="fft_filter_spectrum",
    )(kt.reshape(order, na, nb, c), tables["f1_real"], tables["f2"], ss)


def hyena_mixer_block(h, mods, g, prm):
    bsz, n, d = h.shape
    assert bsz % 2 == 0 and d % LANES == 0, "batch rows travel in pairs, channels in LANES-wide columns"
    tables = _fft_tables(n)
    proj = hyena_in_proj(h, mods, g, prm["w_in"], prm["b_in"], prm["conv_w"], prm["conv_b"])
    kt, ss = hyena_filters_time(n, prm["f_w1"], prm["f_b1"], prm["f_w2"], prm["f_b2"], prm["f_w3"],
                                prm["f_b3"], prm["f_freq"], prm["log_decay"])
    kf = fft_filter_spectrum(kt, ss, n, tables)
    z = fft_long_conv(proj, 2 * d, kf, 0, proj, 2 * d, proj, 0, prm["skip"][0], n, d, tables)
    y = fft_long_conv(z, 0, kf, 1, z, 0, proj, d, prm["skip"][1], n, d, tables)
    return linear_residual(y, h, mods[:, 2:3, :], prm["w_out"], prm["b_out"])


def _mlstm_pre_kernel(hp_ref, h_ref, hn_ref, mod_ref, g_ref, wup_ref, cw_ref, cb_ref, wq_ref, wk_ref, wkt_ref, wv_ref,
                      wg_ref, bg_ref, q_ref, kt_ref, v_ref, xc_ref, z_ref, gt_ref, *, k_scale):
    tm = h_ref.shape[0]
    inner = q_ref.shape[-1]
    grp = wq_ref.shape[-1]
    rows = jnp.concatenate([hp_ref[...], h_ref[...], hn_ref[...]], axis=0)
    u = _rms_mod(rows, g_ref[...], mod_ref[0:1, :], mod_ref[1:2, :])
    up = _bdot(u, wup_ref[...])
    xm_h = _mask_seq_ends(up[:, :inner], tm)
    z_ref[...] = up[HALO:HALO + tm, inner:].astype(z_ref.dtype)
    xc_f = _silu(_conv3(xm_h, cw_ref, tm) + cb_ref[...])
    xc_ref[...] = xc_f.astype(xc_ref.dtype)
    xc_all, xm_all = xc_f.astype(BF16), xm_h[HALO:HALO + tm].astype(BF16)
    gates = jnp.zeros(gt_ref.shape, F32) + bg_ref[...]
    for gi in range(inner // grp):
        lo, hi = gi * grp, (gi + 1) * grp
        xc, xm = xc_all[:, lo:hi], xm_all[:, lo:hi]
        q = jnp.dot(xc, wq_ref[gi], preferred_element_type=F32)
        k = jnp.dot(xc, wk_ref[gi], preferred_element_type=F32)
        v = jnp.dot(xm, wv_ref[gi], preferred_element_type=F32)
        q_ref[:, lo:hi] = q.astype(q_ref.dtype)
        kt_ref[lo:hi, :] = (_nt(wkt_ref[gi], xc) * k_scale).astype(kt_ref.dtype)
        v_ref[:, lo:hi] = v.astype(v_ref.dtype)
        gates += (_bdot(q, wg_ref[lo:hi, :]) + _bdot(k, wg_ref[inner + lo:inner + hi, :])
                  + _bdot(v, wg_ref[2 * inner + lo:2 * inner + hi, :]))
    gt_ref[...] = gates


def mlstm_pre(h, mods, g, prm, *, tm=512):
    bsz, n, d = h.shape
    inner = prm["w_up"].shape[1] // 2
    ngate = prm["w_gate"].shape[1]
    grp = prm["wq_dense"].shape[-1]
    tm = _tile(n, tm)
    const = lambda shape: pl.BlockSpec(shape, lambda b_, i: (0,) * len(shape))
    row_spec = lambda w: pl.BlockSpec((None, tm, w), lambda b_, i: (b_, i, 0))
    mm = jax.ShapeDtypeStruct((bsz, n, inner), BF16)
    return pl.pallas_call(
        functools.partial(_mlstm_pre_kernel, k_scale=float((inner // ML_HEADS) ** -0.5)),
        grid=(bsz, n // tm),
        in_specs=_halo_specs(n, tm, d) + [
            pl.BlockSpec((None, 3, d), lambda b_, i: (b_, 0, 0)),
            const((1, d)), _resident((d, 2 * inner), lambda b_, i: (0, 0)), const((3, inner)), const((1, inner)),
            const((inner // grp, grp, grp)), const((inner // grp, grp, grp)), const((inner // grp, grp, grp)),
            const((inner // grp, grp, grp)), const((3 * inner, ngate)), const((1, ngate)),
        ],
        out_specs=[row_spec(inner), pl.BlockSpec((None, inner, tm), lambda b_, i: (b_, 0, i))]
                  + [row_spec(inner)] * 3 + [row_spec(ngate)],
        out_shape=[mm, jax.ShapeDtypeStruct((bsz, inner, n), BF16), mm, mm, mm,
                   jax.ShapeDtypeStruct((bsz, n, ngate), F32)],
        compiler_params=_cparams("parallel", "parallel"),
        name="mlstm_pre",
    )(h, h, h, mods, g.reshape(1, d), prm["w_up"], prm["conv_w"], prm["conv_b"].reshape(1, inner),
      prm["wq_dense"], prm["wk_dense"], jnp.swapaxes(prm["wk_dense"], 1, 2), prm["wv_dense"], prm["w_gate"],
      prm["b_gate"].reshape(1, ngate))


def _dense_blockdiag(w, grp):
    nb, bs, _ = w.shape
    per = grp // bs
    eye = jnp.eye(per, dtype=w.dtype)
    dense = jnp.einsum("gmde,mn->gmdne", w.reshape(nb // per, per, bs, bs), eye)
    return dense.reshape(nb // per, grp, grp).astype(BF16)


def _split3(x):
    p1 = x.astype(BF16)
    r1 = x - p1.astype(F32)
    p2 = r1.astype(BF16)
    p3 = (r1 - p2.astype(F32)).astype(BF16)
    return p1, p2, p3


def _nt(a, b):
    return lax.dot_general(a, b, (((1,), (1,)), ((), ())), preferred_element_type=F32)


def _scan_kernel(*refs, has_init, emit_state, heads_per_step):
    it = iter(refs)
    q_ref, kt_ref, v_ref, gt_ref = next(it), next(it), next(it), next(it)
    allow_ref, allow_t_ref, neg_ref = next(it), next(it), next(it)
    init_refs = [next(it) for _ in range(3)] if has_init else None
    h_ref = next(it)
    final_refs = [next(it) for _ in range(3)] if emit_state else None
    state = c_sc, n_sc, m_sc = next(it), next(it), next(it)
    c = pl.program_id(3)
    ch = q_ref.shape[0]
    dh = q_ref.shape[1] // heads_per_step

    @pl.when(c == 0)
    def _():
        for sc, src in zip(state, init_refs or [None] * 3):
            sc[...] = jnp.zeros_like(sc) if src is None else src[...]

    allow, allow_t, neg = allow_ref[...], allow_t_ref[...], neg_ref[...]
    ones_rows = jnp.ones((SUB, ch), BF16)

    heads = range(heads_per_step)
    cols = [slice(hh * dh, (hh + 1) * dh) for hh in heads]

    def gate_terms(hh):
        i_row = gt_ref[hh, 0:1, :]
        f_row = gt_ref[hh, 1:2, :]
        lf_row = jnp.minimum(f_row, 0.0) - jnp.log1p(jnp.exp(-jnp.abs(f_row)))
        pieces = jnp.concatenate(list(_split3(lf_row)) + [jnp.zeros((SUB - 3, ch), BF16)], axis=0)
        cum_col = _nt(allow, pieces)
        cum_row = jnp.dot(pieces, allow_t, preferred_element_type=F32)
        b_col = cum_col[:, 0:1] + cum_col[:, 1:2] + cum_col[:, 2:3]
        b_row = cum_row[0:1, :] + cum_row[1:2, :] + cum_row[2:3, :]
        total = jnp.sum(lf_row, axis=1, keepdims=True)
        e_row = i_row - b_row
        m_st = m_sc[hh, 0:1, 0:1]
        dmat = (b_col + e_row) + neg
        a_col = b_col + m_st
        m_j = jnp.maximum(a_col, jnp.max(dmat, axis=1, keepdims=True))
        g_row = total + e_row
        m_new = jnp.maximum(total + m_st, jnp.max(g_row, axis=1, keepdims=True))
        return dict(w_inter=jnp.exp(a_col - m_j), dexp=jnp.exp(dmat - m_j), floor=jnp.exp(-m_j), m_new=m_new,
                    decay=jnp.exp(total + m_st - m_new), w_row=jnp.exp(g_row - m_new))

    def read_out(hh, t):
        q, kt, v = q_ref[:, cols[hh]], kt_ref[cols[hh], :], v_ref[:, cols[hh]]
        s = jnp.dot(q, kt, preferred_element_type=F32) * t["dexp"]
        num = t["w_inter"] * jnp.dot(q, c_sc[hh].astype(BF16), preferred_element_type=F32) + _bdot(s, v)
        qn = (t["w_inter"] * jnp.sum(q.astype(F32) * n_sc[hh, 0:1, :], axis=1, keepdims=True)
              + jnp.sum(s, axis=1, keepdims=True))
        h_ref[:, cols[hh]] = (num / jnp.maximum(jnp.abs(qn), t["floor"])).astype(h_ref.dtype)

    def update(hh, t):
        kwt = (kt_ref[cols[hh], :].astype(F32) * t["w_row"]).astype(BF16)
        c_sc[hh] = t["decay"] * c_sc[hh] + jnp.dot(kwt, v_ref[:, cols[hh]], preferred_element_type=F32)
        n_sc[hh] = t["decay"] * n_sc[hh] + _nt(ones_rows, kwt)
        m_sc[hh] = jnp.broadcast_to(t["m_new"], m_sc.shape[1:])

    terms = [gate_terms(hh) for hh in heads]
    for hh in heads:
        read_out(hh, terms[hh])
        update(hh, terms[hh])

    if emit_state:
        @pl.when(c == pl.num_programs(3) - 1)
        def _():
            for dst, sc in zip(final_refs, state):
                dst[...] = sc[...]


@functools.lru_cache(maxsize=None)
def _scan_masks(ch):
    r = np.arange(ch)
    causal = (r[None, :] <= r[:, None]).astype(np.float32)
    allow = np.stack([causal, causal.T])
    neg = np.where(allow > 0, 0.0, -np.inf).astype(np.float32)
    return allow.astype(BF16), np.transpose(allow, (0, 2, 1)).astype(BF16), neg


def mlstm_scan(q, kt, v, gates_t, init=None, *, emit_state=False):
    bsz, n, inner = q.shape
    nh = gates_t.shape[2]
    dh = inner // nh
    hp = nh if nh <= 4 else (2 if nh % 2 == 0 else 1)
    ch = min(SCAN_CHUNK, n)
    nc = n // ch
    ceff = lambda d, c: c + d * (nc - 1 - 2 * c)
    seq = pl.BlockSpec((None, ch, hp * dh), lambda d, b, hd, c: (b, ceff(d, c), hd))
    state_shapes = [(dh, dh), (SUB, dh), (SUB, 128)]
    st_specs = [pl.BlockSpec((None, None, hp) + s, lambda d, b, hd, c: (d, b, hd, 0, 0)) for s in state_shapes]
    by_dir = pl.BlockSpec((None, ch, ch), lambda d, b, hd, c: (d, 0, 0))
    in_specs = [seq, pl.BlockSpec((None, hp * dh, ch), lambda d, b, hd, c: (b, hd, ceff(d, c))), seq,
                pl.BlockSpec((None, None, hp, 2, ch), lambda d, b, hd, c: (b, d, hd, 0, ceff(d, c))),
                by_dir, by_dir, by_dir]
    args = [q, kt, v, gates_t, *_scan_masks(ch)]
    if init is not None:
        in_specs += st_specs
        args += list(init)
    out_specs = [pl.BlockSpec((None, None, ch, hp * dh), lambda d, b, hd, c: (d, b, ceff(d, c), hd))]
    out_shape = [jax.ShapeDtypeStruct((2, bsz, n, inner), BF16)]
    if emit_state:
        out_specs += st_specs
        out_shape += [jax.ShapeDtypeStruct((2, bsz, nh) + s, F32) for s in state_shapes]
    res = pl.pallas_call(
        functools.partial(_scan_kernel, has_init=init is not None, emit_state=emit_state, heads_per_step=hp),
        grid=(2, bsz, nh // hp, nc),
        in_specs=in_specs,
        out_specs=out_specs,
        out_shape=out_shape,
        scratch_shapes=[pltpu.VMEM((hp,) + s, F32) for s in state_shapes],
        compiler_params=_cparams("parallel", "parallel", "parallel", "arbitrary"),
        name="mlstm_scan",
    )(*args)
    return (res[0], tuple(res[1:])) if emit_state else (res[0], None)


def _mlstm_post_kernel(hf_ref, hb_ref, xc_ref, z_ref, res_ref, gate_ref, ng_ref, sk_ref, wd_ref, o_ref):
    inner = xc_ref.shape[-1]
    dh = inner // ML_HEADS
    acc = jnp.zeros(o_ref.shape, F32)
    for hd in range(ML_HEADS):
        lo, hi = hd * dh, (hd + 1) * dh
        x = hf_ref[:, lo:hi].astype(F32) + hb_ref[:, lo:hi].astype(F32)
        xz = x - jnp.mean(x, axis=-1, keepdims=True)
        var = jnp.mean(xz * xz, axis=-1, keepdims=True)
        hn = xz * lax.rsqrt(var + ML_NORM_EPS) * ng_ref[:, lo:hi]
        t = (hn + sk_ref[:, lo:hi] * xc_ref[:, lo:hi].astype(F32)) * _silu(z_ref[:, lo:hi].astype(F32))
        acc += _bdot(t, wd_ref[lo:hi, :])
    o_ref[...] = res_ref[...] + gate_ref[...] * acc


def mlstm_post(hs, xc, z, h_res, gate, prm, *, tm=512):
    bsz, n, inner = xc.shape
    d = h_res.shape[-1]
    tm = _tile(n, tm)
    const = lambda shape: pl.BlockSpec(shape, lambda b_, i: (0,) * len(shape))
    return pl.pallas_call(
        _mlstm_post_kernel,
        grid=(bsz, n // tm),
        in_specs=[
            pl.BlockSpec((None, None, tm, inner), lambda b_, i: (0, b_, i, 0)),
            pl.BlockSpec((None, None, tm, inner), lambda b_, i: (1, b_, i, 0)),
            pl.BlockSpec((None, tm, inner), lambda b_, i: (b_, i, 0)),
            pl.BlockSpec((None, tm, inner), lambda b_, i: (b_, i, 0)),
            pl.BlockSpec((None, tm, d), lambda b_, i: (b_, i, 0)),
            pl.BlockSpec((None, 1, d), lambda b_, i: (b_, 0, 0)),
            const((1, inner)), const((1, inner)), _resident((inner, d), lambda b_, i: (0, 0)),
        ],
        out_specs=pl.BlockSpec((None, tm, d), lambda b_, i: (b_, i, 0)),
        out_shape=jax.ShapeDtypeStruct((bsz, n, d), F32),
        compiler_params=_cparams("parallel", "parallel"),
        name="mlstm_post",
    )(hs, hs, xc, z, h_res, gate, prm["norm_g"].reshape(1, inner), prm["skip"].reshape(1, inner), prm["w_down"])


def _gates_by_head(gates, nh):
    bsz, n, _ = gates.shape
    return jnp.transpose(gates.reshape(bsz, n, 2, 2, nh), (0, 2, 4, 3, 1))


def mlstm_mixer_block(h_ctx, h_lat, mods_ctx, mods_lat, g, prm, *, with_ctx_out):
    pc = mlstm_pre(h_ctx, mods_ctx, g, prm)
    pq = mlstm_pre(h_lat, mods_lat, g, prm)
    hs_ctx, state = mlstm_scan(pc[0], pc[1], pc[2], _gates_by_head(pc[5], ML_HEADS), emit_state=True)
    hs_lat, _ = mlstm_scan(pq[0], pq[1], pq[2], _gates_by_head(pq[5], ML_HEADS), init=state)
    out_lat = mlstm_post(hs_lat, pq[3], pq[4], h_lat, mods_lat[:, 2:3, :], prm)
    out_ctx = mlstm_post(hs_ctx, pc[3], pc[4], h_ctx, mods_ctx[:, 2:3, :], prm) if with_ctx_out else None
    return out_ctx, out_lat


def _pos_embed_2d(n_tokens, d):
    rows = n_tokens // GRID_W
    quarter = d // 4
    omega = 1.0 / (POS_BASE ** (jnp.arange(quarter, dtype=F32) / quarter))
    ang_r = jnp.arange(rows, dtype=F32)[:, None] * omega
    ang_c = jnp.arange(GRID_W, dtype=F32)[:, None] * omega
    row_emb = jnp.concatenate([jnp.sin(ang_r), jnp.cos(ang_r)], axis=-1)
    col_emb = jnp.concatenate([jnp.sin(ang_c), jnp.cos(ang_c)], axis=-1)
    emb = jnp.concatenate([
        jnp.broadcast_to(row_emb[:, None, :], (rows, GRID_W, d // 2)),
        jnp.broadcast_to(col_emb[None, :, :], (rows, GRID_W, d // 2))], axis=-1)
    return emb.reshape(rows * GRID_W, d)


def kernel(x, c, ctx, c_ctx, ada_w, ada_b, norm_g, final_g, ffn_w_in, ffn_w_out, hy_w_in, hy_b_in, hy_conv_w, hy_conv_b, hy_f_w1, hy_f_b1, hy_f_w2, hy_f_b2, hy_f_w3, hy_f_b3, hy_f_freq, hy_log_decay, hy_skip, hy_w_out, hy_b_out, ml_w_up, ml_conv_w, ml_conv_b, ml_w_q, ml_w_k, ml_w_v, ml_w_gate, ml_b_gate, ml_norm_g, ml_skip, ml_w_down):
    bsz, n_lat, d = x.shape
    n_ctx = ctx.shape[1]
    depth = ada_w.shape[0]
    n_mixers = 2

    cond_rows = 8
    cond = jnp.zeros((cond_rows, d), F32).at[:bsz].set(c).at[bsz].set(c_ctx)
    mods_all = ada_modulation(cond, ada_w, ada_b).reshape(depth, cond_rows, N_MOD, d)

    ffn_w_in = ffn_w_in.astype(BF16)
    ffn_w_out = ffn_w_out.astype(BF16)
    pos = _pos_embed_2d(n_lat, d)

    h_lat = x
    h_ctx = ctx.reshape(1, bsz * n_ctx, d)
    for l in range(depth):
        last = l == depth - 1
        j = l // n_mixers
        m_lat = mods_all[l, :bsz]
        m_ctx = mods_all[l, bsz:bsz + 1]
        m_ctx_b = jnp.broadcast_to(m_ctx, (bsz, N_MOD, d))

        h_lat = ffn_block(h_lat, m_lat[:, 0:3], norm_g[l, 0], ffn_w_in, ffn_w_out, widx=(l, 0),
                          pos=pos if l == 0 else None)
        h_ctx = ffn_block(h_ctx, m_ctx[:, 0:3], norm_g[l, 0], ffn_w_in, ffn_w_out, widx=(l, 0))

        h_ctx = h_ctx.reshape(bsz, n_ctx, d)
        if l % n_mixers == 0:
            prm = dict(w_in=hy_w_in[j].astype(BF16), b_in=hy_b_in[j], conv_w=hy_conv_w[j], conv_b=hy_conv_b[j],
                       f_w1=hy_f_w1[j], f_b1=hy_f_b1[j], f_w2=hy_f_w2[j], f_b2=hy_f_b2[j], f_w3=hy_f_w3[j],
                       f_b3=hy_f_b3[j], f_freq=hy_f_freq[j], log_decay=hy_log_decay[j], skip=hy_skip[j],
                       w_out=hy_w_out[j].astype(BF16), b_out=hy_b_out[j])
            h_lat = hyena_mixer_block(h_lat, m_lat[:, 3:6], norm_g[l, 1], prm)
            if not last:
                h_ctx = hyena_mixer_block(h_ctx, m_ctx_b[:, 3:6], norm_g[l, 1], prm)
        else:
            prm = dict(w_up=ml_w_up[j].astype(BF16), conv_w=ml_conv_w[j], conv_b=ml_conv_b[j],
                       wq_dense=_dense_blockdiag(ml_w_q[j], ML_QKV_GROUP),
                       wk_dense=_dense_blockdiag(ml_w_k[j], ML_QKV_GROUP),
                       wv_dense=_dense_blockdiag(ml_w_v[j], ML_QKV_GROUP),
                       w_gate=ml_w_gate[j].astype(BF16), b_gate=ml_b_gate[j], norm_g=ml_norm_g[j],
                       skip=ml_skip[j], w_down=ml_w_down[j].astype(BF16))
            new_ctx, h_lat = mlstm_mixer_block(h_ctx, h_lat, m_ctx_b[:, 3:6], m_lat[:, 3:6], norm_g[l, 1], prm,
                                               with_ctx_out=not last)
            h_ctx = h_ctx if last else new_ctx
        h_ctx = h_ctx.reshape(1, bsz * n_ctx, d)

        h_lat = ffn_block(h_lat, m_lat[:, 6:9], norm_g[l, 2], ffn_w_in, ffn_w_out, widx=(l, 1),
                          final_g=final_g if last else None)
        if not last:
            h_ctx = ffn_block(h_ctx, m_ctx[:, 6:9], norm_g[l, 2], ffn_w_in, ffn_w_out, widx=(l, 1))
    return h_lat
```

```python
import functools
import math

import jax
import jax.numpy as jnp
import numpy as np
from jax import lax
from jax.experimental import pallas as pl
from jax.experimental.pallas import tpu as pltpu

F32 = jnp.float32
BF16 = jnp.bfloat16

NORM_EPS = 1e-6
GRID_W = 64
POS_BASE = 10000.0
N_MOD = 9
HY_EMB = 33
HY_BANDS = (HY_EMB - 1) // 2
ML_HEADS = 4
ML_NORM_EPS = 1e-5
ML_QKV_GROUP = 256
SCAN_CHUNK = 256
HALO = 8
CONV_COLS = 256

V7X_VMEM_LIMIT = 56 * 1024 * 1024


def _cparams(*sem):
    return pltpu.CompilerParams(dimension_semantics=sem, vmem_limit_bytes=V7X_VMEM_LIMIT)


def _bdot(a, b):
    return jnp.dot(a.astype(BF16), b.astype(BF16), preferred_element_type=F32)


def _rms_mod(h, g, shift, scale):
    y = h * lax.rsqrt(jnp.mean(h * h, axis=-1, keepdims=True) + NORM_EPS)
    return (y * g) * (1.0 + scale) + shift


def _silu(x):
    return x * (1.0 / (1.0 + jnp.exp(-x)))


def _tile(n, pref):
    if n <= pref:
        return n
    for t in range(pref, 7, -1):
        if n % t == 0 and t % 8 == 0:
            return t
    return n


def _mod_kernel(c_ref, w_ref, b_ref, o_ref):
    o_ref[...] = _bdot(_silu(c_ref[...]), w_ref[...]) + b_ref[...]


def ada_modulation(cond, ada_w, ada_b):
    n_layers, d, n_out = ada_w.shape
    r = cond.shape[0]
    tn = _tile(n_out, 2304) if n_out % 128 == 0 else n_out
    return pl.pallas_call(
        _mod_kernel,
        grid=(n_layers, n_out // tn),
        in_specs=[
            pl.BlockSpec((r, d), lambda l, j: (0, 0)),
            pl.BlockSpec((None, d, tn), lambda l, j: (l, 0, j)),
            pl.BlockSpec((None, 1, tn), lambda l, j: (l, 0, j)),
        ],
        out_specs=pl.BlockSpec((None, r, tn), lambda l, j: (l, 0, j)),
        out_shape=jax.ShapeDtypeStruct((n_layers, r, n_out), F32),
        compiler_params=_cparams("parallel", "parallel"),
        name="ada_modulation",
    )(cond, ada_w, ada_b.reshape(n_layers, 1, n_out))


def _ffn_kernel(*refs, has_pos, final):
    it = iter(refs)
    h_ref = next(it)
    pos_ref = next(it) if has_pos else None
    mod_ref, g_ref, wg_ref, wv_ref, wo_ref = next(it), next(it), next(it), next(it), next(it)
    fg_ref = next(it) if final else None
    o_ref = next(it)
    h = h_ref[...]
    if has_pos:
        h = h + pos_ref[...]
    u = _rms_mod(h, g_ref[...], mod_ref[0:1, :], mod_ref[1:2, :]).astype(BF16)
    gate = jnp.dot(u, wg_ref[...], preferred_element_type=F32)
    val = jnp.dot(u, wv_ref[...], preferred_element_type=F32)
    out = h + (0.5 * mod_ref[2:3, :]) * _bdot(_silu(gate) * val, wo_ref[...])
    if final:
        out = out * lax.rsqrt(jnp.mean(out * out, axis=-1, keepdims=True) + NORM_EPS) * fg_ref[...]
    o_ref[...] = out


def _resident(shape, index_map):
    return pl.BlockSpec(shape, index_map, pipeline_mode=pl.Buffered(1))


def ffn_block(h, mods, g, w_in, w_out, *, widx=(), pos=None, final_g=None, tm=1024):
    bsz, n, d = h.shape
    ff = w_out.shape[-2]
    has_pos, final = pos is not None, final_g is not None
    tm = _tile(n, tm // 2 if has_pos else tm)
    lead = (None,) * len(widx)
    in_specs = [pl.BlockSpec((None, tm, d), lambda b, i: (b, i, 0))]
    args = [h]
    if has_pos:
        in_specs.append(pl.BlockSpec((tm, d), lambda b, i: (i, 0)))
        args.append(pos)
    in_specs += [
        pl.BlockSpec((None, 3, d), lambda b, i: (b, 0, 0)),
        pl.BlockSpec((1, d), lambda b, i: (0, 0)),
        _resident(lead + (d, ff), lambda b, i: widx + (0, 0)),
        _resident(lead + (d, ff), lambda b, i: widx + (0, 1)),
        _resident(lead + (ff, d), lambda b, i: widx + (0, 0)),
    ]
    args += [mods, g.reshape(1, d), w_in, w_in, w_out]
    if final:
        in_specs.append(pl.BlockSpec((1, d), lambda b, i: (0, 0)))
        args.append(final_g.reshape(1, d))
    return pl.pallas_call(
        functools.partial(_ffn_kernel, has_pos=has_pos, final=final),
        grid=(bsz, n // tm),
        in_specs=in_specs,
        out_specs=pl.BlockSpec((None, tm, d), lambda b, i: (b, i, 0)),
        out_shape=jax.ShapeDtypeStruct((bsz, n, d), F32),
        compiler_params=_cparams("parallel", "parallel"),
        name="ffn_block",
    )(*args)


def _linres_kernel(y_ref, h_ref, gate_ref, w_ref, b_ref, o_ref):
    for r, y in enumerate(_unpack_c(y_ref[...])):
        o_ref[r] = h_ref[r] + gate_ref[r] * (_bdot(y, w_ref[...]) + b_ref[...])


def linear_residual(y, h, gate, w, b, *, tm=512):
    npair, n, kdim = y.shape
    d = h.shape[-1]
    tm = _tile(n, tm)
    return pl.pallas_call(
        _linres_kernel,
        grid=(npair, n // tm),
        in_specs=[
            pl.BlockSpec((None, tm, kdim), lambda q, i: (q, i, 0)),
            pl.BlockSpec((2, tm, d), lambda q, i: (q, i, 0)),
            pl.BlockSpec((2, 1, d), lambda q, i: (q, 0, 0)),
            _resident((kdim, d), lambda q, i: (0, 0)),
            pl.BlockSpec((1, d), lambda q, i: (0, 0)),
        ],
        out_specs=pl.BlockSpec((2, tm, d), lambda q, i: (q, i, 0)),
        out_shape=jax.ShapeDtypeStruct(h.shape, F32),
        compiler_params=_cparams("parallel", "parallel"),
        name="linear_residual",
    )(y, h, gate, w, b.reshape(1, d))


def _halo_specs(n, tm, d, nbatch=None):
    hb, last = tm // HALO, n // HALO - 1
    return [
        pl.BlockSpec((nbatch, HALO, d), lambda b, i: (b, jnp.maximum(i * hb - 1, 0), 0)),
        pl.BlockSpec((nbatch, tm, d), lambda b, i: (b, i, 0)),
        pl.BlockSpec((nbatch, HALO, d), lambda b, i: (b, jnp.minimum((i + 1) * hb, last), 0)),
    ]


def _mask_seq_ends(p, tm):
    i, ni = pl.program_id(1), pl.num_programs(1)
    head = jnp.where(i == 0, 0.0, p[:HALO])
    tail = jnp.where(i == ni - 1, 0.0, p[tm + HALO:])
    return jnp.concatenate([head, p[HALO:tm + HALO], tail], axis=0)


def _conv3(p, cw_ref, tm):
    rows = p.shape[0]
    prev = pltpu.roll(p, 1, 0)[HALO:HALO + tm]
    nxt = pltpu.roll(p, rows - 1, 0)[HALO:HALO + tm]
    return prev * cw_ref[0:1, :] + p[HALO:HALO + tm] * cw_ref[1:2, :] + nxt * cw_ref[2:3, :]


def _hyena_in_kernel(hp_ref, h_ref, hn_ref, mod_ref, g_ref, w_ref, b_ref, cw_ref, cb_ref, o_ref):
    tm = h_ref.shape[1]
    us = []
    for r in range(2):
        rows = jnp.concatenate([hp_ref[r], h_ref[r], hn_ref[r]], axis=0)
        us.append(_rms_mod(rows, g_ref[...], mod_ref[r, 0:1, :], mod_ref[r, 1:2, :]).astype(BF16))
    nout = o_ref.shape[-1]
    step = CONV_COLS if nout % CONV_COLS == 0 else nout
    for lo in range(0, nout, step):
        cs = slice(lo, lo + step)
        outs = []
        for u in us:
            p = _mask_seq_ends(jnp.dot(u, w_ref[:, cs], preferred_element_type=F32) + b_ref[:, cs], tm)
            outs.append(_conv3(p, cw_ref.at[:, cs], tm) + cb_ref[:, cs])
        o_ref[:, cs] = _pack_c(outs[0], outs[1])


def hyena_in_proj(h, mods, g, w, b, conv_w, conv_b, *, tm=512):
    bsz, n, d = h.shape
    nout = w.shape[1]
    tm = _tile(n, tm)
    const = lambda shape: pl.BlockSpec(shape, lambda b_, i: (0,) * len(shape))
    return pl.pallas_call(
        _hyena_in_kernel,
        grid=(bsz // 2, n // tm),
        in_specs=_halo_specs(n, tm, d, 2) + [
            pl.BlockSpec((2, 3, d), lambda b_, i: (b_, 0, 0)),
            const((1, d)), _resident((d, nout), lambda b_, i: (0, 0)), const((1, nout)), const((3, nout)),
            const((1, nout)),
        ],
        out_specs=pl.BlockSpec((None, tm, nout), lambda b_, i: (b_, i, 0)),
        out_shape=jax.ShapeDtypeStruct((bsz // 2, n, nout), jnp.uint32),
        compiler_params=_cparams("parallel", "parallel"),
        name="hyena_in_proj",
    )(h, h, h, mods, g.reshape(1, d), w, b.reshape(1, nout), conv_w, conv_b.reshape(1, nout))


def _hdot(a, b):
    return jnp.dot(a, b, preferred_element_type=F32, precision=lax.Precision.HIGHEST)


def _dot3(a, b):
    ah, bh = a.astype(BF16), b.astype(BF16)
    al, bl = (a - ah.astype(F32)).astype(BF16), (b - bh.astype(F32)).astype(BF16)
    dot = functools.partial(jnp.dot, preferred_element_type=F32)
    return dot(ah, bh) + (dot(al, bh) + dot(ah, bl))


def _filter_kernel(ft_ref, w1_ref, b1_ref, w2_ref, b2_ref, w3_ref, b3_ref, fr_ref, ld_ref, k_ref, ss_ref):
    half, i = pl.program_id(0), pl.program_id(1)
    tt = ft_ref.shape[0]
    c = k_ref.shape[-1]
    ft = ft_ref[...]
    h = jnp.sin(fr_ref[0:1, :] * (_hdot(ft, w1_ref[...]) + b1_ref[...]))
    h = jnp.sin(fr_ref[1:2, :] * (_hdot(h, w2_ref[...]) + b2_ref[...]))
    no_lag = (lax.broadcasted_iota(jnp.int32, (tt, c), 0) + (1 - half) + i) == 0

    @pl.when((half == 0) & (i == 0))
    def _():
        ss_ref[...] = jnp.zeros_like(ss_ref)

    ks = []
    for o in range(ss_ref.shape[0]):
        k = (_dot3(h, w3_ref[o]) + b3_ref[o]) * jnp.exp(-ft[:, 0:1] * jnp.exp(ld_ref[o]))
        k = jnp.where(no_lag, 0.0, k)
        ks.append(k)
        ss_ref[o] += jnp.broadcast_to(jnp.sum(k * k, axis=0, keepdims=True), ss_ref.shape[1:])
    k_ref[...] = _pack_c(*ks)


def hyena_filters_time(n, f_w1, f_b1, f_w2, f_b2, f_w3, f_b3, f_freq, log_decay, *, tt=512):
    order, _, d = log_decay.shape
    assert order == 2, "the packed filter word holds exactly two orders"
    fh = f_w2.shape[0]
    emb_pad = 64
    lag = np.arange(2 * n)
    lag = np.where(lag < n, lag, 2 * n - lag).astype(np.float32)
    t_norm = lag / np.float32(max(n - 1, 1))
    bands = np.linspace(1e-4, HY_BANDS - 1, HY_BANDS, dtype=np.float32)
    ang = (np.float32(2.0 * math.pi / n) * lag)[:, None] * bands[None, :]
    feats = np.zeros((2 * n, emb_pad), np.float32)
    feats[:, :HY_EMB] = np.concatenate([t_norm[:, None], np.cos(ang), -np.sin(ang)], axis=-1)
    w1 = jnp.zeros((emb_pad, fh), F32).at[:HY_EMB].set(f_w1)
    tt = _tile(n, tt)
    nt = n // tt
    const = lambda shape: pl.BlockSpec(shape, lambda hf, i: (0,) * len(shape))
    return pl.pallas_call(
        _filter_kernel,
        grid=(2, nt),
        in_specs=[
            pl.BlockSpec((tt, emb_pad), lambda hf, i: (hf * nt + i, 0)),
            const((emb_pad, fh)), const((1, fh)), const((fh, fh)), const((1, fh)),
            pl.BlockSpec((None, order, fh, d), lambda hf, i: (hf, 0, 0, 0)),
            pl.BlockSpec((order, None, 1, d), lambda hf, i: (0, hf, 0, 0)),
            const((2, fh)),
            pl.BlockSpec((order, None, 1, d), lambda hf, i: (0, hf, 0, 0)),
        ],
        out_specs=[
            pl.BlockSpec((tt, d), lambda hf, i: (hf * nt + i, 0)),
            pl.BlockSpec((order, 8, d), lambda hf, i: (0, 0, 0)),
        ],
        out_shape=[jax.ShapeDtypeStruct((2 * n, d), jnp.uint32), jax.ShapeDtypeStruct((order, 8, d), F32)],
        compiler_params=_cparams("arbitrary", "arbitrary"),
        name="hyena_filter_mlp",
    )(jnp.asarray(feats), w1, f_b1.reshape(1, fh), f_w2, f_b2.reshape(1, fh),
      jnp.transpose(f_w3.reshape(fh, order, 2, d), (2, 1, 0, 3)),
      f_b3.reshape(order, 2, 1, d), f_freq, log_decay.reshape(order, 2, 1, d))


def _fft_dims(n):
    nb = 1 << int(math.floor(math.log2(math.sqrt(2 * n))))
    na = 2 * n // nb
    assert na * nb == 2 * n and na % 16 == 0 and nb % 8 == 0, (n, na, nb)
    return na, nb


@functools.lru_cache(maxsize=None)
def _fft_tables(n):
    na, nb = _fft_dims(n)
    na2, nn = na // 2, 2 * n
    ka = np.arange(na, dtype=np.int64)[None, :, None]
    a = np.arange(na2, dtype=np.int64)[None, None, :]
    b = np.arange(nb, dtype=np.int64)[:, None, None]
    ang = (2.0 * np.pi / nn) * ((ka * (a * nb + b)) % nn)
    mr, mi = np.cos(ang), -np.sin(ang)
    f1 = np.concatenate([np.concatenate([mr, -mi], 2), np.concatenate([mi, mr], 2)], 1)
    mrt, mit = np.swapaxes(mr, 1, 2) / nn, np.swapaxes(mi, 1, 2) / nn
    g1 = np.concatenate([np.concatenate([mrt, mit], 2), np.concatenate([-mit, mrt], 2)], 1)
    kb = np.arange(nb, dtype=np.int64)
    ang2 = (2.0 * np.pi / nb) * ((kb[:, None] * kb[None, :]) % nb)
    er, ei = np.cos(ang2), -np.sin(ang2)
    f2 = np.block([[er, -ei], [ei, er]])
    g2 = np.block([[er, ei], [-ei, er]])
    a_all = np.arange(na, dtype=np.int64)[None, None, :]
    ang_f = (2.0 * np.pi / nn) * ((ka * (a_all * nb + b)) % nn)
    f1_real = np.concatenate([np.cos(ang_f), -np.sin(ang_f)], 1)
    as_bf16 = lambda x: x.astype(np.float32).astype(BF16)
    return dict(f1=as_bf16(f1), f1_real=as_bf16(f1_real), g1=as_bf16(g1), f2=as_bf16(f2), g2=as_bf16(g2))


def _pack_c(re, im):
    rb = lax.bitcast_convert_type(re.astype(BF16).astype(F32), jnp.uint32)
    ib = lax.bitcast_convert_type(im.astype(BF16).astype(F32), jnp.uint32)
    return rb | lax.shift_right_logical(ib, jnp.uint32(16))


def _unpack_c(w):
    re = lax.bitcast_convert_type(w & jnp.uint32(0xFFFF0000), F32)
    im = lax.bitcast_convert_type(lax.shift_left(w, jnp.uint32(16)), F32)
    return re, im


def _stack_bf16(re, im):
    return jnp.concatenate([re, im], axis=0).astype(BF16)


LANES = 128
SUB = 8
SHORT_SEQ = 1024


def _lane_cat(parts):
    return parts[0] if len(parts) == 1 else jnp.concatenate(parts, axis=1)


def _flat_rows(ref):
    return ref.reshape(math.prod(ref.shape[:-1]), ref.shape[-1])


def _fft_conv_kernel(x_ref, f1_ref, k_ref, f2_ref, g2_ref, g1_ref, z_ref, m_ref, skip_ref, o_ref, w_ref, *, n1, n2):
    t = pl.program_id(1)
    npair, na2, jb, _ = x_ref.shape
    na, kblk, nb = 2 * na2, k_ref.shape[1], k_ref.shape[2]
    pitch = w_ref.shape[1] // na
    x2, z2, m2, o2 = _flat_rows(x_ref), _flat_rows(z_ref), _flat_rows(m_ref), _flat_rows(o_ref)
    col = lambda q, j: pl.ds(q * na2 * jb + j, na2, stride=jb)
    lane = lambda v, q: v[:, q * LANES:(q + 1) * LANES]

    @pl.when(t < n1)
    def _():
        for j in range(jb):
            xs = _lane_cat([_stack_bf16(*_unpack_c(x2[col(q, j), :])) for q in range(npair)])
            res = jnp.dot(f1_ref[t * jb + j], xs, preferred_element_type=F32)
            packed = _pack_c(res[:na], res[na:])
            for q in range(npair):
                w_ref[q, pl.ds(t * jb + j, na, stride=pitch), :] = lane(packed, q)

    @pl.when((t >= n1) & (t < n1 + n2))
    def _():
        for k in range(kblk):
            rows = pl.ds(pl.multiple_of(((t - n1) * kblk + k) * pitch, SUB), nb)
            ar, ai = _unpack_c(_lane_cat([w_ref[q, rows, :] for q in range(npair)]))
            x = jnp.dot(f2_ref[...], _stack_bf16(ar, ai), preferred_element_type=F32)
            xr, xi = x[:nb], x[nb:]
            kr = _lane_cat([k_ref[0, k].astype(F32)] * npair)
            ki = _lane_cat([k_ref[1, k].astype(F32)] * npair)
            bv = jnp.dot(g2_ref[...], _stack_bf16(xr * kr - xi * ki, xr * ki + xi * kr), preferred_element_type=F32)
            packed = _pack_c(bv[:nb], bv[nb:])
            for q in range(npair):
                w_ref[q, rows, :] = lane(packed, q)

    @pl.when(t >= n1 + n2)
    def _():
        bb = t - (n1 + n2)
        skip = skip_ref[...]
        for j in range(jb):
            br, bi = _unpack_c(_lane_cat([w_ref[q, pl.ds(bb * jb + j, na, stride=pitch), :] for q in range(npair)]))
            y = jnp.dot(g1_ref[bb * jb + j], _stack_bf16(br, bi), preferred_element_type=F32)
            for q in range(npair):
                rows = col(q, j)
                (zr, zi), (mr, mi) = _unpack_c(z2[rows, :]), _unpack_c(m2[rows, :])
                o2[rows, :] = _pack_c(mr * (lane(y[:na2], q) + skip * zr), mi * (lane(y[na2:], q) + skip * zi))


def fft_long_conv(x, x_off, kf, order, z, z_off, m, m_off, skip, n, c, tables, *, jb=16, kblk=32):
    na, nb = _fft_dims(n)
    na2 = na // 2
    jb, kblk = (nb, na) if na * nb <= SHORT_SEQ else (min(jb, nb), min(kblk, na))
    n1, n2 = nb // jb, na // kblk
    npair = x.shape[0]
    view = lambda a: a.reshape(a.shape[0], na2, nb, a.shape[-1])
    xo, zo, mo = x_off // LANES, z_off // LANES, m_off // LANES
    col1 = lambda t: jnp.minimum(t, n1 - 1)
    col3 = lambda t: jnp.clip(t - (n1 + n2), 0, n1 - 1)
    seq = lambda off, col: pl.BlockSpec((npair, na2, jb, LANES), lambda ci, t: (0, 0, col(t), ci + off))
    const = lambda shape: _resident(shape, lambda ci, t: (0,) * len(shape))
    out = pl.pallas_call(
        functools.partial(_fft_conv_kernel, n1=n1, n2=n2),
        grid=(c // LANES, n1 + n2 + n1),
        in_specs=[
            seq(xo, col1),
            const((nb, 2 * na, na)),
            pl.BlockSpec((None, 2, kblk, nb, LANES), lambda ci, t: (order, 0, jnp.clip(t - n1, 0, n2 - 1), 0, ci)),
            const((2 * nb, 2 * nb)), const((2 * nb, 2 * nb)),
            const((nb, na, 2 * na)),
            seq(zo, col3), seq(mo, col3),
            pl.BlockSpec((1, LANES), lambda ci, t: (0, ci)),
        ],
        out_specs=seq(0, col3),
        out_shape=jax.ShapeDtypeStruct((npair, na2, nb, c), jnp.uint32),
        scratch_shapes=[pltpu.VMEM((npair, na * (nb + SUB), LANES), jnp.uint32)],
        compiler_params=_cparams("parallel", "arbitrary"),
        name="fft_long_conv",
    )(view(x), tables["f1"], kf, tables["f2"], tables["g2"], tables["g1"], view(z), view(m), skip.reshape(1, c))
    return out.reshape(npair, n, c)


def _fft_filter_kernel(x_ref, f1_ref, f2_ref, ss_ref, o_ref, w_ref, *, n1):
    t = pl.program_id(1)
    _, na, jb, _ = x_ref.shape
    order, kblk, nb = o_ref.shape[0], o_ref.shape[2], o_ref.shape[3]
    pitch = w_ref.shape[1] // na
    x2 = _flat_rows(x_ref)
    lane = lambda v, q: v[:, q * LANES:(q + 1) * LANES]

    @pl.when(t < n1)
    def _():
        for j in range(jb):
            xs = _lane_cat(list(_unpack_c(x2[pl.ds(j, na, stride=jb), :])))
            res = jnp.dot(f1_ref[t * jb + j], xs.astype(BF16), preferred_element_type=F32)
            packed = _pack_c(res[:na], res[na:])
            for o in range(order):
                w_ref[o, pl.ds(t * jb + j, na, stride=pitch), :] = lane(packed, o)

    @pl.when(t >= n1)
    def _():
        scale = _lane_cat([lax.rsqrt(ss_ref[o, 0:1, :] + 1e-12) for o in range(order)])
        for k in range(kblk):
            rows = pl.ds(pl.multiple_of(((t - n1) * kblk + k) * pitch, SUB), nb)
            ar, ai = _unpack_c(_lane_cat([w_ref[o, rows, :] for o in range(order)]))
            x = jnp.dot(f2_ref[...], _stack_bf16(ar, ai), preferred_element_type=F32) * scale
            for o in range(order):
                o_ref[o, 0, k] = lane(x[:nb], o).astype(o_ref.dtype)
                o_ref[o, 1, k] = lane(x[nb:], o).astype(o_ref.dtype)


def fft_filter_spectrum(kt, ss, n, tables, *, jb=16, kblk=32):
    na, nb = _fft_dims(n)
    order, c = ss.shape[0], kt.shape[-1]
    jb, kblk = (nb, na) if na * nb <= SHORT_SEQ else (min(jb, nb), min(kblk, na))
    n1, n2 = nb // jb, na // kblk
    return pl.pallas_call(
        functools.partial(_fft_filter_kernel, n1=n1),
        grid=(c // LANES, n1 + n2),
        in_specs=[
            pl.BlockSpec((1, na, jb, LANES), lambda ci, t: (0, 0, jnp.minimum(t, n1 - 1), ci)),
            _resident((nb, 2 * na, na), lambda ci, t: (0, 0, 0)),
            _resident((2 * nb, 2 * nb), lambda ci, t: (0, 0)),
            pl.BlockSpec((order, 8, LANES), lambda ci, t: (0, 0, ci)),
        ],
        out_specs=pl.BlockSpec((order, 2, kblk, nb, LANES), lambda ci, t: (0, 0, jnp.maximum(t - n1, 0), 0, ci)),
        out_shape=jax.ShapeDtypeStruct((order, 2, na, nb, c), BF16),
        scratch_shapes=[pltpu.VMEM((order, na * (nb + SUB), LANES), jnp.uint32)],
        compiler_params=_cparams("parallel", "arbitrary"),
        name="fft_filter_spectrum",
    )(kt.reshape(1, na, nb, c), tables["f1_real"], tables["f2"], ss)


def hyena_mixer_block(h, mods, g, prm):
    bsz, n, d = h.shape
    assert bsz % 2 == 0 and d % LANES == 0, "batch rows travel in pairs, channels in LANES-wide columns"
    tables = _fft_tables(n)
    proj = hyena_in_proj(h, mods, g, prm["w_in"], prm["b_in"], prm["conv_w"], prm["conv_b"])
    kt, ss = hyena_filters_time(n, prm["f_w1"], prm["f_b1"], prm["f_w2"], prm["f_b2"], prm["f_w3"],
                                prm["f_b3"], prm["f_freq"], prm["log_decay"])
    kf = fft_filter_spectrum(kt, ss, n, tables)
    z = fft_long_conv(proj, 2 * d, kf, 0, proj, 2 * d, proj, 0, prm["skip"][0], n, d, tables)
    y = fft_long_conv(z, 0, kf, 1, z, 0, proj, d, prm["skip"][1], n, d, tables)
    return linear_residual(y, h, mods[:, 2:3, :], prm["w_out"], prm["b_out"])


def _mlstm_pre_kernel(hp_ref, h_ref, hn_ref, mod_ref, g_ref, wup_ref, cw_ref, cb_ref, wq_ref, wk_ref, wkt_ref, wv_ref,
                      wg_ref, bg_ref, q_ref, kt_ref, v_ref, xc_ref, z_ref, gt_ref, *, k_scale):
    tm = h_ref.shape[0]
    inner = q_ref.shape[-1]
    grp = wq_ref.shape[-1]
    rows = jnp.concatenate([hp_ref[...], h_ref[...], hn_ref[...]], axis=0)
    u = _rms_mod(rows, g_ref[...], mod_ref[0:1, :], mod_ref[1:2, :])
    up = _bdot(u, wup_ref[...])
    xm_h = _mask_seq_ends(up[:, :inner], tm)
    z_ref[...] = up[HALO:HALO + tm, inner:].astype(z_ref.dtype)
    xc_f = _silu(_conv3(xm_h, cw_ref, tm) + cb_ref[...])
    xc_ref[...] = xc_f.astype(xc_ref.dtype)
    xc_all, xm_all = xc_f.astype(BF16), xm_h[HALO:HALO + tm].astype(BF16)
    gates = jnp.zeros(gt_ref.shape, F32) + bg_ref[...]
    for gi in range(inner // grp):
        lo, hi = gi * grp, (gi + 1) * grp
        xc, xm = xc_all[:, lo:hi], xm_all[:, lo:hi]
        q = jnp.dot(xc, wq_ref[gi], preferred_element_type=F32)
        k = jnp.dot(xc, wk_ref[gi], preferred_element_type=F32)
        v = jnp.dot(xm, wv_ref[gi], preferred_element_type=F32)
        q_ref[:, lo:hi] = q.astype(q_ref.dtype)
        kt_ref[lo:hi, :] = (_nt(wkt_ref[gi], xc) * k_scale).astype(kt_ref.dtype)
        v_ref[:, lo:hi] = v.astype(v_ref.dtype)
        gates += (_bdot(q, wg_ref[lo:hi, :]) + _bdot(k, wg_ref[inner + lo:inner + hi, :])
                  + _bdot(v, wg_ref[2 * inner + lo:2 * inner + hi, :]))
    gt_ref[...] = gates


def mlstm_pre(h, mods, g, prm, *, tm=512):
    bsz, n, d = h.shape
    inner = prm["w_up"].shape[1] // 2
    ngate = prm["w_gate"].shape[1]
    grp = prm["wq_dense"].shape[-1]
    tm = _tile(n, tm)
    const = lambda shape: pl.BlockSpec(shape, lambda b_, i: (0,) * len(shape))
    row_spec = lambda w: pl.BlockSpec((None, tm, w), lambda b_, i: (b_, i, 0))
    mm = jax.ShapeDtypeStruct((bsz, n, inner), BF16)
    return pl.pallas_call(
        functools.partial(_mlstm_pre_kernel, k_scale=float((inner // ML_HEADS) ** -0.5)),
        grid=(bsz, n // tm),
        in_specs=_halo_specs(n, tm, d) + [
            pl.BlockSpec((None, 3, d), lambda b_, i: (b_, 0, 0)),
            const((1, d)), _resident((d, 2 * inner), lambda b_, i: (0, 0)), const((3, inner)), const((1, inner)),
            const((inner // grp, grp, grp)), const((inner // grp, grp, grp)), const((inner // grp, grp, grp)),
            const((inner // grp, grp, grp)), const((3 * inner, ngate)), const((1, ngate)),
        ],
        out_specs=[row_spec(inner), pl.BlockSpec((None, inner, tm), lambda b_, i: (b_, 0, i))]
                  + [row_spec(inner)] * 3 + [row_spec(ngate)],
        out_shape=[mm, jax.ShapeDtypeStruct((bsz, inner, n), BF16), mm, mm, mm,
                   jax.ShapeDtypeStruct((bsz, n, ngate), F32)],
        compiler_params=_cparams("parallel", "parallel"),
        name="mlstm_pre",
    )(h, h, h, mods, g.reshape(1, d), prm["w_up"], prm["conv_w"], prm["conv_b"].reshape(1, inner),
      prm["wq_dense"], prm["wk_dense"], jnp.swapaxes(prm["wk_dense"], 1, 2), prm["wv_dense"], prm["w_gate"],
      prm["b_gate"].reshape(1, ngate))


def _dense_blockdiag(w, grp):
    nb, bs, _ = w.shape
    per = grp // bs
    eye = jnp.eye(per, dtype=w.dtype)
    dense = jnp.einsum("gmde,mn->gmdne", w.reshape(nb // per, per, bs, bs), eye)
    return dense.reshape(nb // per, grp, grp).astype(BF16)


def _split3(x):
    p1 = x.astype(BF16)
    r1 = x - p1.astype(F32)
    p2 = r1.astype(BF16)
    p3 = (r1 - p2.astype(F32)).astype(BF16)
    return p1, p2, p3


def _nt(a, b):
    return lax.dot_general(a, b, (((1,), (1,)), ((), ())), preferred_element_type=F32)


def _scan_kernel(*refs, has_init, emit_state, heads_per_step):
    it = iter(refs)
    q_ref, kt_ref, v_ref, gt_ref = next(it), next(it), next(it), next(it)
    allow_ref, allow_t_ref, neg_ref = next(it), next(it), next(it)
    init_refs = [next(it) for _ in range(3)] if has_init else None
    h_ref = next(it)
    final_refs = [next(it) for _ in range(3)] if emit_state else None
    state = c_sc, n_sc, m_sc = next(it), next(it), next(it)
    c = pl.program_id(3)
    ch = q_ref.shape[0]
    dh = q_ref.shape[1] // heads_per_step

    @pl.when(c == 0)
    def _():
        for sc, src in zip(state, init_refs or [None] * 3):
            sc[...] = jnp.zeros_like(sc) if src is None else src[...]

    allow, allow_t, neg = allow_ref[...], allow_t_ref[...], neg_ref[...]
    ones_rows = jnp.ones((SUB, ch), BF16)

    heads = range(heads_per_step)
    cols = [slice(hh * dh, (hh + 1) * dh) for hh in heads]

    def gate_terms(hh):
        i_row = gt_ref[hh, 0:1, :]
        f_row = gt_ref[hh, 1:2, :]
        lf_row = jnp.minimum(f_row, 0.0) - jnp.log1p(jnp.exp(-jnp.abs(f_row)))
        pieces = jnp.concatenate(list(_split3(lf_row)) + [jnp.zeros((SUB - 3, ch), BF16)], axis=0)
        cum_col = _nt(allow, pieces)
        cum_row = jnp.dot(pieces, allow_t, preferred_element_type=F32)
        b_col = cum_col[:, 0:1] + cum_col[:, 1:2] + cum_col[:, 2:3]
        b_row = cum_row[0:1, :] + cum_row[1:2, :] + cum_row[2:3, :]
        total = jnp.sum(lf_row, axis=1, keepdims=True)
        e_row = i_row - b_row
        m_st = m_sc[hh, 0:1, 0:1]
        dmat = (b_col + e_row) + neg
        a_col = b_col + m_st
        m_j = jnp.maximum(a_col, jnp.max(dmat, axis=1, keepdims=True))
        g_row = total + e_row
        m_new = jnp.maximum(total + m_st, jnp.max(g_row, axis=1, keepdims=True))
        return dict(w_inter=jnp.exp(a_col - m_j), dexp=jnp.exp(dmat - m_j), floor=jnp.exp(-m_j), m_new=m_new,
                    decay=jnp.exp(total + m_st - m_new), w_row=jnp.exp(g_row - m_new))

    def read_out(hh, t):
        q, kt, v = q_ref[:, cols[hh]], kt_ref[cols[hh], :], v_ref[:, cols[hh]]
        s = jnp.dot(q, kt, preferred_element_type=F32) * t["dexp"]
        num = t["w_inter"] * jnp.dot(q, c_sc[hh].astype(BF16), preferred_element_type=F32) + _bdot(s, v)
        qn = (t["w_inter"] * jnp.sum(q.astype(F32) * n_sc[hh, 0:1, :], axis=1, keepdims=True)
              + jnp.sum(s, axis=1, keepdims=True))
        h_ref[:, cols[hh]] = (num / jnp.maximum(jnp.abs(qn), t["floor"])).astype(h_ref.dtype)

    def update(hh, t):
        kwt = (kt_ref[cols[hh], :].astype(F32) * t["w_row"]).astype(BF16)
        c_sc[hh] = t["decay"] * c_sc[hh] + jnp.dot(kwt, v_ref[:, cols[hh]], preferred_element_type=F32)
        n_sc[hh] = t["decay"] * n_sc[hh] + _nt(ones_rows, kwt)
        m_sc[hh] = jnp.broadcast_to(t["m_new"], m_sc.shape[1:])

    terms = [gate_terms(hh) for hh in heads]
    for hh in heads:
        read_out(hh, terms[hh])
        update(hh, terms[hh])

    if emit_state:
        @pl.when(c == pl.num_programs(3) - 1)
        def _():
            for dst, sc in zip(final_refs, state):
                dst[...] = sc[...]


@functools.lru_cache(maxsize=None)
def _scan_masks(ch):
    r = np.arange(ch)
    causal = (r[None, :] <= r[:, None]).astype(np.float32)
    allow = np.stack([causal, causal.T])
    neg = np.where(allow > 0, 0.0, -np.inf).astype(np.float32)
    return allow.astype(BF16), np.transpose(allow, (0, 2, 1)).astype(BF16), neg


def mlstm_scan(q, kt, v, gates_t, init=None, *, emit_state=False):
    bsz, n, inner = q.shape
    nh = gates_t.shape[2]
    dh = inner // nh
    hp = nh if nh <= 4 else (2 if nh % 2 == 0 else 1)
    ch = min(SCAN_CHUNK, n)
    nc = n // ch
    ceff = lambda d, c: c + d * (nc - 1 - 2 * c)
    seq = pl.BlockSpec((None, ch, hp * dh), lambda d, b, hd, c: (b, ceff(d, c), hd))
    state_shapes = [(dh, dh), (SUB, dh), (SUB, 128)]
    st_specs = [pl.BlockSpec((None, None, hp) + s, lambda d, b, hd, c: (d, b, hd, 0, 0)) for s in state_shapes]
    by_dir = pl.BlockSpec((None, ch, ch), lambda d, b, hd, c: (d, 0, 0))
    in_specs = [seq, pl.BlockSpec((None, hp * dh, ch), lambda d, b, hd, c: (b, hd, ceff(d, c))), seq,
                pl.BlockSpec((None, None, hp, 2, ch), lambda d, b, hd, c: (b, d, hd, 0, ceff(d, c))),
                by_dir, by_dir, by_dir]
    args = [q, kt, v, gates_t, *_scan_masks(ch)]
    if init is not None:
        in_specs += st_specs
        args += list(init)
    out_specs = [pl.BlockSpec((None, None, ch, hp * dh), lambda d, b, hd, c: (d, b, ceff(d, c), hd))]
    out_shape = [jax.ShapeDtypeStruct((2, bsz, n, inner), BF16)]
    if emit_state:
        out_specs += st_specs
        out_shape += [jax.ShapeDtypeStruct((2, bsz, nh) + s, F32) for s in state_shapes]
    res = pl.pallas_call(
        functools.partial(_scan_kernel, has_init=init is not None, emit_state=emit_state, heads_per_step=hp),
        grid=(2, bsz, nh // hp, nc),
        in_specs=in_specs,
        out_specs=out_specs,
        out_shape=out_shape,
        scratch_shapes=[pltpu.VMEM((hp,) + s, F32) for s in state_shapes],
        compiler_params=_cparams("parallel", "parallel", "parallel", "arbitrary"),
        name="mlstm_scan",
    )(*args)
    return (res[0], tuple(res[1:])) if emit_state else (res[0], None)


def _mlstm_post_kernel(hf_ref, hb_ref, xc_ref, z_ref, res_ref, gate_ref, ng_ref, sk_ref, wd_ref, o_ref):
    inner = xc_ref.shape[-1]
    dh = inner // ML_HEADS
    acc = jnp.zeros(o_ref.shape, F32)
    for hd in range(ML_HEADS):
        lo, hi = hd * dh, (hd + 1) * dh
        x = hf_ref[:, lo:hi].astype(F32) + hb_ref[:, lo:hi].astype(F32)
        xz = x - jnp.mean(x, axis=-1, keepdims=True)
        var = jnp.mean(xz * xz, axis=-1, keepdims=True)
        hn = xz * lax.rsqrt(var + ML_NORM_EPS) * ng_ref[:, lo:hi]
        t = (hn + sk_ref[:, lo:hi] * xc_ref[:, lo:hi].astype(F32)) * _silu(z_ref[:, lo:hi].astype(F32))
        acc += _bdot(t, wd_ref[lo:hi, :])
    o_ref[...] = res_ref[...] + gate_ref[...] * acc


def mlstm_post(hs, xc, z, h_res, gate, prm, *, tm=512):
    bsz, n, inner = xc.shape
    d = h_res.shape[-1]
    tm = _tile(n, tm)
    const = lambda shape: pl.BlockSpec(shape, lambda b_, i: (0,) * len(shape))
    return pl.pallas_call(
        _mlstm_post_kernel,
        grid=(bsz, n // tm),
        in_specs=[
            pl.BlockSpec((None, None, tm, inner), lambda b_, i: (0, b_, i, 0)),
            pl.BlockSpec((None, None, tm, inner), lambda b_, i: (1, b_, i, 0)),
            pl.BlockSpec((None, tm, inner), lambda b_, i: (b_, i, 0)),
            pl.BlockSpec((None, tm, inner), lambda b_, i: (b_, i, 0)),
            pl.BlockSpec((None, tm, d), lambda b_, i: (b_, i, 0)),
            pl.BlockSpec((None, 1, d), lambda b_, i: (b_, 0, 0)),
            const((1, inner)), const((1, inner)), _resident((inner, d), lambda b_, i: (0, 0)),
        ],
        out_specs=pl.BlockSpec((None, tm, d), lambda b_, i: (b_, i, 0)),
        out_shape=jax.ShapeDtypeStruct((bsz, n, d), F32),
        compiler_params=_cparams("parallel", "parallel"),
        name="mlstm_post",
    )(hs, hs, xc, z, h_res, gate, prm["norm_g"].reshape(1, inner), prm["skip"].reshape(1, inner), prm["w_down"])


def _gates_by_head(gates, nh):
    bsz, n, _ = gates.shape
    return jnp.transpose(gates.reshape(bsz, n, 2, 2, nh), (0, 2, 4, 3, 1))


def mlstm_mixer_block(h_ctx, h_lat, mods_ctx, mods_lat, g, prm, *, with_ctx_out):
    pc = mlstm_pre(h_ctx, mods_ctx, g, prm)
    pq = mlstm_pre(h_lat, mods_lat, g, prm)
    hs_ctx, state = mlstm_scan(pc[0], pc[1], pc[2], _gates_by_head(pc[5], ML_HEADS), emit_state=True)
    hs_lat, _ = mlstm_scan(pq[0], pq[1], pq[2], _gates_by_head(pq[5], ML_HEADS), init=state)
    out_lat = mlstm_post(hs_lat, pq[3], pq[4], h_lat, mods_lat[:, 2:3, :], prm)
    out_ctx = mlstm_post(hs_ctx, pc[3], pc[4], h_ctx, mods_ctx[:, 2:3, :], prm) if with_ctx_out else None
    return out_ctx, out_lat


def _pos_embed_2d(n_tokens, d):
    rows = n_tokens // GRID_W
    quarter = d // 4
    omega = 1.0 / (POS_BASE ** (jnp.arange(quarter, dtype=F32) / quarter))
    ang_r = jnp.arange(rows, dtype=F32)[:, None] * omega
    ang_c = jnp.arange(GRID_W, dtype=F32)[:, None] * omega
    row_emb = jnp.concatenate([jnp.sin(ang_r), jnp.cos(ang_r)], axis=-1)
    col_emb = jnp.concatenate([jnp.sin(ang_c), jnp.cos(ang_c)], axis=-1)
    emb = jnp.concatenate([
        jnp.broadcast_to(row_emb[:, None, :], (rows, GRID_W, d // 2)),
        jnp.broadcast_to(col_emb[None, :, :], (rows, GRID_W, d // 2))], axis=-1)
    return emb.reshape(rows * GRID_W, d)


def kernel(x, c, ctx, c_ctx, ada_w, ada_b, norm_g, final_g, ffn_w_in, ffn_w_out, hy_w_in, hy_b_in, hy_conv_w, hy_conv_b, hy_f_w1, hy_f_b1, hy_f_w2, hy_f_b2, hy_f_w3, hy_f_b3, hy_f_freq, hy_log_decay, hy_skip, hy_w_out, hy_b_out, ml_w_up, ml_conv_w, ml_conv_b, ml_w_q, ml_w_k, ml_w_v, ml_w_gate, ml_b_gate, ml_norm_g, ml_skip, ml_w_down):
    bsz, n_lat, d = x.shape
    n_ctx = ctx.shape[1]
    depth = ada_w.shape[0]
    n_mixers = 2

    cond_rows = 8
    cond = jnp.zeros((cond_rows, d), F32).at[:bsz].set(c).at[bsz].set(c_ctx)
    mods_all = ada_modulation(cond, ada_w, ada_b).reshape(depth, cond_rows, N_MOD, d)

    ffn_w_in = ffn_w_in.astype(BF16)
    ffn_w_out = ffn_w_out.astype(BF16)
    pos = _pos_embed_2d(n_lat, d)

    h_lat = x
    h_ctx = ctx.reshape(1, bsz * n_ctx, d)
    for l in range(depth):
        last = l == depth - 1
        j = l // n_mixers
        m_lat = mods_all[l, :bsz]
        m_ctx = mods_all[l, bsz:bsz + 1]
        m_ctx_b = jnp.broadcast_to(m_ctx, (bsz, N_MOD, d))

        h_lat = ffn_block(h_lat, m_lat[:, 0:3], norm_g[l, 0], ffn_w_in, ffn_w_out, widx=(l, 0),
                          pos=pos if l == 0 else None)
        h_ctx = ffn_block(h_ctx, m_ctx[:, 0:3], norm_g[l, 0], ffn_w_in, ffn_w_out, widx=(l, 0))

        h_ctx = h_ctx.reshape(bsz, n_ctx, d)
        if l % n_mixers == 0:
            prm = dict(w_in=hy_w_in[j].astype(BF16), b_in=hy_b_in[j], conv_w=hy_conv_w[j], conv_b=hy_conv_b[j],
                       f_w1=hy_f_w1[j], f_b1=hy_f_b1[j], f_w2=hy_f_w2[j], f_b2=hy_f_b2[j], f_w3=hy_f_w3[j],
                       f_b3=hy_f_b3[j], f_freq=hy_f_freq[j], log_decay=hy_log_decay[j], skip=hy_skip[j],
                       w_out=hy_w_out[j].astype(BF16), b_out=hy_b_out[j])
            h_lat = hyena_mixer_block(h_lat, m_lat[:, 3:6], norm_g[l, 1], prm)
            if not last:
                h_ctx = hyena_mixer_block(h_ctx, m_ctx_b[:, 3:6], norm_g[l, 1], prm)
        else:
            prm = dict(w_up=ml_w_up[j].astype(BF16), conv_w=ml_conv_w[j], conv_b=ml_conv_b[j],
                       wq_dense=_dense_blockdiag(ml_w_q[j], ML_QKV_GROUP),
                       wk_dense=_dense_blockdiag(ml_w_k[j], ML_QKV_GROUP),
                       wv_dense=_dense_blockdiag(ml_w_v[j], ML_QKV_GROUP),
                       w_gate=ml_w_gate[j].astype(BF16), b_gate=ml_b_gate[j], norm_g=ml_norm_g[j],
                       skip=ml_skip[j], w_down=ml_w_down[j].astype(BF16))
            new_ctx, h_lat = mlstm_mixer_block(h_ctx, h_lat, m_ctx_b[:, 3:6], m_lat[:, 3:6], norm_g[l, 1], prm,
                                               with_ctx_out=not last)
            h_ctx = h_ctx if last else new_ctx
        h_ctx = h_ctx.reshape(1, bsz * n_ctx, d)

        h_lat = ffn_block(h_lat, m_lat[:, 6:9], norm_g[l, 2], ffn_w_in, ffn_w_out, widx=(l, 1),
                          final_g=final_g if last else None)
        if not last:
            h_ctx = ffn_block(h_ctx, m_ctx[:, 6:9], norm_g[l, 2], ffn_w_in, ffn_w_out, widx=(l, 1))
    return h_lat
```

```python
import functools
import math

import jax
import jax.numpy as jnp
import numpy as np
from jax import lax
from jax.experimental import pallas as pl
from jax.experimental.pallas import tpu as pltpu

F32 = jnp.float32
BF16 = jnp.bfloat16

NORM_EPS = 1e-6
GRID_W = 64
POS_BASE = 10000.0
N_MOD = 9
HY_EMB = 33
HY_BANDS = (HY_EMB - 1) // 2
ML_HEADS = 4
ML_NORM_EPS = 1e-5
ML_QKV_GROUP = 256
SCAN_CHUNK = 256
HALO = 8
CONV_COLS = 256

V7X_VMEM_LIMIT = 56 * 1024 * 1024


def _cparams(*sem):
    return pltpu.CompilerParams(dimension_semantics=sem, vmem_limit_bytes=V7X_VMEM_LIMIT)


def _bdot(a, b):
    return jnp.dot(a.astype(BF16), b.astype(BF16), preferred_element_type=F32)


def _rms_mod(h, g, shift, scale):
    y = h * lax.rsqrt(jnp.mean(h * h, axis=-1, keepdims=True) + NORM_EPS)
    return (y * g) * (1.0 + scale) + shift


def _silu(x):
    return x * (1.0 / (1.0 + jnp.exp(-x)))


def _tile(n, pref):
    if n <= pref:
        return n
    for t in range(pref, 7, -1):
        if n % t == 0 and t % 8 == 0:
            return t
    return n


def _mod_kernel(c_ref, w_ref, b_ref, o_ref):
    o_ref[...] = _bdot(_silu(c_ref[...]), w_ref[...]) + b_ref[...]


def ada_modulation(cond, ada_w, ada_b):
    n_layers, d, n_out = ada_w.shape
    r = cond.shape[0]
    tn = _tile(n_out, 2304) if n_out % 128 == 0 else n_out
    return pl.pallas_call(
        _mod_kernel,
        grid=(n_layers, n_out // tn),
        in_specs=[
            pl.BlockSpec((r, d), lambda l, j: (0, 0)),
            pl.BlockSpec((None, d, tn), lambda l, j: (l, 0, j)),
            pl.BlockSpec((None, 1, tn), lambda l, j: (l, 0, j)),
        ],
        out_specs=pl.BlockSpec((None, r, tn), lambda l, j: (l, 0, j)),
        out_shape=jax.ShapeDtypeStruct((n_layers, r, n_out), F32),
        compiler_params=_cparams("parallel", "parallel"),
        name="ada_modulation",
    )(cond, ada_w, ada_b.reshape(n_layers, 1, n_out))


def _ffn_kernel(*refs, has_pos, final):
    it = iter(refs)
    h_ref = next(it)
    pos_ref = next(it) if has_pos else None
    mod_ref, g_ref, wg_ref, wv_ref, wo_ref = next(it), next(it), next(it), next(it), next(it)
    fg_ref = next(it) if final else None
    o_ref = next(it)
    h = h_ref[...]
    if has_pos:
        h = h + pos_ref[...]
    u = _rms_mod(h, g_ref[...], mod_ref[0:1, :], mod_ref[1:2, :]).astype(BF16)
    gate = jnp.dot(u, wg_ref[...], preferred_element_type=F32)
    val = jnp.dot(u, wv_ref[...], preferred_element_type=F32)
    out = h + (0.5 * mod_ref[2:3, :]) * _bdot(_silu(gate) * val, wo_ref[...])
    if final:
        out = out * lax.rsqrt(jnp.mean(out * out, axis=-1, keepdims=True) + NORM_EPS) * fg_ref[...]
    o_ref[...] = out


def _resident(shape, index_map):
    return pl.BlockSpec(shape, index_map, pipeline_mode=pl.Buffered(1))


def ffn_block(h, mods, g, w_in, w_out, *, widx=(), pos=None, final_g=None, tm=1024):
    bsz, n, d = h.shape
    ff = w_out.shape[-2]
    has_pos, final = pos is not None, final_g is not None
    tm = _tile(n, tm // 2 if has_pos else tm)
    lead = (None,) * len(widx)
    in_specs = [pl.BlockSpec((None, tm, d), lambda b, i: (b, i, 0))]
    args = [h]
    if has_pos:
        in_specs.append(pl.BlockSpec((tm, d), lambda b, i: (i, 0)))
        args.append(pos)
    in_specs += [
        pl.BlockSpec((None, 3, d), lambda b, i: (b, 0, 0)),
        pl.BlockSpec((1, d), lambda b, i: (0, 0)),
        _resident(lead + (d, ff), lambda b, i: widx + (0, 0)),
        _resident(lead + (d, ff), lambda b, i: widx + (0, 1)),
        _resident(lead + (ff, d), lambda b, i: widx + (0, 0)),
    ]
    args += [mods, g.reshape(1, d), w_in, w_in, w_out]
    if final:
        in_specs.append(pl.BlockSpec((1, d), lambda b, i: (0, 0)))
        args.append(final_g.reshape(1, d))
    return pl.pallas_call(
        functools.partial(_ffn_kernel, has_pos=has_pos, final=final),
        grid=(bsz, n // tm),
        in_specs=in_specs,
        out_specs=pl.BlockSpec((None, tm, d), lambda b, i: (b, i, 0)),
        out_shape=jax.ShapeDtypeStruct((bsz, n, d), F32),
        compiler_params=_cparams("parallel", "parallel"),
        name="ffn_block",
    )(*args)


def _linres_kernel(y_ref, h_ref, gate_ref, w_ref, b_ref, o_ref):
    for r, y in enumerate(_unpack_c(y_ref[...])):
        o_ref[r] = h_ref[r] + gate_ref[r] * (_bdot(y, w_ref[...]) + b_ref[...])


def linear_residual(y, h, gate, w, b, *, tm=512):
    npair, n, kdim = y.shape
    d = h.shape[-1]
    tm = _tile(n, tm)
    return pl.pallas_call(
        _linres_kernel,
        grid=(npair, n // tm),
        in_specs=[
            pl.BlockSpec((None, tm, kdim), lambda q, i: (q, i, 0)),
            pl.BlockSpec((2, tm, d), lambda q, i: (q, i, 0)),
            pl.BlockSpec((2, 1, d), lambda q, i: (q, 0, 0)),
            _resident((kdim, d), lambda q, i: (0, 0)),
            pl.BlockSpec((1, d), lambda q, i: (0, 0)),
        ],
        out_specs=pl.BlockSpec((2, tm, d), lambda q, i: (q, i, 0)),
        out_shape=jax.ShapeDtypeStruct(h.shape, F32),
        compiler_params=_cparams("parallel", "parallel"),
        name="linear_residual",
    )(y, h, gate, w, b.reshape(1, d))


def _halo_specs(n, tm, d, nbatch=None):
    hb, last = tm // HALO, n // HALO - 1
    return [
        pl.BlockSpec((nbatch, HALO, d), lambda b, i: (b, jnp.maximum(i * hb - 1, 0), 0)),
        pl.BlockSpec((nbatch, tm, d), lambda b, i: (b, i, 0)),
        pl.BlockSpec((nbatch, HALO, d), lambda b, i: (b, jnp.minimum((i + 1) * hb, last), 0)),
    ]


def _mask_seq_ends(p, tm):
    i, ni = pl.program_id(1), pl.num_programs(1)
    head = jnp.where(i == 0, 0.0, p[:HALO])
    tail = jnp.where(i == ni - 1, 0.0, p[tm + HALO:])
    return jnp.concatenate([head, p[HALO:tm + HALO], tail], axis=0)


def _conv3(p, cw_ref, tm):
    rows = p.shape[0]
    prev = pltpu.roll(p, 1, 0)[HALO:HALO + tm]
    nxt = pltpu.roll(p, rows - 1, 0)[HALO:HALO + tm]
    return prev * cw_ref[0:1, :] + p[HALO:HALO + tm] * cw_ref[1:2, :] + nxt * cw_ref[2:3, :]


def _hyena_in_kernel(hp_ref, h_ref, hn_ref, mod_ref, g_ref, w_ref, b_ref, cw_ref, cb_ref, o_ref):
    tm = h_ref.shape[1]
    us = []
    for r in range(2):
        rows = jnp.concatenate([hp_ref[r], h_ref[r], hn_ref[r]], axis=0)
        us.append(_rms_mod(rows, g_ref[...], mod_ref[r, 0:1, :], mod_ref[r, 1:2, :]).astype(BF16))
    nout = o_ref.shape[-1]
    step = CONV_COLS if nout % CONV_COLS == 0 else nout
    for lo in range(0, nout, step):
        cs = slice(lo, lo + step)
        outs = []
        for u in us:
            p = _mask_seq_ends(jnp.dot(u, w_ref[:, cs], preferred_element_type=F32) + b_ref[:, cs], tm)
            outs.append(_conv3(p, cw_ref.at[:, cs], tm) + cb_ref[:, cs])
        o_ref[:, cs] = _pack_c(outs[0], outs[1])


def hyena_in_proj(h, mods, g, w, b, conv_w, conv_b, *, tm=512):
    bsz, n, d = h.shape
    nout = w.shape[1]
    tm = _tile(n, tm)
    const = lambda shape: pl.BlockSpec(shape, lambda b_, i: (0,) * len(shape))
    return pl.pallas_call(
        _hyena_in_kernel,
        grid=(bsz // 2, n // tm),
        in_specs=_halo_specs(n, tm, d, 2) + [
            pl.BlockSpec((2, 3, d), lambda b_, i: (b_, 0, 0)),
            const((1, d)), _resident((d, nout), lambda b_, i: (0, 0)), const((1, nout)), const((3, nout)),
            const((1, nout)),
        ],
        out_specs=pl.BlockSpec((None, tm, nout), lambda b_, i: (b_, i, 0)),
        out_shape=jax.ShapeDtypeStruct((bsz // 2, n, nout), jnp.uint32),
        compiler_params=_cparams("parallel", "parallel"),
        name="hyena_in_proj",
    )(h, h, h, mods, g.reshape(1, d), w, b.reshape(1, nout), conv_w, conv_b.reshape(1, nout))


def _hdot(a, b):
    return jnp.dot(a, b, preferred_element_type=F32, precision=lax.Precision.HIGHEST)


def _dot3(a, b):
    ah, bh = a.astype(BF16), b.astype(BF16)
    al, bl = (a - ah.astype(F32)).astype(BF16), (b - bh.astype(F32)).astype(BF16)
    dot = functools.partial(jnp.dot, preferred_element_type=F32)
    return dot(ah, bh) + (dot(al, bh) + dot(ah, bl))


def _filter_kernel(ft_ref, w1_ref, b1_ref, w2_ref, b2_ref, w3_ref, b3_ref, fr_ref, ld_ref, k_ref, ss_ref):
    half, i = pl.program_id(0), pl.program_id(1)
    tt = ft_ref.shape[0]
    c = k_ref.shape[-1]
    ft = ft_ref[...]
    h = jnp.sin(fr_ref[0:1, :] * (_hdot(ft, w1_ref[...]) + b1_ref[...]))
    h = jnp.sin(fr_ref[1:2, :] * (_hdot(h, w2_ref[...]) + b2_ref[...]))
    no_lag = (lax.broadcasted_iota(jnp.int32, (tt, c), 0) + (1 - half) + i) == 0

    @pl.when((half == 0) & (i == 0))
    def _():
        ss_ref[...] = jnp.zeros_like(ss_ref)

    ks = []
    for o in range(ss_ref.shape[0]):
        k = (_dot3(h, w3_ref[o]) + b3_ref[o]) * jnp.exp(-ft[:, 0:1] * jnp.exp(ld_ref[o]))
        k = jnp.where(no_lag, 0.0, k)
        ks.append(k)
        ss_ref[o] += jnp.broadcast_to(jnp.sum(k * k, axis=0, keepdims=True), ss_ref.shape[1:])
    k_ref[...] = _pack_c(*ks)


def hyena_filters_time(n, f_w1, f_b1, f_w2, f_b2, f_w3, f_b3, f_freq, log_decay, *, tt=512):
    order, _, d = log_decay.shape
    assert order == 2, "the packed filter word holds exactly two orders"
    fh = f_w2.shape[0]
    emb_pad = 64
    lag = np.arange(2 * n)
    lag = np.where(lag < n, lag, 2 * n - lag).astype(np.float32)
    t_norm = lag / np.float32(max(n - 1, 1))
    bands = np.linspace(1e-4, HY_BANDS - 1, HY_BANDS, dtype=np.float32)
    ang = (np.float32(2.0 * math.pi / n) * lag)[:, None] * bands[None, :]
    feats = np.zeros((2 * n, emb_pad), np.float32)
    feats[:, :HY_EMB] = np.concatenate([t_norm[:, None], np.cos(ang), -np.sin(ang)], axis=-1)
    w1 = jnp.zeros((emb_pad, fh), F32).at[:HY_EMB].set(f_w1)
    tt = _tile(n, tt)
    nt = n // tt
    const = lambda shape: pl.BlockSpec(shape, lambda hf, i: (0,) * len(shape))
    return pl.pallas_call(
        _filter_kernel,
        grid=(2, nt),
        in_specs=[
            pl.BlockSpec((tt, emb_pad), lambda hf, i: (hf * nt + i, 0)),
            const((emb_pad, fh)), const((1, fh)), const((fh, fh)), const((1, fh)),
            pl.BlockSpec((None, order, fh, d), lambda hf, i: (hf, 0, 0, 0)),
            pl.BlockSpec((order, None, 1, d), lambda hf, i: (0, hf, 0, 0)),
            const((2, fh)),
            pl.BlockSpec((order, None, 1, d), lambda hf, i: (0, hf, 0, 0)),
        ],
        out_specs=[
            pl.BlockSpec((tt, d), lambda hf, i: (hf * nt + i, 0)),
            pl.BlockSpec((order, 8, d), lambda hf, i: (0, 0, 0)),
        ],
        out_shape=[jax.ShapeDtypeStruct((2 * n, d), jnp.uint32), jax.ShapeDtypeStruct((order, 8, d), F32)],
        compiler_params=_cparams("arbitrary", "arbitrary"),
        name="hyena_filter_mlp",
    )(jnp.asarray(feats), w1, f_b1.reshape(1, fh), f_w2, f_b2.reshape(1, fh),
      jnp.transpose(f_w3.reshape(fh, order, 2, d), (2, 1, 0, 3)),
      f_b3.reshape(order, 2, 1, d), f_freq, log_decay.reshape(order, 2, 1, d))


def _fft_dims(n):
    nb = 1 << int(math.floor(math.log2(math.sqrt(2 * n))))
    na = 2 * n // nb
    assert na * nb == 2 * n and na % 16 == 0 and nb % 8 == 0, (n, na, nb)
    return na, nb


@functools.lru_cache(maxsize=None)
def _fft_tables(n):
    na, nb = _fft_dims(n)
    na2, nn = na // 2, 2 * n
    ka = np.arange(na, dtype=np.int64)[None, :, None]
    a = np.arange(na2, dtype=np.int64)[None, None, :]
    b = np.arange(nb, dtype=np.int64)[:, None, None]
    ang = (2.0 * np.pi / nn) * ((ka * (a * nb + b)) % nn)
    mr, mi = np.cos(ang), -np.sin(ang)
    f1 = np.concatenate([np.concatenate([mr, -mi], 2), np.concatenate([mi, mr], 2)], 1)
    mrt, mit = np.swapaxes(mr, 1, 2) / nn, np.swapaxes(mi, 1, 2) / nn
    g1 = np.concatenate([np.concatenate([mrt, mit], 2), np.concatenate([-mit, mrt], 2)], 1)
    kb = np.arange(nb, dtype=np.int64)
    ang2 = (2.0 * np.pi / nb) * ((kb[:, None] * kb[None, :]) % nb)
    er, ei = np.cos(ang2), -np.sin(ang2)
    f2 = np.block([[er, -ei], [ei, er]])
    g2 = np.block([[er, ei], [-ei, er]])
    a_all = np.arange(na, dtype=np.int64)[None, None, :]
    ang_f = (2.0 * np.pi / nn) * ((ka * (a_all * nb + b)) % nn)
    f1_real = np.concatenate([np.cos(ang_f), -np.sin(ang_f)], 1)
    as_bf16 = lambda x: x.astype(np.float32).astype(BF16)
    return dict(f1=as_bf16(f1), f1_real=as_bf16(f1_real), g1=as_bf16(g1), f2=as_bf16(f2), g2=as_bf16(g2))


def _pack_c(re, im):
    rb = lax.bitcast_convert_type(re.astype(BF16).astype(F32), jnp.uint32)
    ib = lax.bitcast_convert_type(im.astype(BF16).astype(F32), jnp.uint32)
    return rb | lax.shift_right_logical(ib, jnp.uint32(16))


def _unpack_c(w):
    re = lax.bitcast_convert_type(w & jnp.uint32(0xFFFF0000), F32)
    im = lax.bitcast_convert_type(lax.shift_left(w, jnp.uint32(16)), F32)
    return re, im


def _stack_bf16(re, im):
    return jnp.concatenate([re, im], axis=0).astype(BF16)


LANES = 128
SUB = 8
SHORT_SEQ = 1024


def _lane_cat(parts):
    return parts[0] if len(parts) == 1 else jnp.concatenate(parts, axis=1)


def _flat_rows(ref):
    return ref.reshape(math.prod(ref.shape[:-1]), ref.shape[-1])


def _fft_conv_kernel(x_ref, f1_ref, k_ref, f2_ref, g2_ref, g1_ref, z_ref, m_ref, skip_ref, o_ref, w_ref, *, n1, n2):
    t = pl.program_id(1)
    npair, na2, jb, _ = x_ref.shape
    na, kblk, nb = 2 * na2, k_ref.shape[1], k_ref.shape[2]
    pitch = w_ref.shape[1] // na
    x2, z2, m2, o2 = _flat_rows(x_ref), _flat_rows(z_ref), _flat_rows(m_ref), _flat_rows(o_ref)
    col = lambda q, j: pl.ds(q * na2 * jb + j, na2, stride=jb)
    lane = lambda v, q: v[:, q * LANES:(q + 1) * LANES]

    @pl.when(t < n1)
    def _():
        for j in range(jb):
            xs = _lane_cat([_stack_bf16(*_unpack_c(x2[col(q, j), :])) for q in range(npair)])
            res = jnp.dot(f1_ref[t * jb + j], xs, preferred_element_type=F32)
            packed = _pack_c(res[:na], res[na:])
            for q in range(npair):
                w_ref[q, pl.ds(t * jb + j, na, stride=pitch), :] = lane(packed, q)

    @pl.when((t >= n1) & (t < n1 + n2))
    def _():
        for k in range(kblk):
            rows = pl.ds(pl.multiple_of(((t - n1) * kblk + k) * pitch, SUB), nb)
            ar, ai = _unpack_c(_lane_cat([w_ref[q, rows, :] for q in range(npair)]))
            x = jnp.dot(f2_ref[...], _stack_bf16(ar, ai), preferred_element_type=F32)
            xr, xi = x[:nb], x[nb:]
            kr = _lane_cat([k_ref[0, k].astype(F32)] * npair)
            ki = _lane_cat([k_ref[1, k].astype(F32)] * npair)
            bv = jnp.dot(g2_ref[...], _stack_bf16(xr * kr - xi * ki, xr * ki + xi * kr), preferred_element_type=F32)
            packed = _pack_c(bv[:nb], bv[nb:])
            for q in range(npair):
                w_ref[q, rows, :] = lane(packed, q)

    @pl.when(t >= n1 + n2)
    def _():
        bb = t - (n1 + n2)
        skip = skip_ref[...]
        for j in range(jb):
            br, bi = _unpack_c(_lane_cat([w_ref[q, pl.ds(bb * jb + j, na, stride=pitch), :] for q in range(npair)]))
            y = jnp.dot(g1_ref[bb * jb + j], _stack_bf16(br, bi), preferred_element_type=F32)
            for q in range(npair):
                rows = col(q, j)
                (zr, zi), (mr, mi) = _unpack_c(z2[rows, :]), _unpack_c(m2[rows, :])
                o2[rows, :] = _pack_c(mr * (lane(y[:na2], q) + skip * zr), mi * (lane(y[na2:], q) + skip * zi))


def fft_long_conv(x, x_off, kf, order, z, z_off, m, m_off, skip, n, c, tables, *, jb=16, kblk=32):
    na, nb = _fft_dims(n)
    na2 = na // 2
    jb, kblk = (nb, na) if na * nb <= SHORT_SEQ else (min(jb, nb), min(kblk, na))
    n1, n2 = nb // jb, na // kblk
    npair = x.shape[0]
    view = lambda a: a.reshape(a.shape[0], na2, nb, a.shape[-1])
    xo, zo, mo = x_off // LANES, z_off // LANES, m_off // LANES
    col1 = lambda t: jnp.minimum(t, n1 - 1)
    col3 = lambda t: jnp.clip(t - (n1 + n2), 0, n1 - 1)
    seq = lambda off, col: pl.BlockSpec((npair, na2, jb, LANES), lambda ci, t: (0, 0, col(t), ci + off))
    const = lambda shape: _resident(shape, lambda ci, t: (0,) * len(shape))
    out = pl.pallas_call(
        functools.partial(_fft_conv_kernel, n1=n1, n2=n2),
        grid=(c // LANES, n1 + n2 + n1),
        in_specs=[
            seq(xo, col1),
            const((nb, 2 * na, na)),
            pl.BlockSpec((None, 2, kblk, nb, LANES), lambda ci, t: (order, 0, jnp.clip(t - n1, 0, n2 - 1), 0, ci)),
            const((2 * nb, 2 * nb)), const((2 * nb, 2 * nb)),
            const((nb, na, 2 * na)),
            seq(zo, col3), seq(mo, col3),
            pl.BlockSpec((1, LANES), lambda ci, t: (0, ci)),
        ],
        out_specs=seq(0, col3),
        out_shape=jax.ShapeDtypeStruct((npair, na2, nb, c), jnp.uint32),
        scratch_shapes=[pltpu.VMEM((npair, na * (nb + SUB), LANES), jnp.uint32)],
        compiler_params=_cparams("parallel", "arbitrary"),
        name="fft_long_conv",
    )(view(x), tables["f1"], kf, tables["f2"], tables["g2"], tables["g1"], view(z), view(m), skip.reshape(1, c))
    return out.reshape(npair, n, c)


def _fft_filter_kernel(x_ref, f1_ref, f2_ref, ss_ref, o_ref, w_ref, *, n1):
    t = pl.program_id(1)
    _, na, jb, _ = x_ref.shape
    order, kblk, nb = o_ref.shape[0], o_ref.shape[2], o_ref.shape[3]
    pitch = w_ref.shape[1] // na
    x2 = _flat_rows(x_ref)
    lane = lambda v, q: v[:, q * LANES:(q + 1) * LANES]

    @pl.when(t < n1)
    def _():
        for j in range(jb):
            xs = _lane_cat(list(_unpack_c(x2[pl.ds(j, na, stride=jb), :])))
            res = jnp.dot(f1_ref[t * jb + j], xs.astype(BF16), preferred_element_type=F32)
            packed = _pack_c(res[:na], res[na:])
            for o in range(order):
                w_ref[o, pl.ds(t * jb + j, na, stride=pitch), :] = lane(packed, o)

    @pl.when(t >= n1)
    def _():
        scale = _lane_cat([lax.rsqrt(ss_ref[o, 0:1, :] + 1e-12) for o in range(order)])
        for k in range(kblk):
            rows = pl.ds(pl.multiple_of(((t - n1) * kblk + k) * pitch, SUB), nb)
            ar, ai = _unpack_c(_lane_cat([w_ref[o, rows, :] for o in range(order)]))
            x = jnp.dot(f2_ref[...], _stack_bf16(ar, ai), preferred_element_type=F32) * scale
            for o in range(order):
                o_ref[o, 0, k] = lane(x[:nb], o).astype(o_ref.dtype)
                o_ref[o, 1, k] = lane(x[nb:], o).astype(o_ref.dtype)


def fft_filter_spectrum(kt, ss, n, tables, *, jb=16, kblk=32):
    na, nb = _fft_dims(n)
    order, c = ss.shape[0], kt.shape[-1]
    jb, kblk = (nb, na) if na * nb <= SHORT_SEQ else (min(jb, nb), min(kblk, na))
    n1, n2 = nb // jb, na // kblk
    return pl.pallas_call(
        functools.partial(_fft_filter_kernel, n1=n1),
        grid=(c // LANES, n1 + n2),
        in_specs=[
            pl.BlockSpec((1, na, jb, LANES), lambda ci, t: (0, 0, jnp.minimum(t, n1 - 1), ci)),
            _resident((nb, 2 * na, na), lambda ci, t: (0, 0, 0)),
            _resident((2 * nb, 2 * nb), lambda ci, t: (0, 0)),
            pl.BlockSpec((order, 8, LANES), lambda ci, t: (0, 0, ci)),
        ],
        out_specs=pl.BlockSpec((order, 2, kblk, nb, LANES), lambda ci, t: (0, 0, jnp.maximum(t - n1, 0), 0, ci)),
        out_shape=jax.ShapeDtypeStruct((order, 2, na, nb, c), BF16),
        scratch_shapes=[pltpu.VMEM((order, na * (nb + SUB), LANES), jnp.uint32)],
        compiler_params=_cparams("parallel", "arbitrary"),
        name="fft_filter_spectrum",
    )(kt.reshape(1, na, nb, c), tables["f1_real"], tables["f2"], ss)


def hyena_mixer_block(h, mods, g, prm):
    bsz, n, d = h.shape
    assert bsz % 2 == 0 and d % LANES == 0, "batch rows travel in pairs, channels in LANES-wide columns"
    tables = _fft_tables(n)
    proj = hyena_in_proj(h, mods, g, prm["w_in"], prm["b_in"], prm["conv_w"], prm["conv_b"])
    kt, ss = hyena_filters_time(n, prm["f_w1"], prm["f_b1"], prm["f_w2"], prm["f_b2"], prm["f_w3"],
                                prm["f_b3"], prm["f_freq"], prm["log_decay"])
    kf = fft_filter_spectrum(kt, ss, n, tables)
    z = fft_long_conv(proj, 2 * d, kf, 0, proj, 2 * d, proj, 0, prm["skip"][0], n, d, tables)
    y = fft_long_conv(z, 0, kf, 1, z, 0, proj, d, prm["skip"][1], n, d, tables)
    return linear_residual(y, h, mods[:, 2:3, :], prm["w_out"], prm["b_out"])


def _mlstm_pre_kernel(hp_ref, h_ref, hn_ref, mod_ref, g_ref, wup_ref, cw_ref, cb_ref, wq_ref, wk_ref, wkt_ref, wv_ref,
                      wg_ref, bg_ref, q_ref, kt_ref, v_ref, xc_ref, z_ref, gt_ref, *, k_scale):
    tm = h_ref.shape[0]
    inner = q_ref.shape[-1]
    grp = wq_ref.shape[-1]
    rows = jnp.concatenate([hp_ref[...], h_ref[...], hn_ref[...]], axis=0)
    u = _rms_mod(rows, g_ref[...], mod_ref[0:1, :], mod_ref[1:2, :])
    up = _bdot(u, wup_ref[...])
    xm_h = _mask_seq_ends(up[:, :inner], tm)
    z_ref[...] = up[HALO:HALO + tm, inner:].astype(z_ref.dtype)
    xc_f = _silu(_conv3(xm_h, cw_ref, tm) + cb_ref[...])
    xc_ref[...] = xc_f.astype(xc_ref.dtype)
    xc_all, xm_all = xc_f.astype(BF16), xm_h[HALO:HALO + tm].astype(BF16)
    gates = jnp.zeros(gt_ref.shape, F32) + bg_ref[...]
    for gi in range(inner // grp):
        lo, hi = gi * grp, (gi + 1) * grp
        xc, xm = xc_all[:, lo:hi], xm_all[:, lo:hi]
        q = jnp.dot(xc, wq_ref[gi], preferred_element_type=F32)
        k = jnp.dot(xc, wk_ref[gi], preferred_element_type=F32)
        v = jnp.dot(xm, wv_ref[gi], preferred_element_type=F32)
        q_ref[:, lo:hi] = q.astype(q_ref.dtype)
        kt_ref[lo:hi, :] = (_nt(wkt_ref[gi], xc) * k_scale).astype(kt_ref.dtype)
        v_ref[:, lo:hi] = v.astype(v_ref.dtype)
        gates += (_bdot(q, wg_ref[lo:hi, :]) + _bdot(k, wg_ref[inner + lo:inner + hi, :])
                  + _bdot(v, wg_ref[2 * inner + lo:2 * inner + hi, :]))
    gt_ref[...] = gates


def mlstm_pre(h, mods, g, prm, *, tm=512):
    bsz, n, d = h.shape
    inner = prm["w_up"].shape[1] // 2
    ngate = prm["w_gate"].shape[1]
    grp = prm["wq_dense"].shape[-1]
    tm = _tile(n, tm)
    const = lambda shape: pl.BlockSpec(shape, lambda b_, i: (0,) * len(shape))
    row_spec = lambda w: pl.BlockSpec((None, tm, w), lambda b_, i: (b_, i, 0))
    mm = jax.ShapeDtypeStruct((bsz, n, inner), BF16)
    return pl.pallas_call(
        functools.partial(_mlstm_pre_kernel, k_scale=float((inner // ML_HEADS) ** -0.5)),
        grid=(bsz, n // tm),
        in_specs=_halo_specs(n, tm, d) + [
            pl.BlockSpec((None, 3, d), lambda b_, i: (b_, 0, 0)),
            const((1, d)), _resident((d, 2 * inner), lambda b_, i: (0, 0)), const((3, inner)), const((1, inner)),
            const((inner // grp, grp, grp)), const((inner // grp, grp, grp)), const((inner // grp, grp, grp)),
            const((inner // grp, grp, grp)), const((3 * inner, ngate)), const((1, ngate)),
        ],
        out_specs=[row_spec(inner), pl.BlockSpec((None, inner, tm), lambda b_, i: (b_, 0, i))]
                  + [row_spec(inner)] * 3 + [row_spec(ngate)],
        out_shape=[mm, jax.ShapeDtypeStruct((bsz, inner, n), BF16), mm, mm, mm,
                   jax.ShapeDtypeStruct((bsz, n, ngate), F32)],
        compiler_params=_cparams("parallel", "parallel"),
        name="mlstm_pre",
    )(h, h, h, mods, g.reshape(1, d), prm["w_up"], prm["conv_w"], prm["conv_b"].reshape(1, inner),
      prm["wq_dense"], prm["wk_dense"], jnp.swapaxes(prm["wk_dense"], 1, 2), prm["wv_dense"], prm["w_gate"],
      prm["b_gate"].reshape(1, ngate))


def _dense_blockdiag(w, grp):
    nb, bs, _ = w.shape
    per = grp // bs
    eye = jnp.eye(per, dtype=w.dtype)
    dense = jnp.einsum("gmde,mn->gmdne", w.reshape(nb // per, per, bs, bs), eye)
    return dense.reshape(nb // per, grp, grp).astype(BF16)


def _split3(x):
    p1 = x.astype(BF16)
    r1 = x - p1.astype(F32)
    p2 = r1.astype(BF16)
    p3 = (r1 - p2.astype(F32)).astype(BF16)
    return p1, p2, p3


def _nt(a, b):
    return lax.dot_general(a, b, (((1,), (1,)), ((), ())), preferred_element_type=F32)


def _scan_kernel(*refs, has_init, emit_state, heads_per_step):
    it = iter(refs)
    q_ref, kt_ref, v_ref, gt_ref = next(it), next(it), next(it), next(it)
    allow_ref, allow_t_ref, neg_ref = next(it), next(it), next(it)
    init_refs = [next(it) for _ in range(3)] if has_init else None
    h_ref = next(it)
    final_refs = [next(it) for _ in range(3)] if emit_state else None
    state = c_sc, n_sc, m_sc = next(it), next(it), next(it)
    c = pl.program_id(3)
    nrow, ch = q_ref.shape[0], q_ref.shape[1]
    dh = q_ref.shape[2] // heads_per_step

    @pl.when(c == 0)
    def _():
        for sc, src in zip(state, init_refs or [None] * 3):
            sc[...] = jnp.zeros_like(sc) if src is None else src[...]

    allow, allow_t, neg = allow_ref[...], allow_t_ref[...], neg_ref[...]
    ones_rows = jnp.ones((SUB, ch), BF16)

    streams = [(r, hh) for r in range(nrow) for hh in range(heads_per_step)]
    cols = [slice(hh * dh, (hh + 1) * dh) for hh in range(heads_per_step)]

    def gate_terms(st):
        i_row = gt_ref[st[0], st[1], 0:1, :]
        f_row = gt_ref[st[0], st[1], 1:2, :]
        lf_row = jnp.minimum(f_row, 0.0) - jnp.log1p(jnp.exp(-jnp.abs(f_row)))
        pieces = jnp.concatenate(list(_split3(lf_row)) + [jnp.zeros((SUB - 3, ch), BF16)], axis=0)
        cum_col = _nt(allow, pieces)
        cum_row = jnp.dot(pieces, allow_t, preferred_element_type=F32)
        b_col = cum_col[:, 0:1] + cum_col[:, 1:2] + cum_col[:, 2:3]
        b_row = cum_row[0:1, :] + cum_row[1:2, :] + cum_row[2:3, :]
        total = jnp.sum(lf_row, axis=1, keepdims=True)
        e_row = i_row - b_row
        m_st = m_sc[st[0], st[1], 0:1, 0:1]
        dmat = (b_col + e_row) + neg
        a_col = b_col + m_st
        m_j = jnp.maximum(a_col, jnp.max(dmat, axis=1, keepdims=True))
        g_row = total + e_row
        m_new = jnp.maximum(total + m_st, jnp.max(g_row, axis=1, keepdims=True))
        return dict(w_inter=jnp.exp(a_col - m_j), dexp=jnp.exp(dmat - m_j), floor=jnp.exp(-m_j), m_new=m_new,
                    decay=jnp.exp(total + m_st - m_new), w_row=jnp.exp(g_row - m_new))

    def read_out(st, t):
        r, cs = st[0], cols[st[1]]
        q, kt, v = q_ref[r, :, cs], kt_ref[r, cs, :], v_ref[r, :, cs]
        s = jnp.dot(q, kt, preferred_element_type=F32) * t["dexp"]
        num = t["w_inter"] * jnp.dot(q, c_sc[st].astype(BF16), preferred_element_type=F32) + _bdot(s, v)
        qn = (t["w_inter"] * jnp.sum(q.astype(F32) * n_sc[st[0], st[1], 0:1, :], axis=1, keepdims=True)
              + jnp.sum(s, axis=1, keepdims=True))
        h_ref[r, :, cs] = (num / jnp.maximum(jnp.abs(qn), t["floor"])).astype(h_ref.dtype)

    def update(st, t):
        r, cs = st[0], cols[st[1]]
        kwt = (kt_ref[r, cs, :].astype(F32) * t["w_row"]).astype(BF16)
        c_sc[st] = t["decay"] * c_sc[st] + jnp.dot(kwt, v_ref[r, :, cs], preferred_element_type=F32)
        n_sc[st] = t["decay"] * n_sc[st] + _nt(ones_rows, kwt)
        m_sc[st] = jnp.broadcast_to(t["m_new"], m_sc.shape[2:])

    terms = [gate_terms(st) for st in streams]
    for st, t in zip(streams, terms):
        read_out(st, t)
        update(st, t)

    if emit_state:
        @pl.when(c == pl.num_programs(3) - 1)
        def _():
            for dst, sc in zip(final_refs, state):
                dst[...] = sc[...]


@functools.lru_cache(maxsize=None)
def _scan_masks(ch):
    r = np.arange(ch)
    causal = (r[None, :] <= r[:, None]).astype(np.float32)
    allow = np.stack([causal, causal.T])
    neg = np.where(allow > 0, 0.0, -np.inf).astype(np.float32)
    return allow.astype(BF16), np.transpose(allow, (0, 2, 1)).astype(BF16), neg


def mlstm_scan(q, kt, v, gates_t, init=None, *, emit_state=False, rows_per_step=2):
    bsz, n, inner = q.shape
    nh = gates_t.shape[2]
    dh = inner // nh
    hp = nh if nh <= 4 else (2 if nh % 2 == 0 else 1)
    nr = rows_per_step if bsz % rows_per_step == 0 else 1
    ch = min(SCAN_CHUNK, n)
    nc = n // ch
    ceff = lambda d, c: c + d * (nc - 1 - 2 * c)
    seq = pl.BlockSpec((nr, ch, hp * dh), lambda d, b, hd, c: (b, ceff(d, c), hd))
    state_shapes = [(dh, dh), (SUB, dh), (SUB, 128)]
    st_specs = [pl.BlockSpec((None, nr, hp) + s, lambda d, b, hd, c: (d, b, hd, 0, 0)) for s in state_shapes]
    by_dir = pl.BlockSpec((None, ch, ch), lambda d, b, hd, c: (d, 0, 0))
    in_specs = [seq, pl.BlockSpec((nr, hp * dh, ch), lambda d, b, hd, c: (b, hd, ceff(d, c))), seq,
                pl.BlockSpec((nr, None, hp, 2, ch), lambda d, b, hd, c: (b, d, hd, 0, ceff(d, c))),
                by_dir, by_dir, by_dir]
    args = [q, kt, v, gates_t, *_scan_masks(ch)]
    if init is not None:
        in_specs += st_specs
        args += list(init)
    out_specs = [pl.BlockSpec((None, nr, ch, hp * dh), lambda d, b, hd, c: (d, b, ceff(d, c), hd))]
    out_shape = [jax.ShapeDtypeStruct((2, bsz, n, inner), BF16)]
    if emit_state:
        out_specs += st_specs
        out_shape += [jax.ShapeDtypeStruct((2, bsz, nh) + s, F32) for s in state_shapes]
    res = pl.pallas_call(
        functools.partial(_scan_kernel, has_init=init is not None, emit_state=emit_state, heads_per_step=hp),
        grid=(2, bsz // nr, nh // hp, nc),
        in_specs=in_specs,
        out_specs=out_specs,
        out_shape=out_shape,
        scratch_shapes=[pltpu.VMEM((nr, hp) + s, F32) for s in state_shapes],
        compiler_params=_cparams("parallel", "parallel", "parallel", "arbitrary"),
        name="mlstm_scan",
    )(*args)
    return (res[0], tuple(res[1:])) if emit_state else (res[0], None)


def _mlstm_post_kernel(hf_ref, hb_ref, xc_ref, z_ref, res_ref, gate_ref, ng_ref, sk_ref, wd_ref, o_ref):
    inner = xc_ref.shape[-1]
    dh = inner // ML_HEADS
    acc = jnp.zeros(o_ref.shape, F32)
    for hd in range(ML_HEADS):
        lo, hi = hd * dh, (hd + 1) * dh
        x = hf_ref[:, lo:hi].astype(F32) + hb_ref[:, lo:hi].astype(F32)
        xz = x - jnp.mean(x, axis=-1, keepdims=True)
        var = jnp.mean(xz * xz, axis=-1, keepdims=True)
        hn = xz * lax.rsqrt(var + ML_NORM_EPS) * ng_ref[:, lo:hi]
        t = (hn + sk_ref[:, lo:hi] * xc_ref[:, lo:hi].astype(F32)) * _silu(z_ref[:, lo:hi].astype(F32))
        acc += _bdot(t, wd_ref[lo:hi, :])
    o_ref[...] = res_ref[...] + gate_ref[...] * acc


def mlstm_post(hs, xc, z, h_res, gate, prm, *, tm=512):
    bsz, n, inner = xc.shape
    d = h_res.shape[-1]
    tm = _tile(n, tm)
    const = lambda shape: pl.BlockSpec(shape, lambda b_, i: (0,) * len(shape))
    return pl.pallas_call(
        _mlstm_post_kernel,
        grid=(bsz, n // tm),
        in_specs=[
            pl.BlockSpec((None, None, tm, inner), lambda b_, i: (0, b_, i, 0)),
            pl.BlockSpec((None, None, tm, inner), lambda b_, i: (1, b_, i, 0)),
            pl.BlockSpec((None, tm, inner), lambda b_, i: (b_, i, 0)),
            pl.BlockSpec((None, tm, inner), lambda b_, i: (b_, i, 0)),
            pl.BlockSpec((None, tm, d), lambda b_, i: (b_, i, 0)),
            pl.BlockSpec((None, 1, d), lambda b_, i: (b_, 0, 0)),
            const((1, inner)), const((1, inner)), _resident((inner, d), lambda b_, i: (0, 0)),
        ],
        out_specs=pl.BlockSpec((None, tm, d), lambda b_, i: (b_, i, 0)),
        out_shape=jax.ShapeDtypeStruct((bsz, n, d), F32),
        compiler_params=_cparams("parallel", "parallel"),
        name="mlstm_post",
    )(hs, hs, xc, z, h_res, gate, prm["norm_g"].reshape(1, inner), prm["skip"].reshape(1, inner), prm["w_down"])


def _gates_by_head(gates, nh):
    bsz, n, _ = gates.shape
    return jnp.transpose(gates.reshape(bsz, n, 2, 2, nh), (0, 2, 4, 3, 1))


def mlstm_mixer_block(h_ctx, h_lat, mods_ctx, mods_lat, g, prm, *, with_ctx_out):
    pc = mlstm_pre(h_ctx, mods_ctx, g, prm)
    pq = mlstm_pre(h_lat, mods_lat, g, prm)
    hs_ctx, state = mlstm_scan(pc[0], pc[1], pc[2], _gates_by_head(pc[5], ML_HEADS), emit_state=True)
    hs_lat, _ = mlstm_scan(pq[0], pq[1], pq[2], _gates_by_head(pq[5], ML_HEADS), init=state)
    out_lat = mlstm_post(hs_lat, pq[3], pq[4], h_lat, mods_lat[:, 2:3, :], prm)
    out_ctx = mlstm_post(hs_ctx, pc[3], pc[4], h_ctx, mods_ctx[:, 2:3, :], prm) if with_ctx_out else None
    return out_ctx, out_lat


def _pos_embed_2d(n_tokens, d):
    rows = n_tokens // GRID_W
    quarter = d // 4
    omega = 1.0 / (POS_BASE ** (jnp.arange(quarter, dtype=F32) / quarter))
    ang_r = jnp.arange(rows, dtype=F32)[:, None] * omega
    ang_c = jnp.arange(GRID_W, dtype=F32)[:, None] * omega
    row_emb = jnp.concatenate([jnp.sin(ang_r), jnp.cos(ang_r)], axis=-1)
    col_emb = jnp.concatenate([jnp.sin(ang_c), jnp.cos(ang_c)], axis=-1)
    emb = jnp.concatenate([
        jnp.broadcast_to(row_emb[:, None, :], (rows, GRID_W, d // 2)),
        jnp.broadcast_to(col_emb[None, :, :], (rows, GRID_W, d // 2))], axis=-1)
    return emb.reshape(rows * GRID_W, d)


def kernel(x, c, ctx, c_ctx, ada_w, ada_b, norm_g, final_g, ffn_w_in, ffn_w_out, hy_w_in, hy_b_in, hy_conv_w, hy_conv_b, hy_f_w1, hy_f_b1, hy_f_w2, hy_f_b2, hy_f_w3, hy_f_b3, hy_f_freq, hy_log_decay, hy_skip, hy_w_out, hy_b_out, ml_w_up, ml_conv_w, ml_conv_b, ml_w_q, ml_w_k, ml_w_v, ml_w_gate, ml_b_gate, ml_norm_g, ml_skip, ml_w_down):
    bsz, n_lat, d = x.shape
    n_ctx = ctx.shape[1]
    depth = ada_w.shape[0]
    n_mixers = 2

    cond_rows = 8
    cond = jnp.zeros((cond_rows, d), F32).at[:bsz].set(c).at[bsz].set(c_ctx)
    mods_all = ada_modulation(cond, ada_w, ada_b).reshape(depth, cond_rows, N_MOD, d)

    ffn_w_in = ffn_w_in.astype(BF16)
    ffn_w_out = ffn_w_out.astype(BF16)
    pos = _pos_embed_2d(n_lat, d)

    h_lat = x
    h_ctx = ctx.reshape(1, bsz * n_ctx, d)
    for l in range(depth):
        last = l == depth - 1
        j = l // n_mixers
        m_lat = mods_all[l, :bsz]
        m_ctx = mods_all[l, bsz:bsz + 1]
        m_ctx_b = jnp.broadcast_to(m_ctx, (bsz, N_MOD, d))

        h_lat = ffn_block(h_lat, m_lat[:, 0:3], norm_g[l, 0], ffn_w_in, ffn_w_out, widx=(l, 0),
                          pos=pos if l == 0 else None)
        h_ctx = ffn_block(h_ctx, m_ctx[:, 0:3], norm_g[l, 0], ffn_w_in, ffn_w_out, widx=(l, 0))

        h_ctx = h_ctx.reshape(bsz, n_ctx, d)
        if l % n_mixers == 0:
            prm = dict(w_in=hy_w_in[j].astype(BF16), b_in=hy_b_in[j], conv_w=hy_conv_w[j], conv_b=hy_conv_b[j],
                       f_w1=hy_f_w1[j], f_b1=hy_f_b1[j], f_w2=hy_f_w2[j], f_b2=hy_f_b2[j], f_w3=hy_f_w3[j],
                       f_b3=hy_f_b3[j], f_freq=hy_f_freq[j], log_decay=hy_log_decay[j], skip=hy_skip[j],
                       w_out=hy_w_out[j].astype(BF16), b_out=hy_b_out[j])
            h_lat = hyena_mixer_block(h_lat, m_lat[:, 3:6], norm_g[l, 1], prm)
            if not last:
                h_ctx = hyena_mixer_block(h_ctx, m_ctx_b[:, 3:6], norm_g[l, 1], prm)
        else:
            prm = dict(w_up=ml_w_up[j].astype(BF16), conv_w=ml_conv_w[j], conv_b=ml_conv_b[j],
                       wq_dense=_dense_blockdiag(ml_w_q[j], ML_QKV_GROUP),
                       wk_dense=_dense_blockdiag(ml_w_k[j], ML_QKV_GROUP),
                       wv_dense=_dense_blockdiag(ml_w_v[j], ML_QKV_GROUP),
                       w_gate=ml_w_gate[j].astype(BF16), b_gate=ml_b_gate[j], norm_g=ml_norm_g[j],
                       skip=ml_skip[j], w_down=ml_w_down[j].astype(BF16))
            new_ctx, h_lat = mlstm_mixer_block(h_ctx, h_lat, m_ctx_b[:, 3:6], m_lat[:, 3:6], norm_g[l, 1], prm,
                                               with_ctx_out=not last)
            h_ctx = h_ctx if last else new_ctx
        h_ctx = h_ctx.reshape(1, bsz * n_ctx, d)

        h_lat = ffn_block(h_lat, m_lat[:, 6:9], norm_g[l, 2], ffn_w_in, ffn_w_out, widx=(l, 1),
                          final_g=final_g if last else None)
        if not last:
            h_ctx = ffn_block(h_ctx, m_ctx[:, 6:9], norm_g[l, 2], ffn_w_in, ffn_w_out, widx=(l, 1))
    return h_lat
```

```python
import functools
import math

import jax
import jax.numpy as jnp
import numpy as np
from jax import lax
from jax.experimental import pallas as pl
from jax.experimental.pallas import tpu as pltpu

F32 = jnp.float32
BF16 = jnp.bfloat16

NORM_EPS = 1e-6
GRID_W = 64
POS_BASE = 10000.0
N_MOD = 9
HY_EMB = 33
HY_BANDS = (HY_EMB - 1) // 2
ML_HEADS = 4
ML_NORM_EPS = 1e-5
ML_QKV_GROUP = 256
SCAN_CHUNK = 256
HALO = 8
CONV_COLS = 256

V7X_VMEM_LIMIT = 56 * 1024 * 1024


def _cparams(*sem):
    return pltpu.CompilerParams(dimension_semantics=sem, vmem_limit_bytes=V7X_VMEM_LIMIT)


def _bdot(a, b):
    return jnp.dot(a.astype(BF16), b.astype(BF16), preferred_element_type=F32)


def _rms_mod(h, g, shift, scale):
    y = h * lax.rsqrt(jnp.mean(h * h, axis=-1, keepdims=True) + NORM_EPS)
    return (y * g) * (1.0 + scale) + shift


def _silu(x):
    return x * (1.0 / (1.0 + jnp.exp(-x)))


def _tile(n, pref):
    if n <= pref:
        return n
    for t in range(pref, 7, -1):
        if n % t == 0 and t % 8 == 0:
            return t
    return n


def _mod_kernel(c_ref, w_ref, b_ref, o_ref):
    o_ref[...] = _bdot(_silu(c_ref[...]), w_ref[...]) + b_ref[...]


def ada_modulation(cond, ada_w, ada_b):
    n_layers, d, n_out = ada_w.shape
    r = cond.shape[0]
    tn = _tile(n_out, 2304) if n_out % 128 == 0 else n_out
    return pl.pallas_call(
        _mod_kernel,
        grid=(n_layers, n_out // tn),
        in_specs=[
            pl.BlockSpec((r, d), lambda l, j: (0, 0)),
            pl.BlockSpec((None, d, tn), lambda l, j: (l, 0, j)),
            pl.BlockSpec((None, 1, tn), lambda l, j: (l, 0, j)),
        ],
        out_specs=pl.BlockSpec((None, r, tn), lambda l, j: (l, 0, j)),
        out_shape=jax.ShapeDtypeStruct((n_layers, r, n_out), F32),
        compiler_params=_cparams("parallel", "parallel"),
        name="ada_modulation",
    )(cond, ada_w, ada_b.reshape(n_layers, 1, n_out))


def _ffn_kernel(*refs, has_pos, final):
    it = iter(refs)
    h_ref = next(it)
    pos_ref = next(it) if has_pos else None
    mod_ref, g_ref, wg_ref, wv_ref, wo_ref = next(it), next(it), next(it), next(it), next(it)
    fg_ref = next(it) if final else None
    o_ref = next(it)
    h = h_ref[...]
    if has_pos:
        h = h + pos_ref[...]
    u = _rms_mod(h, g_ref[...], mod_ref[0:1, :], mod_ref[1:2, :]).astype(BF16)
    gate = jnp.dot(u, wg_ref[...], preferred_element_type=F32)
    val = jnp.dot(u, wv_ref[...], preferred_element_type=F32)
    out = h + (0.5 * mod_ref[2:3, :]) * _bdot(_silu(gate) * val, wo_ref[...])
    if final:
        out = out * lax.rsqrt(jnp.mean(out * out, axis=-1, keepdims=True) + NORM_EPS) * fg_ref[...]
    o_ref[...] = out


def _resident(shape, index_map):
    return pl.BlockSpec(shape, index_map, pipeline_mode=pl.Buffered(1))


def ffn_block(h, mods, g, w_in, w_out, *, widx=(), pos=None, final_g=None, tm=1024):
    bsz, n, d = h.shape
    ff = w_out.shape[-2]
    has_pos, final = pos is not None, final_g is not None
    tm = _tile(n, tm // 2 if has_pos else tm)
    lead = (None,) * len(widx)
    in_specs = [pl.BlockSpec((None, tm, d), lambda b, i: (b, i, 0))]
    args = [h]
    if has_pos:
        in_specs.append(pl.BlockSpec((tm, d), lambda b, i: (i, 0)))
        args.append(pos)
    in_specs += [
        pl.BlockSpec((None, 3, d), lambda b, i: (b, 0, 0)),
        pl.BlockSpec((1, d), lambda b, i: (0, 0)),
        _resident(lead + (d, ff), lambda b, i: widx + (0, 0)),
        _resident(lead + (d, ff), lambda b, i: widx + (0, 1)),
        _resident(lead + (ff, d), lambda b, i: widx + (0, 0)),
    ]
    args += [mods, g.reshape(1, d), w_in, w_in, w_out]
    if final:
        in_specs.append(pl.BlockSpec((1, d), lambda b, i: (0, 0)))
        args.append(final_g.reshape(1, d))
    return pl.pallas_call(
        functools.partial(_ffn_kernel, has_pos=has_pos, final=final),
        grid=(bsz, n // tm),
        in_specs=in_specs,
        out_specs=pl.BlockSpec((None, tm, d), lambda b, i: (b, i, 0)),
        out_shape=jax.ShapeDtypeStruct((bsz, n, d), F32),
        compiler_params=_cparams("parallel", "parallel"),
        name="ffn_block",
    )(*args)


def _linres_kernel(y_ref, h_ref, gate_ref, w_ref, b_ref, o_ref):
    for r, y in enumerate(_unpack_c(y_ref[...])):
        o_ref[r] = h_ref[r] + gate_ref[r] * (_bdot(y, w_ref[...]) + b_ref[...])


def linear_residual(y, h, gate, w, b, *, tm=512):
    npair, n, kdim = y.shape
    d = h.shape[-1]
    tm = _tile(n, tm)
    return pl.pallas_call(
        _linres_kernel,
        grid=(npair, n // tm),
        in_specs=[
            pl.BlockSpec((None, tm, kdim), lambda q, i: (q, i, 0)),
            pl.BlockSpec((2, tm, d), lambda q, i: (q, i, 0)),
            pl.BlockSpec((2, 1, d), lambda q, i: (q, 0, 0)),
            _resident((kdim, d), lambda q, i: (0, 0)),
            pl.BlockSpec((1, d), lambda q, i: (0, 0)),
        ],
        out_specs=pl.BlockSpec((2, tm, d), lambda q, i: (q, i, 0)),
        out_shape=jax.ShapeDtypeStruct(h.shape, F32),
        compiler_params=_cparams("parallel", "parallel"),
        name="linear_residual",
    )(y, h, gate, w, b.reshape(1, d))


def _halo_specs(n, tm, d, nbatch=None):
    hb, last = tm // HALO, n // HALO - 1
    return [
        pl.BlockSpec((nbatch, HALO, d), lambda b, i: (b, jnp.maximum(i * hb - 1, 0), 0)),
        pl.BlockSpec((nbatch, tm, d), lambda b, i: (b, i, 0)),
        pl.BlockSpec((nbatch, HALO, d), lambda b, i: (b, jnp.minimum((i + 1) * hb, last), 0)),
    ]


def _mask_seq_ends(p, tm):
    i, ni = pl.program_id(1), pl.num_programs(1)
    head = jnp.where(i == 0, 0.0, p[:HALO])
    tail = jnp.where(i == ni - 1, 0.0, p[tm + HALO:])
    return jnp.concatenate([head, p[HALO:tm + HALO], tail], axis=0)


def _conv3(p, cw_ref, tm):
    rows = p.shape[0]
    prev = pltpu.roll(p, 1, 0)[HALO:HALO + tm]
    nxt = pltpu.roll(p, rows - 1, 0)[HALO:HALO + tm]
    return prev * cw_ref[0:1, :] + p[HALO:HALO + tm] * cw_ref[1:2, :] + nxt * cw_ref[2:3, :]


def _hyena_in_kernel(hp_ref, h_ref, hn_ref, mod_ref, g_ref, w_ref, b_ref, cw_ref, cb_ref, o_ref):
    tm = h_ref.shape[1]
    us = []
    for r in range(2):
        rows = jnp.concatenate([hp_ref[r], h_ref[r], hn_ref[r]], axis=0)
        us.append(_rms_mod(rows, g_ref[...], mod_ref[r, 0:1, :], mod_ref[r, 1:2, :]).astype(BF16))
    nout = o_ref.shape[-1]
    step = CONV_COLS if nout % CONV_COLS == 0 else nout
    for lo in range(0, nout, step):
        cs = slice(lo, lo + step)
        outs = []
        for u in us:
            p = _mask_seq_ends(jnp.dot(u, w_ref[:, cs], preferred_element_type=F32) + b_ref[:, cs], tm)
            outs.append(_conv3(p, cw_ref.at[:, cs], tm) + cb_ref[:, cs])
        o_ref[:, cs] = _pack_c(outs[0], outs[1])


def hyena_in_proj(h, mods, g, w, b, conv_w, conv_b, *, tm=512):
    bsz, n, d = h.shape
    nout = w.shape[1]
    tm = _tile(n, tm)
    const = lambda shape: pl.BlockSpec(shape, lambda b_, i: (0,) * len(shape))
    return pl.pallas_call(
        _hyena_in_kernel,
        grid=(bsz // 2, n // tm),
        in_specs=_halo_specs(n, tm, d, 2) + [
            pl.BlockSpec((2, 3, d), lambda b_, i: (b_, 0, 0)),
            const((1, d)), _resident((d, nout), lambda b_, i: (0, 0)), const((1, nout)), const((3, nout)),
            const((1, nout)),
        ],
        out_specs=pl.BlockSpec((None, tm, nout), lambda b_, i: (b_, i, 0)),
        out_shape=jax.ShapeDtypeStruct((bsz // 2, n, nout), jnp.uint32),
        compiler_params=_cparams("parallel", "parallel"),
        name="hyena_in_proj",
    )(h, h, h, mods, g.reshape(1, d), w, b.reshape(1, nout), conv_w, conv_b.reshape(1, nout))


def _hdot(a, b):
    return jnp.dot(a, b, preferred_element_type=F32, precision=lax.Precision.HIGHEST)


def _dot3(a, b):
    ah, bh = a.astype(BF16), b.astype(BF16)
    al, bl = (a - ah.astype(F32)).astype(BF16), (b - bh.astype(F32)).astype(BF16)
    dot = functools.partial(jnp.dot, preferred_element_type=F32)
    return dot(ah, bh) + (dot(al, bh) + dot(ah, bl))


def _filter_kernel(ft_ref, w1_ref, b1_ref, w2_ref, b2_ref, w3_ref, b3_ref, fr_ref, ld_ref, k_ref, ss_ref):
    half, i = pl.program_id(0), pl.program_id(1)
    tt = ft_ref.shape[0]
    c = k_ref.shape[-1]
    ft = ft_ref[...]
    h = jnp.sin(fr_ref[0:1, :] * (_hdot(ft, w1_ref[...]) + b1_ref[...]))
    h = jnp.sin(fr_ref[1:2, :] * (_hdot(h, w2_ref[...]) + b2_ref[...]))
    no_lag = (lax.broadcasted_iota(jnp.int32, (tt, c), 0) + (1 - half) + i) == 0

    @pl.when((half == 0) & (i == 0))
    def _():
        ss_ref[...] = jnp.zeros_like(ss_ref)

    ks = []
    for o in range(ss_ref.shape[0]):
        k = (_dot3(h, w3_ref[o]) + b3_ref[o]) * jnp.exp(-ft[:, 0:1] * jnp.exp(ld_ref[o]))
        k = jnp.where(no_lag, 0.0, k)
        ks.append(k)
        ss_ref[o] += jnp.broadcast_to(jnp.sum(k * k, axis=0, keepdims=True), ss_ref.shape[1:])
    k_ref[...] = _pack_c(*ks)


def hyena_filters_time(n, f_w1, f_b1, f_w2, f_b2, f_w3, f_b3, f_freq, log_decay, *, tt=512):
    order, _, d = log_decay.shape
    assert order == 2, "the packed filter word holds exactly two orders"
    fh = f_w2.shape[0]
    emb_pad = 64
    lag = np.arange(2 * n)
    lag = np.where(lag < n, lag, 2 * n - lag).astype(np.float32)
    t_norm = lag / np.float32(max(n - 1, 1))
    bands = np.linspace(1e-4, HY_BANDS - 1, HY_BANDS, dtype=np.float32)
    ang = (np.float32(2.0 * math.pi / n) * lag)[:, None] * bands[None, :]
    feats = np.zeros((2 * n, emb_pad), np.float32)
    feats[:, :HY_EMB] = np.concatenate([t_norm[:, None], np.cos(ang), -np.sin(ang)], axis=-1)
    w1 = jnp.zeros((emb_pad, fh), F32).at[:HY_EMB].set(f_w1)
    tt = _tile(n, tt)
    nt = n // tt
    const = lambda shape: pl.BlockSpec(shape, lambda hf, i: (0,) * len(shape))
    return pl.pallas_call(
        _filter_kernel,
        grid=(2, nt),
        in_specs=[
            pl.BlockSpec((tt, emb_pad), lambda hf, i: (hf * nt + i, 0)),
            const((emb_pad, fh)), const((1, fh)), const((fh, fh)), const((1, fh)),
            pl.BlockSpec((None, order, fh, d), lambda hf, i: (hf, 0, 0, 0)),
            pl.BlockSpec((order, None, 1, d), lambda hf, i: (0, hf, 0, 0)),
            const((2, fh)),
            pl.BlockSpec((order, None, 1, d), lambda hf, i: (0, hf, 0, 0)),
        ],
        out_specs=[
            pl.BlockSpec((tt, d), lambda hf, i: (hf * nt + i, 0)),
            pl.BlockSpec((order, 8, d), lambda hf, i: (0, 0, 0)),
        ],
        out_shape=[jax.ShapeDtypeStruct((2 * n, d), jnp.uint32), jax.ShapeDtypeStruct((order, 8, d), F32)],
        compiler_params=_cparams("arbitrary", "arbitrary"),
        name="hyena_filter_mlp",
    )(jnp.asarray(feats), w1, f_b1.reshape(1, fh), f_w2, f_b2.reshape(1, fh),
      jnp.transpose(f_w3.reshape(fh, order, 2, d), (2, 1, 0, 3)),
      f_b3.reshape(order, 2, 1, d), f_freq, log_decay.reshape(order, 2, 1, d))


def _fft_dims(n):
    nb = 1 << int(math.floor(math.log2(math.sqrt(2 * n))))
    na = 2 * n // nb
    assert na * nb == 2 * n and na % 16 == 0 and nb % 8 == 0, (n, na, nb)
    return na, nb


@functools.lru_cache(maxsize=None)
def _fft_tables(n):
    na, nb = _fft_dims(n)
    na2, nn = na // 2, 2 * n
    ka = np.arange(na, dtype=np.int64)[None, :, None]
    a = np.arange(na2, dtype=np.int64)[None, None, :]
    b = np.arange(nb, dtype=np.int64)[:, None, None]
    ang = (2.0 * np.pi / nn) * ((ka * (a * nb + b)) % nn)
    mr, mi = np.cos(ang), -np.sin(ang)
    f1 = np.concatenate([np.concatenate([mr, -mi], 2), np.concatenate([mi, mr], 2)], 1)
    mrt, mit = np.swapaxes(mr, 1, 2) / nn, np.swapaxes(mi, 1, 2) / nn
    g1 = np.concatenate([np.concatenate([mrt, mit], 2), np.concatenate([-mit, mrt], 2)], 1)
    kb = np.arange(nb, dtype=np.int64)
    ang2 = (2.0 * np.pi / nb) * ((kb[:, None] * kb[None, :]) % nb)
    er, ei = np.cos(ang2), -np.sin(ang2)
    f2 = np.block([[er, -ei], [ei, er]])
    g2 = np.block([[er, ei], [-ei, er]])
    a_all = np.arange(na, dtype=np.int64)[None, None, :]
    ang_f = (2.0 * np.pi / nn) * ((ka * (a_all * nb + b)) % nn)
    f1_real = np.concatenate([np.cos(ang_f), -np.sin(ang_f)], 1)
    as_bf16 = lambda x: x.astype(np.float32).astype(BF16)
    return dict(f1=as_bf16(f1), f1_real=as_bf16(f1_real), g1=as_bf16(g1), f2=as_bf16(f2), g2=as_bf16(g2))


def _pack_c(re, im):
    rb = lax.bitcast_convert_type(re.astype(BF16).astype(F32), jnp.uint32)
    ib = lax.bitcast_convert_type(im.astype(BF16).astype(F32), jnp.uint32)
    return rb | lax.shift_right_logical(ib, jnp.uint32(16))


def _unpack_c(w):
    re = lax.bitcast_convert_type(w & jnp.uint32(0xFFFF0000), F32)
    im = lax.bitcast_convert_type(lax.shift_left(w, jnp.uint32(16)), F32)
    return re, im


def _stack_bf16(re, im):
    return jnp.concatenate([re, im], axis=0).astype(BF16)


LANES = 128
SUB = 8
SHORT_SEQ = 1024


def _lane_cat(parts):
    return parts[0] if len(parts) == 1 else jnp.concatenate(parts, axis=1)


def _flat_rows(ref):
    return ref.reshape(math.prod(ref.shape[:-1]), ref.shape[-1])


def _fft_conv_kernel(x_ref, f1_ref, k_ref, f2_ref, g2_ref, g1_ref, z_ref, m_ref, skip_ref, o_ref, w_ref, *, n1, n2):
    t = pl.program_id(1)
    npair, na2, jb, _ = x_ref.shape
    na, kblk, nb = 2 * na2, k_ref.shape[1], k_ref.shape[2]
    pitch = w_ref.shape[1] // na
    x2, z2, m2, o2 = _flat_rows(x_ref), _flat_rows(z_ref), _flat_rows(m_ref), _flat_rows(o_ref)
    col = lambda q, j: pl.ds(q * na2 * jb + j, na2, stride=jb)
    lane = lambda v, q: v[:, q * LANES:(q + 1) * LANES]

    @pl.when(t < n1)
    def _():
        for j in range(jb):
            xs = _lane_cat([_stack_bf16(*_unpack_c(x2[col(q, j), :])) for q in range(npair)])
            res = jnp.dot(f1_ref[t * jb + j], xs, preferred_element_type=F32)
            packed = _pack_c(res[:na], res[na:])
            for q in range(npair):
                w_ref[q, pl.ds(t * jb + j, na, stride=pitch), :] = lane(packed, q)

    @pl.when((t >= n1) & (t < n1 + n2))
    def _():
        for k in range(kblk):
            rows = pl.ds(pl.multiple_of(((t - n1) * kblk + k) * pitch, SUB), nb)
            ar, ai = _unpack_c(_lane_cat([w_ref[q, rows, :] for q in range(npair)]))
            x = jnp.dot(f2_ref[...], _stack_bf16(ar, ai), preferred_element_type=F32)
            xr, xi = x[:nb], x[nb:]
            kr = _lane_cat([k_ref[0, k].astype(F32)] * npair)
            ki = _lane_cat([k_ref[1, k].astype(F32)] * npair)
            bv = jnp.dot(g2_ref[...], _stack_bf16(xr * kr - xi * ki, xr * ki + xi * kr), preferred_element_type=F32)
            packed = _pack_c(bv[:nb], bv[nb:])
            for q in range(npair):
                w_ref[q, rows, :] = lane(packed, q)

    @pl.when(t >= n1 + n2)
    def _():
        bb = t - (n1 + n2)
        skip = skip_ref[...]
        for j in range(jb):
            br, bi = _unpack_c(_lane_cat([w_ref[q, pl.ds(bb * jb + j, na, stride=pitch), :] for q in range(npair)]))
            y = jnp.dot(g1_ref[bb * jb + j], _stack_bf16(br, bi), preferred_element_type=F32)
            for q in range(npair):
                rows = col(q, j)
                (zr, zi), (mr, mi) = _unpack_c(z2[rows, :]), _unpack_c(m2[rows, :])
                o2[rows, :] = _pack_c(mr * (lane(y[:na2], q) + skip * zr), mi * (lane(y[na2:], q) + skip * zi))


def fft_long_conv(x, x_off, kf, order, z, z_off, m, m_off, skip, n, c, tables, *, jb=16, kblk=32):
    na, nb = _fft_dims(n)
    na2 = na // 2
    jb, kblk = (nb, na) if na * nb <= SHORT_SEQ else (min(jb, nb), min(kblk, na))
    n1, n2 = nb // jb, na // kblk
    npair = x.shape[0]
    view = lambda a: a.reshape(a.shape[0], na2, nb, a.shape[-1])
    xo, zo, mo = x_off // LANES, z_off // LANES, m_off // LANES
    col1 = lambda t: jnp.minimum(t, n1 - 1)
    col3 = lambda t: jnp.clip(t - (n1 + n2), 0, n1 - 1)
    seq = lambda off, col: pl.BlockSpec((npair, na2, jb, LANES), lambda ci, t: (0, 0, col(t), ci + off))
    const = lambda shape: _resident(shape, lambda ci, t: (0,) * len(shape))
    out = pl.pallas_call(
        functools.partial(_fft_conv_kernel, n1=n1, n2=n2),
        grid=(c // LANES, n1 + n2 + n1),
        in_specs=[
            seq(xo, col1),
            const((nb, 2 * na, na)),
            pl.BlockSpec((None, 2, kblk, nb, LANES), lambda ci, t: (order, 0, jnp.clip(t - n1, 0, n2 - 1), 0, ci)),
            const((2 * nb, 2 * nb)), const((2 * nb, 2 * nb)),
            const((nb, na, 2 * na)),
            seq(zo, col3), seq(mo, col3),
            pl.BlockSpec((1, LANES), lambda ci, t: (0, ci)),
        ],
        out_specs=seq(0, col3),
        out_shape=jax.ShapeDtypeStruct((npair, na2, nb, c), jnp.uint32),
        scratch_shapes=[pltpu.VMEM((npair, na * (nb + SUB), LANES), jnp.uint32)],
        compiler_params=_cparams("parallel", "arbitrary"),
        name="fft_long_conv",
    )(view(x), tables["f1"], kf, tables["f2"], tables["g2"], tables["g1"], view(z), view(m), skip.reshape(1, c))
    return out.reshape(npair, n, c)


def _fft_filter_kernel(x_ref, f1_ref, f2_ref, ss_ref, o_ref, w_ref, *, n1):
    t = pl.program_id(1)
    _, na, jb, _ = x_ref.shape
    order, kblk, nb = o_ref.shape[0], o_ref.shape[2], o_ref.shape[3]
    pitch = w_ref.shape[1] // na
    x2 = _flat_rows(x_ref)
    lane = lambda v, q: v[:, q * LANES:(q + 1) * LANES]

    @pl.when(t < n1)
    def _():
        for j in range(jb):
            xs = _lane_cat(list(_unpack_c(x2[pl.ds(j, na, stride=jb), :])))
            res = jnp.dot(f1_ref[t * jb + j], xs.astype(BF16), preferred_element_type=F32)
            packed = _pack_c(res[:na], res[na:])
            for o in range(order):
                w_ref[o, pl.ds(t * jb + j, na, stride=pitch), :] = lane(packed, o)

    @pl.when(t >= n1)
    def _():
        scale = _lane_cat([lax.rsqrt(ss_ref[o, 0:1, :] + 1e-12) for o in range(order)])
        for k in range(kblk):
            rows = pl.ds(pl.multiple_of(((t - n1) * kblk + k) * pitch, SUB), nb)
            ar, ai = _unpack_c(_lane_cat([w_ref[o, rows, :] for o in range(order)]))
            x = jnp.dot(f2_ref[...], _stack_bf16(ar, ai), preferred_element_type=F32) * scale
            for o in range(order):
                o_ref[o, 0, k] = lane(x[:nb], o).astype(o_ref.dtype)
                o_ref[o, 1, k] = lane(x[nb:], o).astype(o_ref.dtype)


def fft_filter_spectrum(kt, ss, n, tables, *, jb=16, kblk=32):
    na, nb = _fft_dims(n)
    order, c = ss.shape[0], kt.shape[-1]
    jb, kblk = (nb, na) if na * nb <= SHORT_SEQ else (min(jb, nb), min(kblk, na))
    n1, n2 = nb // jb, na // kblk
    return pl.pallas_call(
        functools.partial(_fft_filter_kernel, n1=n1),
        grid=(c // LANES, n1 + n2),
        in_specs=[
            pl.BlockSpec((1, na, jb, LANES), lambda ci, t: (0, 0, jnp.minimum(t, n1 - 1), ci)),
            _resident((nb, 2 * na, na), lambda ci, t: (0, 0, 0)),
            _resident((2 * nb, 2 * nb), lambda ci, t: (0, 0)),
            pl.BlockSpec((order, 8, LANES), lambda ci, t: (0, 0, ci)),
        ],
        out_specs=pl.BlockSpec((order, 2, kblk, nb, LANES), lambda ci, t: (0, 0, jnp.maximum(t - n1, 0), 0, ci)),
        out_shape=jax.ShapeDtypeStruct((order, 2, na, nb, c), BF16),
        scratch_shapes=[pltpu.VMEM((order, na * (nb + SUB), LANES), jnp.uint32)],
        compiler_params=_cparams("parallel", "arbitrary"),
        name="fft_filter_spectrum",
    )(kt.reshape(1, na, nb, c), tables["f1_real"], tables["f2"], ss)


def hyena_mixer_block(h, mods, g, prm):
    bsz, n, d = h.shape
    assert bsz % 2 == 0 and d % LANES == 0, "batch rows travel in pairs, channels in LANES-wide columns"
    tables = _fft_tables(n)
    proj = hyena_in_proj(h, mods, g, prm["w_in"], prm["b_in"], prm["conv_w"], prm["conv_b"])
    kt, ss = hyena_filters_time(n, prm["f_w1"], prm["f_b1"], prm["f_w2"], prm["f_b2"], prm["f_w3"],
                                prm["f_b3"], prm["f_freq"], prm["log_decay"])
    kf = fft_filter_spectrum(kt, ss, n, tables)
    z = fft_long_conv(proj, 2 * d, kf, 0, proj, 2 * d, proj, 0, prm["skip"][0], n, d, tables)
    y = fft_long_conv(z, 0, kf, 1, z, 0, proj, d, prm["skip"][1], n, d, tables)
    return linear_residual(y, h, mods[:, 2:3, :], prm["w_out"], prm["b_out"])


def _mlstm_pre_kernel(hp_ref, h_ref, hn_ref, mod_ref, g_ref, wup_ref, cw_ref, cb_ref, wq_ref, wk_ref, wkt_ref, wv_ref,
                      wg_ref, bg_ref, q_ref, kt_ref, v_ref, xc_ref, z_ref, gt_ref, *, k_scale):
    tm = h_ref.shape[0]
    inner = q_ref.shape[-1]
    grp = wq_ref.shape[-1]
    rows = jnp.concatenate([hp_ref[...], h_ref[...], hn_ref[...]], axis=0)
    u = _rms_mod(rows, g_ref[...], mod_ref[0:1, :], mod_ref[1:2, :])
    up = _bdot(u, wup_ref[...])
    xm_h = _mask_seq_ends(up[:, :inner], tm)
    z_ref[...] = up[HALO:HALO + tm, inner:].astype(z_ref.dtype)
    xc_f = _silu(_conv3(xm_h, cw_ref, tm) + cb_ref[...])
    xc_ref[...] = xc_f.astype(xc_ref.dtype)
    xc_all, xm_all = xc_f.astype(BF16), xm_h[HALO:HALO + tm].astype(BF16)
    gates = jnp.zeros(gt_ref.shape, F32) + bg_ref[...]
    for gi in range(inner // grp):
        lo, hi = gi * grp, (gi + 1) * grp
        xc, xm = xc_all[:, lo:hi], xm_all[:, lo:hi]
        q = jnp.dot(xc, wq_ref[gi], preferred_element_type=F32)
        k = jnp.dot(xc, wk_ref[gi], preferred_element_type=F32)
        v = jnp.dot(xm, wv_ref[gi], preferred_element_type=F32)
        q_ref[:, lo:hi] = q.astype(q_ref.dtype)
        kt_ref[lo:hi, :] = (_nt(wkt_ref[gi], xc) * k_scale).astype(kt_ref.dtype)
        v_ref[:, lo:hi] = v.astype(v_ref.dtype)
        gates += (_bdot(q, wg_ref[lo:hi, :]) + _bdot(k, wg_ref[inner + lo:inner + hi, :])
                  + _bdot(v, wg_ref[2 * inner + lo:2 * inner + hi, :]))
    gt_ref[...] = gates


def mlstm_pre(h, mods, g, prm, *, tm=512):
    bsz, n, d = h.shape
    inner = prm["w_up"].shape[1] // 2
    ngate = prm["w_gate"].shape[1]
    grp = prm["wq_dense"].shape[-1]
    tm = _tile(n, tm)
    const = lambda shape: pl.BlockSpec(shape, lambda b_, i: (0,) * len(shape))
    row_spec = lambda w: pl.BlockSpec((None, tm, w), lambda b_, i: (b_, i, 0))
    mm = jax.ShapeDtypeStruct((bsz, n, inner), BF16)
    return pl.pallas_call(
        functools.partial(_mlstm_pre_kernel, k_scale=float((inner // ML_HEADS) ** -0.5)),
        grid=(bsz, n // tm),
        in_specs=_halo_specs(n, tm, d) + [
            pl.BlockSpec((None, 3, d), lambda b_, i: (b_, 0, 0)),
            const((1, d)), _resident((d, 2 * inner), lambda b_, i: (0, 0)), const((3, inner)), const((1, inner)),
            const((inner // grp, grp, grp)), const((inner // grp, grp, grp)), const((inner // grp, grp, grp)),
            const((inner // grp, grp, grp)), const((3 * inner, ngate)), const((1, ngate)),
        ],
        out_specs=[row_spec(inner), pl.BlockSpec((None, inner, tm), lambda b_, i: (b_, 0, i))]
                  + [row_spec(inner)] * 3 + [row_spec(ngate)],
        out_shape=[mm, jax.ShapeDtypeStruct((bsz, inner, n), BF16), mm, mm, mm,
                   jax.ShapeDtypeStruct((bsz, n, ngate), F32)],
        compiler_params=_cparams("parallel", "parallel"),
        name="mlstm_pre",
    )(h, h, h, mods, g.reshape(1, d), prm["w_up"], prm["conv_w"], prm["conv_b"].reshape(1, inner),
      prm["wq_dense"], prm["wk_dense"], prm["wkt_dense"], prm["wv_dense"], prm["w_gate"],
      prm["b_gate"].reshape(1, ngate))


def _dense_blockdiag(w, grp):
    nb, bs, _ = w.shape
    per = grp // bs
    rows = jnp.tile(w.reshape(nb // per, grp, bs), (1, 1, per))
    blk = np.arange(grp) // bs
    return (rows * jnp.asarray(blk[:, None] == blk[None, :], w.dtype)).astype(BF16)


def _split3(x):
    p1 = x.astype(BF16)
    r1 = x - p1.astype(F32)
    p2 = r1.astype(BF16)
    p3 = (r1 - p2.astype(F32)).astype(BF16)
    return p1, p2, p3


def _nt(a, b):
    return lax.dot_general(a, b, (((1,), (1,)), ((), ())), preferred_element_type=F32)


def _scan_kernel(*refs, has_init, emit_state, heads_per_step):
    it = iter(refs)
    q_ref, kt_ref, v_ref, gt_ref = next(it), next(it), next(it), next(it)
    allow_ref, allow_t_ref, neg_ref = next(it), next(it), next(it)
    init_refs = [next(it) for _ in range(3)] if has_init else None
    h_ref = next(it)
    final_refs = [next(it) for _ in range(3)] if emit_state else None
    state = c_sc, n_sc, m_sc = next(it), next(it), next(it)
    c = pl.program_id(3)
    nrow, ch = q_ref.shape[0], q_ref.shape[1]
    dh = q_ref.shape[2] // heads_per_step

    @pl.when(c == 0)
    def _():
        for sc, src in zip(state, init_refs or [None] * 3):
            sc[...] = jnp.zeros_like(sc) if src is None else src[...]

    allow, allow_t, neg = allow_ref[...], allow_t_ref[...], neg_ref[...]
    ones_rows = jnp.ones((SUB, ch), BF16)

    streams = [(r, hh) for r in range(nrow) for hh in range(heads_per_step)]
    cols = [slice(hh * dh, (hh + 1) * dh) for hh in range(heads_per_step)]

    def gate_terms(st):
        i_row = gt_ref[st[0], st[1], 0:1, :]
        f_row = gt_ref[st[0], st[1], 1:2, :]
        lf_row = jnp.minimum(f_row, 0.0) - jnp.log1p(jnp.exp(-jnp.abs(f_row)))
        pieces = jnp.concatenate(list(_split3(lf_row)) + [jnp.zeros((SUB - 3, ch), BF16)], axis=0)
        cum_col = _nt(allow, pieces)
        cum_row = jnp.dot(pieces, allow_t, preferred_element_type=F32)
        b_col = cum_col[:, 0:1] + cum_col[:, 1:2] + cum_col[:, 2:3]
        b_row = cum_row[0:1, :] + cum_row[1:2, :] + cum_row[2:3, :]
        total = jnp.sum(lf_row, axis=1, keepdims=True)
        e_row = i_row - b_row
        m_st = m_sc[st[0], st[1], 0:1, 0:1]
        dmat = (b_col + e_row) + neg
        a_col = b_col + m_st
        m_j = jnp.maximum(a_col, jnp.max(dmat, axis=1, keepdims=True))
        g_row = total + e_row
        m_new = jnp.maximum(total + m_st, jnp.max(g_row, axis=1, keepdims=True))
        return dict(w_inter=jnp.exp(a_col - m_j), dexp=jnp.exp(dmat - m_j), floor=jnp.exp(-m_j), m_new=m_new,
                    decay=jnp.exp(total + m_st - m_new), w_row=jnp.exp(g_row - m_new))

    def read_out(st, t):
        r, cs = st[0], cols[st[1]]
        q, kt, v = q_ref[r, :, cs], kt_ref[r, cs, :], v_ref[r, :, cs]
        s = jnp.dot(q, kt, preferred_element_type=F32) * t["dexp"]
        num = t["w_inter"] * jnp.dot(q, c_sc[st].astype(BF16), preferred_element_type=F32) + _bdot(s, v)
        qn = (t["w_inter"] * jnp.sum(q.astype(F32) * n_sc[st[0], st[1], 0:1, :], axis=1, keepdims=True)
              + jnp.sum(s, axis=1, keepdims=True))
        h_ref[r, :, cs] = (num / jnp.maximum(jnp.abs(qn), t["floor"])).astype(h_ref.dtype)

    def update(st, t):
        r, cs = st[0], cols[st[1]]
        kwt = (kt_ref[r, cs, :].astype(F32) * t["w_row"]).astype(BF16)
        c_sc[st] = t["decay"] * c_sc[st] + jnp.dot(kwt, v_ref[r, :, cs], preferred_element_type=F32)
        n_sc[st] = t["decay"] * n_sc[st] + _nt(ones_rows, kwt)
        m_sc[st] = jnp.broadcast_to(t["m_new"], m_sc.shape[2:])

    terms = [gate_terms(st) for st in streams]
    for st, t in zip(streams, terms):
        read_out(st, t)
        update(st, t)

    if emit_state:
        @pl.when(c == pl.num_programs(3) - 1)
        def _():
            for dst, sc in zip(final_refs, state):
                dst[...] = sc[...]


@functools.lru_cache(maxsize=None)
def _scan_masks(ch):
    r = np.arange(ch)
    causal = (r[None, :] <= r[:, None]).astype(np.float32)
    allow = np.stack([causal, causal.T])
    neg = np.where(allow > 0, 0.0, -np.inf).astype(np.float32)
    return allow.astype(BF16), np.transpose(allow, (0, 2, 1)).astype(BF16), neg


def mlstm_scan(q, kt, v, gates_t, init=None, *, emit_state=False, rows_per_step=2):
    bsz, n, inner = q.shape
    nh = gates_t.shape[2]
    dh = inner // nh
    hp = nh if nh <= 4 else (2 if nh % 2 == 0 else 1)
    nr = rows_per_step if bsz % rows_per_step == 0 else 1
    ch = min(SCAN_CHUNK, n)
    nc = n // ch
    ceff = lambda d, c: c + d * (nc - 1 - 2 * c)
    seq = pl.BlockSpec((nr, ch, hp * dh), lambda d, b, hd, c: (b, ceff(d, c), hd))
    state_shapes = [(dh, dh), (SUB, dh), (SUB, 128)]
    st_specs = [pl.BlockSpec((None, nr, hp) + s, lambda d, b, hd, c: (d, b, hd, 0, 0)) for s in state_shapes]
    by_dir = pl.BlockSpec((None, ch, ch), lambda d, b, hd, c: (d, 0, 0))
    in_specs = [seq, pl.BlockSpec((nr, hp * dh, ch), lambda d, b, hd, c: (b, hd, ceff(d, c))), seq,
                pl.BlockSpec((nr, None, hp, 2, ch), lambda d, b, hd, c: (b, d, hd, 0, ceff(d, c))),
                by_dir, by_dir, by_dir]
    args = [q, kt, v, gates_t, *_scan_masks(ch)]
    if init is not None:
        in_specs += st_specs
        args += list(init)
    out_specs = [pl.BlockSpec((None, nr, ch, hp * dh), lambda d, b, hd, c: (d, b, ceff(d, c), hd))]
    out_shape = [jax.ShapeDtypeStruct((2, bsz, n, inner), BF16)]
    if emit_state:
        out_specs += st_specs
        out_shape += [jax.ShapeDtypeStruct((2, bsz, nh) + s, F32) for s in state_shapes]
    res = pl.pallas_call(
        functools.partial(_scan_kernel, has_init=init is not None, emit_state=emit_state, heads_per_step=hp),
        grid=(2, bsz // nr, nh // hp, nc),
        in_specs=in_specs,
        out_specs=out_specs,
        out_shape=out_shape,
        scratch_shapes=[pltpu.VMEM((nr, hp) + s, F32) for s in state_shapes],
        compiler_params=_cparams("parallel", "parallel", "parallel", "arbitrary"),
        name="mlstm_scan",
    )(*args)
    return (res[0], tuple(res[1:])) if emit_state else (res[0], None)


def _mlstm_post_kernel(hf_ref, hb_ref, xc_ref, z_ref, res_ref, gate_ref, ng_ref, sk_ref, wd_ref, o_ref):
    inner = xc_ref.shape[-1]
    dh = inner // ML_HEADS
    acc = jnp.zeros(o_ref.shape, F32)
    for hd in range(ML_HEADS):
        lo, hi = hd * dh, (hd + 1) * dh
        x = hf_ref[:, lo:hi].astype(F32) + hb_ref[:, lo:hi].astype(F32)
        xz = x - jnp.mean(x, axis=-1, keepdims=True)
        var = jnp.mean(xz * xz, axis=-1, keepdims=True)
        hn = xz * lax.rsqrt(var + ML_NORM_EPS) * ng_ref[:, lo:hi]
        t = (hn + sk_ref[:, lo:hi] * xc_ref[:, lo:hi].astype(F32)) * _silu(z_ref[:, lo:hi].astype(F32))
        acc += _bdot(t, wd_ref[lo:hi, :])
    o_ref[...] = res_ref[...] + gate_ref[...] * acc


def mlstm_post(hs, xc, z, h_res, gate, prm, *, tm=512):
    bsz, n, inner = xc.shape
    d = h_res.shape[-1]
    tm = _tile(n, tm)
    const = lambda shape: pl.BlockSpec(shape, lambda b_, i: (0,) * len(shape))
    return pl.pallas_call(
        _mlstm_post_kernel,
        grid=(bsz, n // tm),
        in_specs=[
            pl.BlockSpec((None, None, tm, inner), lambda b_, i: (0, b_, i, 0)),
            pl.BlockSpec((None, None, tm, inner), lambda b_, i: (1, b_, i, 0)),
            pl.BlockSpec((None, tm, inner), lambda b_, i: (b_, i, 0)),
            pl.BlockSpec((None, tm, inner), lambda b_, i: (b_, i, 0)),
            pl.BlockSpec((None, tm, d), lambda b_, i: (b_, i, 0)),
            pl.BlockSpec((None, 1, d), lambda b_, i: (b_, 0, 0)),
            const((1, inner)), const((1, inner)), _resident((inner, d), lambda b_, i: (0, 0)),
        ],
        out_specs=pl.BlockSpec((None, tm, d), lambda b_, i: (b_, i, 0)),
        out_shape=jax.ShapeDtypeStruct((bsz, n, d), F32),
        compiler_params=_cparams("parallel", "parallel"),
        name="mlstm_post",
    )(hs, hs, xc, z, h_res, gate, prm["norm_g"].reshape(1, inner), prm["skip"].reshape(1, inner), prm["w_down"])


def _gates_by_head(gates, nh):
    bsz, n, _ = gates.shape
    return jnp.transpose(gates.reshape(bsz, n, 2, 2, nh), (0, 2, 4, 3, 1))


def mlstm_mixer_block(h_ctx, h_lat, mods_ctx, mods_lat, g, prm, *, with_ctx_out):
    pc = mlstm_pre(h_ctx, mods_ctx, g, prm)
    pq = mlstm_pre(h_lat, mods_lat, g, prm)
    hs_ctx, state = mlstm_scan(pc[0], pc[1], pc[2], _gates_by_head(pc[5], ML_HEADS), emit_state=True)
    hs_lat, _ = mlstm_scan(pq[0], pq[1], pq[2], _gates_by_head(pq[5], ML_HEADS), init=state)
    out_lat = mlstm_post(hs_lat, pq[3], pq[4], h_lat, mods_lat[:, 2:3, :], prm)
    out_ctx = mlstm_post(hs_ctx, pc[3], pc[4], h_ctx, mods_ctx[:, 2:3, :], prm) if with_ctx_out else None
    return out_ctx, out_lat


def _pos_embed_2d(n_tokens, d):
    rows = n_tokens // GRID_W
    quarter = d // 4
    omega = 1.0 / (POS_BASE ** (jnp.arange(quarter, dtype=F32) / quarter))
    ang_r = jnp.arange(rows, dtype=F32)[:, None] * omega
    ang_c = jnp.arange(GRID_W, dtype=F32)[:, None] * omega
    row_emb = jnp.concatenate([jnp.sin(ang_r), jnp.cos(ang_r)], axis=-1)
    col_emb = jnp.concatenate([jnp.sin(ang_c), jnp.cos(ang_c)], axis=-1)
    emb = jnp.concatenate([
        jnp.broadcast_to(row_emb[:, None, :], (rows, GRID_W, d // 2)),
        jnp.broadcast_to(col_emb[None, :, :], (rows, GRID_W, d // 2))], axis=-1)
    return emb.reshape(rows * GRID_W, d)


def kernel(x, c, ctx, c_ctx, ada_w, ada_b, norm_g, final_g, ffn_w_in, ffn_w_out, hy_w_in, hy_b_in, hy_conv_w, hy_conv_b, hy_f_w1, hy_f_b1, hy_f_w2, hy_f_b2, hy_f_w3, hy_f_b3, hy_f_freq, hy_log_decay, hy_skip, hy_w_out, hy_b_out, ml_w_up, ml_conv_w, ml_conv_b, ml_w_q, ml_w_k, ml_w_v, ml_w_gate, ml_b_gate, ml_norm_g, ml_skip, ml_w_down):
    bsz, n_lat, d = x.shape
    n_ctx = ctx.shape[1]
    depth = ada_w.shape[0]
    n_mixers = 2

    cond_rows = 8
    cond = jnp.zeros((cond_rows, d), F32).at[:bsz].set(c).at[bsz].set(c_ctx)
    mods_all = ada_modulation(cond, ada_w, ada_b).reshape(depth, cond_rows, N_MOD, d)

    ffn_w_in = ffn_w_in.astype(BF16)
    ffn_w_out = ffn_w_out.astype(BF16)
    pos = _pos_embed_2d(n_lat, d)

    h_lat = x
    h_ctx = ctx.reshape(1, bsz * n_ctx, d)
    for l in range(depth):
        last = l == depth - 1
        j = l // n_mixers
        m_lat = mods_all[l, :bsz]
        m_ctx = mods_all[l, bsz:bsz + 1]
        m_ctx_b = jnp.broadcast_to(m_ctx, (bsz, N_MOD, d))

        h_lat = ffn_block(h_lat, m_lat[:, 0:3], norm_g[l, 0], ffn_w_in, ffn_w_out, widx=(l, 0),
                          pos=pos if l == 0 else None)
        h_ctx = ffn_block(h_ctx, m_ctx[:, 0:3], norm_g[l, 0], ffn_w_in, ffn_w_out, widx=(l, 0))

        h_ctx = h_ctx.reshape(bsz, n_ctx, d)
        if l % n_mixers == 0:
            prm = dict(w_in=hy_w_in[j].astype(BF16), b_in=hy_b_in[j], conv_w=hy_conv_w[j], conv_b=hy_conv_b[j],
                       f_w1=hy_f_w1[j], f_b1=hy_f_b1[j], f_w2=hy_f_w2[j], f_b2=hy_f_b2[j], f_w3=hy_f_w3[j],
                       f_b3=hy_f_b3[j], f_freq=hy_f_freq[j], log_decay=hy_log_decay[j], skip=hy_skip[j],
                       w_out=hy_w_out[j].astype(BF16), b_out=hy_b_out[j])
            h_lat = hyena_mixer_block(h_lat, m_lat[:, 3:6], norm_g[l, 1], prm)
            if not last:
                h_ctx = hyena_mixer_block(h_ctx, m_ctx_b[:, 3:6], norm_g[l, 1], prm)
        else:
            prm = dict(w_up=ml_w_up[j].astype(BF16), conv_w=ml_conv_w[j], conv_b=ml_conv_b[j],
                       wq_dense=_dense_blockdiag(ml_w_q[j], ML_QKV_GROUP),
                       wk_dense=_dense_blockdiag(ml_w_k[j], ML_QKV_GROUP),
                       wkt_dense=_dense_blockdiag(jnp.swapaxes(ml_w_k[j], 1, 2), ML_QKV_GROUP),
                       wv_dense=_dense_blockdiag(ml_w_v[j], ML_QKV_GROUP),
                       w_gate=ml_w_gate[j].astype(BF16), b_gate=ml_b_gate[j], norm_g=ml_norm_g[j],
                       skip=ml_skip[j], w_down=ml_w_down[j].astype(BF16))
            new_ctx, h_lat = mlstm_mixer_block(h_ctx, h_lat, m_ctx_b[:, 3:6], m_lat[:, 3:6], norm_g[l, 1], prm,
                                               with_ctx_out=not last)
            h_ctx = h_ctx if last else new_ctx
        h_ctx = h_ctx.reshape(1, bsz * n_ctx, d)

        h_lat = ffn_block(h_lat, m_lat[:, 6:9], norm_g[l, 2], ffn_w_in, ffn_w_out, widx=(l, 1),
                          final_g=final_g if last else None)
        if not last:
            h_ctx = ffn_block(h_ctx, m_ctx[:, 6:9], norm_g[l, 2], ffn_w_in, ffn_w_out, widx=(l, 1))
    return h_lat
```

```python
import functools
import math

import jax
import jax.numpy as jnp
import numpy as np
from jax import lax
from jax.experimental import pallas as pl
from jax.experimental.pallas import tpu as pltpu

F32 = jnp.float32
BF16 = jnp.bfloat16

NORM_EPS = 1e-6
GRID_W = 64
POS_BASE = 10000.0
N_MOD = 9
HY_EMB = 33
HY_BANDS = (HY_EMB - 1) // 2
ML_HEADS = 4
ML_NORM_EPS = 1e-5
ML_QKV_GROUP = 256
SCAN_CHUNK = 256
HALO = 8
CONV_COLS = 256
FFN_COLS = 256

V7X_VMEM_LIMIT = 56 * 1024 * 1024


def _cparams(*sem):
    return pltpu.CompilerParams(dimension_semantics=sem, vmem_limit_bytes=V7X_VMEM_LIMIT)


def _bdot(a, b):
    return jnp.dot(a.astype(BF16), b.astype(BF16), preferred_element_type=F32)


def _rms_mod(h, g, shift, scale):
    y = h * lax.rsqrt(jnp.mean(h * h, axis=-1, keepdims=True) + NORM_EPS)
    return (y * g) * (1.0 + scale) + shift


def _silu(x):
    return x * (1.0 / (1.0 + jnp.exp(-x)))


def _tile(n, pref):
    if n <= pref:
        return n
    for t in range(pref, 7, -1):
        if n % t == 0 and t % 8 == 0:
            return t
    return n


def _mod_kernel(c_ref, w_ref, b_ref, o_ref):
    o_ref[...] = _bdot(_silu(c_ref[...]), w_ref[...]) + b_ref[...]


def ada_modulation(cond, ada_w, ada_b):
    n_layers, d, n_out = ada_w.shape
    r = cond.shape[0]
    tn = _tile(n_out, 2304) if n_out % 128 == 0 else n_out
    return pl.pallas_call(
        _mod_kernel,
        grid=(n_layers, n_out // tn),
        in_specs=[
            pl.BlockSpec((r, d), lambda l, j: (0, 0)),
            pl.BlockSpec((None, d, tn), lambda l, j: (l, 0, j)),
            pl.BlockSpec((None, 1, tn), lambda l, j: (l, 0, j)),
        ],
        out_specs=pl.BlockSpec((None, r, tn), lambda l, j: (l, 0, j)),
        out_shape=jax.ShapeDtypeStruct((n_layers, r, n_out), F32),
        compiler_params=_cparams("parallel", "parallel"),
        name="ada_modulation",
    )(cond, ada_w, ada_b.reshape(n_layers, 1, n_out))


def _ffn_kernel(*refs, has_pos, final):
    it = iter(refs)
    h_ref = next(it)
    pos_ref = next(it) if has_pos else None
    mod_ref, g_ref, wg_ref, wv_ref, wo_ref = next(it), next(it), next(it), next(it), next(it)
    fg_ref = next(it) if final else None
    o_ref = next(it)
    h = h_ref[...]
    if has_pos:
        h = h + pos_ref[...]
    u = _rms_mod(h, g_ref[...], mod_ref[0:1, :], mod_ref[1:2, :]).astype(BF16)
    ff = wo_ref.shape[0]
    step = FFN_COLS if ff % FFN_COLS == 0 else ff
    acc = None
    for lo in range(0, ff, step):
        cs = slice(lo, lo + step)
        gate = jnp.dot(u, wg_ref[:, cs], preferred_element_type=F32)
        val = jnp.dot(u, wv_ref[:, cs], preferred_element_type=F32)
        part = _bdot(_silu(gate) * val, wo_ref[cs, :])
        acc = part if acc is None else acc + part
    out = h + (0.5 * mod_ref[2:3, :]) * acc
    if final:
        out = out * lax.rsqrt(jnp.mean(out * out, axis=-1, keepdims=True) + NORM_EPS) * fg_ref[...]
    o_ref[...] = out


def _resident(shape, index_map):
    return pl.BlockSpec(shape, index_map, pipeline_mode=pl.Buffered(1))


def ffn_block(h, mods, g, w_in, w_out, *, widx=(), pos=None, final_g=None, tm=1024):
    bsz, n, d = h.shape
    ff = w_out.shape[-2]
    has_pos, final = pos is not None, final_g is not None
    tm = _tile(n, tm // 2 if has_pos else tm)
    lead = (None,) * len(widx)
    in_specs = [pl.BlockSpec((None, tm, d), lambda b, i: (b, i, 0))]
    args = [h]
    if has_pos:
        in_specs.append(pl.BlockSpec((tm, d), lambda b, i: (i, 0)))
        args.append(pos)
    in_specs += [
        pl.BlockSpec((None, 3, d), lambda b, i: (b, 0, 0)),
        pl.BlockSpec((1, d), lambda b, i: (0, 0)),
        _resident(lead + (d, ff), lambda b, i: widx + (0, 0)),
        _resident(lead + (d, ff), lambda b, i: widx + (0, 1)),
        _resident(lead + (ff, d), lambda b, i: widx + (0, 0)),
    ]
    args += [mods, g.reshape(1, d), w_in, w_in, w_out]
    if final:
        in_specs.append(pl.BlockSpec((1, d), lambda b, i: (0, 0)))
        args.append(final_g.reshape(1, d))
    return pl.pallas_call(
        functools.partial(_ffn_kernel, has_pos=has_pos, final=final),
        grid=(bsz, n // tm),
        in_specs=in_specs,
        out_specs=pl.BlockSpec((None, tm, d), lambda b, i: (b, i, 0)),
        out_shape=jax.ShapeDtypeStruct((bsz, n, d), F32),
        compiler_params=_cparams("parallel", "parallel"),
        name="ffn_block",
    )(*args)


def _linres_kernel(y_ref, h_ref, gate_ref, w_ref, b_ref, o_ref):
    for r, y in enumerate(_unpack_c(y_ref[...])):
        o_ref[r] = h_ref[r] + gate_ref[r] * (_bdot(y, w_ref[...]) + b_ref[...])


def linear_residual(y, h, gate, w, b, *, tm=512):
    npair, n, kdim = y.shape
    d = h.shape[-1]
    tm = _tile(n, tm)
    return pl.pallas_call(
        _linres_kernel,
        grid=(npair, n // tm),
        in_specs=[
            pl.BlockSpec((None, tm, kdim), lambda q, i: (q, i, 0)),
            pl.BlockSpec((2, tm, d), lambda q, i: (q, i, 0)),
            pl.BlockSpec((2, 1, d), lambda q, i: (q, 0, 0)),
            _resident((kdim, d), lambda q, i: (0, 0)),
            pl.BlockSpec((1, d), lambda q, i: (0, 0)),
        ],
        out_specs=pl.BlockSpec((2, tm, d), lambda q, i: (q, i, 0)),
        out_shape=jax.ShapeDtypeStruct(h.shape, F32),
        compiler_params=_cparams("parallel", "parallel"),
        name="linear_residual",
    )(y, h, gate, w, b.reshape(1, d))


def _halo_specs(n, tm, d, nbatch=None):
    hb, last = tm // HALO, n // HALO - 1
    return [
        pl.BlockSpec((nbatch, HALO, d), lambda b, i: (b, jnp.maximum(i * hb - 1, 0), 0)),
        pl.BlockSpec((nbatch, tm, d), lambda b, i: (b, i, 0)),
        pl.BlockSpec((nbatch, HALO, d), lambda b, i: (b, jnp.minimum((i + 1) * hb, last), 0)),
    ]


def _mask_seq_ends(p, tm):
    i, ni = pl.program_id(1), pl.num_programs(1)
    head = jnp.where(i == 0, 0.0, p[:HALO])
    tail = jnp.where(i == ni - 1, 0.0, p[tm + HALO:])
    return jnp.concatenate([head, p[HALO:tm + HALO], tail], axis=0)


def _conv3(p, cw_ref, tm):
    rows = p.shape[0]
    prev = pltpu.roll(p, 1, 0)[HALO:HALO + tm]
    nxt = pltpu.roll(p, rows - 1, 0)[HALO:HALO + tm]
    return prev * cw_ref[0:1, :] + p[HALO:HALO + tm] * cw_ref[1:2, :] + nxt * cw_ref[2:3, :]


def _hyena_in_kernel(hp_ref, h_ref, hn_ref, mod_ref, g_ref, w_ref, b_ref, cw_ref, cb_ref, o_ref):
    tm = h_ref.shape[1]
    us = []
    for r in range(2):
        rows = jnp.concatenate([hp_ref[r], h_ref[r], hn_ref[r]], axis=0)
        us.append(_rms_mod(rows, g_ref[...], mod_ref[r, 0:1, :], mod_ref[r, 1:2, :]).astype(BF16))
    nout = o_ref.shape[-1]
    step = CONV_COLS if nout % CONV_COLS == 0 else nout
    for lo in range(0, nout, step):
        cs = slice(lo, lo + step)
        outs = []
        for u in us:
            p = _mask_seq_ends(jnp.dot(u, w_ref[:, cs], preferred_element_type=F32) + b_ref[:, cs], tm)
            outs.append(_conv3(p, cw_ref.at[:, cs], tm) + cb_ref[:, cs])
        o_ref[:, cs] = _pack_c(outs[0], outs[1])


def hyena_in_proj(h, mods, g, w, b, conv_w, conv_b, *, tm=512):
    bsz, n, d = h.shape
    nout = w.shape[1]
    tm = _tile(n, tm)
    const = lambda shape: pl.BlockSpec(shape, lambda b_, i: (0,) * len(shape))
    return pl.pallas_call(
        _hyena_in_kernel,
        grid=(bsz // 2, n // tm),
        in_specs=_halo_specs(n, tm, d, 2) + [
            pl.BlockSpec((2, 3, d), lambda b_, i: (b_, 0, 0)),
            const((1, d)), _resident((d, nout), lambda b_, i: (0, 0)), const((1, nout)), const((3, nout)),
            const((1, nout)),
        ],
        out_specs=pl.BlockSpec((None, tm, nout), lambda b_, i: (b_, i, 0)),
        out_shape=jax.ShapeDtypeStruct((bsz // 2, n, nout), jnp.uint32),
        compiler_params=_cparams("parallel", "parallel"),
        name="hyena_in_proj",
    )(h, h, h, mods, g.reshape(1, d), w, b.reshape(1, nout), conv_w, conv_b.reshape(1, nout))


def _hdot(a, b):
    return jnp.dot(a, b, preferred_element_type=F32, precision=lax.Precision.HIGHEST)


def _dot3(a, b):
    ah, bh = a.astype(BF16), b.astype(BF16)
    al, bl = (a - ah.astype(F32)).astype(BF16), (b - bh.astype(F32)).astype(BF16)
    dot = functools.partial(jnp.dot, preferred_element_type=F32)
    return dot(ah, bh) + (dot(al, bh) + dot(ah, bl))


def _filter_kernel(ft_ref, w1_ref, b1_ref, w2_ref, b2_ref, w3_ref, b3_ref, fr_ref, ld_ref, k_ref, ss_ref):
    half, i = pl.program_id(0), pl.program_id(1)
    tt = ft_ref.shape[0]
    c = k_ref.shape[-1]
    ft = ft_ref[...]
    h = jnp.sin(fr_ref[0:1, :] * (_hdot(ft, w1_ref[...]) + b1_ref[...]))
    h = jnp.sin(fr_ref[1:2, :] * (_hdot(h, w2_ref[...]) + b2_ref[...]))
    no_lag = (lax.broadcasted_iota(jnp.int32, (tt, c), 0) + (1 - half) + i) == 0

    @pl.when((half == 0) & (i == 0))
    def _():
        ss_ref[...] = jnp.zeros_like(ss_ref)

    ks = []
    for o in range(ss_ref.shape[0]):
        k = (_dot3(h, w3_ref[o]) + b3_ref[o]) * jnp.exp(-ft[:, 0:1] * jnp.exp(ld_ref[o]))
        k = jnp.where(no_lag, 0.0, k)
        ks.append(k)
        ss_ref[o] += jnp.broadcast_to(jnp.sum(k * k, axis=0, keepdims=True), ss_ref.shape[1:])
    k_ref[...] = _pack_c(*ks)


def hyena_filters_time(n, f_w1, f_b1, f_w2, f_b2, f_w3, f_b3, f_freq, log_decay, *, tt=512):
    order, _, d = log_decay.shape
    assert order == 2, "the packed filter word holds exactly two orders"
    fh = f_w2.shape[0]
    emb_pad = 64
    lag = np.arange(2 * n)
    lag = np.where(lag < n, lag, 2 * n - lag).astype(np.float32)
    t_norm = lag / np.float32(max(n - 1, 1))
    bands = np.linspace(1e-4, HY_BANDS - 1, HY_BANDS, dtype=np.float32)
    ang = (np.float32(2.0 * math.pi / n) * lag)[:, None] * bands[None, :]
    feats = np.zeros((2 * n, emb_pad), np.float32)
    feats[:, :HY_EMB] = np.concatenate([t_norm[:, None], np.cos(ang), -np.sin(ang)], axis=-1)
    w1 = jnp.zeros((emb_pad, fh), F32).at[:HY_EMB].set(f_w1)
    tt = _tile(n, tt)
    nt = n // tt
    const = lambda shape: pl.BlockSpec(shape, lambda hf, i: (0,) * len(shape))
    return pl.pallas_call(
        _filter_kernel,
        grid=(2, nt),
        in_specs=[
            pl.BlockSpec((tt, emb_pad), lambda hf, i: (hf * nt + i, 0)),
            const((emb_pad, fh)), const((1, fh)), const((fh, fh)), const((1, fh)),
            pl.BlockSpec((None, order, fh, d), lambda hf, i: (hf, 0, 0, 0)),
            pl.BlockSpec((order, None, 1, d), lambda hf, i: (0, hf, 0, 0)),
            const((2, fh)),
            pl.BlockSpec((order, None, 1, d), lambda hf, i: (0, hf, 0, 0)),
        ],
        out_specs=[
            pl.BlockSpec((tt, d), lambda hf, i: (hf * nt + i, 0)),
            pl.BlockSpec((order, 8, d), lambda hf, i: (0, 0, 0)),
        ],
        out_shape=[jax.ShapeDtypeStruct((2 * n, d), jnp.uint32), jax.ShapeDtypeStruct((order, 8, d), F32)],
        compiler_params=_cparams("arbitrary", "arbitrary"),
        name="hyena_filter_mlp",
    )(jnp.asarray(feats), w1, f_b1.reshape(1, fh), f_w2, f_b2.reshape(1, fh),
      jnp.transpose(f_w3.reshape(fh, order, 2, d), (2, 1, 0, 3)),
      f_b3.reshape(order, 2, 1, d), f_freq, log_decay.reshape(order, 2, 1, d))


def _fft_dims(n):
    nb = 1 << int(math.floor(math.log2(math.sqrt(2 * n))))
    na = 2 * n // nb
    assert na * nb == 2 * n and na % 16 == 0 and nb % 8 == 0, (n, na, nb)
    return na, nb


@functools.lru_cache(maxsize=None)
def _fft_tables(n):
    na, nb = _fft_dims(n)
    na2, nn = na // 2, 2 * n
    ka = np.arange(na, dtype=np.int64)[None, :, None]
    a = np.arange(na2, dtype=np.int64)[None, None, :]
    b = np.arange(nb, dtype=np.int64)[:, None, None]
    ang = (2.0 * np.pi / nn) * ((ka * (a * nb + b)) % nn)
    mr, mi = np.cos(ang), -np.sin(ang)
    f1 = np.concatenate([np.concatenate([mr, -mi], 2), np.concatenate([mi, mr], 2)], 1)
    mrt, mit = np.swapaxes(mr, 1, 2) / nn, np.swapaxes(mi, 1, 2) / nn
    g1 = np.concatenate([np.concatenate([mrt, mit], 2), np.concatenate([-mit, mrt], 2)], 1)
    kb = np.arange(nb, dtype=np.int64)
    ang2 = (2.0 * np.pi / nb) * ((kb[:, None] * kb[None, :]) % nb)
    er, ei = np.cos(ang2), -np.sin(ang2)
    f2 = np.block([[er, -ei], [ei, er]])
    g2 = np.block([[er, ei], [-ei, er]])
    a_all = np.arange(na, dtype=np.int64)[None, None, :]
    ang_f = (2.0 * np.pi / nn) * ((ka * (a_all * nb + b)) % nn)
    f1_real = np.concatenate([np.cos(ang_f), -np.sin(ang_f)], 1)
    as_bf16 = lambda x: x.astype(np.float32).astype(BF16)
    return dict(f1=as_bf16(f1), f1_real=as_bf16(f1_real), g1=as_bf16(g1), f2=as_bf16(f2), g2=as_bf16(g2))


def _pack_c(re, im):
    rb = lax.bitcast_convert_type(re.astype(BF16).astype(F32), jnp.uint32)
    ib = lax.bitcast_convert_type(im.astype(BF16).astype(F32), jnp.uint32)
    return rb | lax.shift_right_logical(ib, jnp.uint32(16))


def _unpack_c(w):
    re = lax.bitcast_convert_type(w & jnp.uint32(0xFFFF0000), F32)
    im = lax.bitcast_convert_type(lax.shift_left(w, jnp.uint32(16)), F32)
    return re, im


def _stack_bf16(re, im):
    return jnp.concatenate([re, im], axis=0).astype(BF16)


LANES = 128
SUB = 8
SHORT_SEQ = 1024


def _lane_cat(parts):
    return parts[0] if len(parts) == 1 else jnp.concatenate(parts, axis=1)


def _flat_rows(ref):
    return ref.reshape(math.prod(ref.shape[:-1]), ref.shape[-1])


def _fft_conv_kernel(x_ref, f1_ref, k_ref, f2_ref, g2_ref, g1_ref, z_ref, m_ref, skip_ref, o_ref, w_ref, *, n1, n2):
    t = pl.program_id(1)
    npair, na2, jb, _ = x_ref.shape
    na, kblk, nb = 2 * na2, k_ref.shape[1], k_ref.shape[2]
    pitch = w_ref.shape[1] // na
    x2, z2, m2, o2 = _flat_rows(x_ref), _flat_rows(z_ref), _flat_rows(m_ref), _flat_rows(o_ref)
    col = lambda q, j: pl.ds(q * na2 * jb + j, na2, stride=jb)
    lane = lambda v, q: v[:, q * LANES:(q + 1) * LANES]

    @pl.when(t < n1)
    def _():
        for j in range(jb):
            xs = _lane_cat([_stack_bf16(*_unpack_c(x2[col(q, j), :])) for q in range(npair)])
            res = jnp.dot(f1_ref[t * jb + j], xs, preferred_element_type=F32)
            packed = _pack_c(res[:na], res[na:])
            for q in range(npair):
                w_ref[q, pl.ds(t * jb + j, na, stride=pitch), :] = lane(packed, q)

    @pl.when((t >= n1) & (t < n1 + n2))
    def _():
        for k in range(kblk):
            rows = pl.ds(pl.multiple_of(((t - n1) * kblk + k) * pitch, SUB), nb)
            ar, ai = _unpack_c(_lane_cat([w_ref[q, rows, :] for q in range(npair)]))
            x = jnp.dot(f2_ref[...], _stack_bf16(ar, ai), preferred_element_type=F32)
            xr, xi = x[:nb], x[nb:]
            kr = _lane_cat([k_ref[0, k].astype(F32)] * npair)
            ki = _lane_cat([k_ref[1, k].astype(F32)] * npair)
            bv = jnp.dot(g2_ref[...], _stack_bf16(xr * kr - xi * ki, xr * ki + xi * kr), preferred_element_type=F32)
            packed = _pack_c(bv[:nb], bv[nb:])
            for q in range(npair):
                w_ref[q, rows, :] = lane(packed, q)

    @pl.when(t >= n1 + n2)
    def _():
        bb = t - (n1 + n2)
        skip = skip_ref[...]
        for j in range(jb):
            br, bi = _unpack_c(_lane_cat([w_ref[q, pl.ds(bb * jb + j, na, stride=pitch), :] for q in range(npair)]))
            y = jnp.dot(g1_ref[bb * jb + j], _stack_bf16(br, bi), preferred_element_type=F32)
            for q in range(npair):
                rows = col(q, j)
                (zr, zi), (mr, mi) = _unpack_c(z2[rows, :]), _unpack_c(m2[rows, :])
                o2[rows, :] = _pack_c(mr * (lane(y[:na2], q) + skip * zr), mi * (lane(y[na2:], q) + skip * zi))


def fft_long_conv(x, x_off, kf, order, z, z_off, m, m_off, skip, n, c, tables, *, jb=16, kblk=32):
    na, nb = _fft_dims(n)
    na2 = na // 2
    jb, kblk = (nb, na) if na * nb <= SHORT_SEQ else (min(jb, nb), min(kblk, na))
    n1, n2 = nb // jb, na // kblk
    npair = x.shape[0]
    view = lambda a: a.reshape(a.shape[0], na2, nb, a.shape[-1])
    xo, zo, mo = x_off // LANES, z_off // LANES, m_off // LANES
    col1 = lambda t: jnp.minimum(t, n1 - 1)
    col3 = lambda t: jnp.clip(t - (n1 + n2), 0, n1 - 1)
    seq = lambda off, col: pl.BlockSpec((npair, na2, jb, LANES), lambda ci, t: (0, 0, col(t), ci + off))
    const = lambda shape: _resident(shape, lambda ci, t: (0,) * len(shape))
    out = pl.pallas_call(
        functools.partial(_fft_conv_kernel, n1=n1, n2=n2),
        grid=(c // LANES, n1 + n2 + n1),
        in_specs=[
            seq(xo, col1),
            const((nb, 2 * na, na)),
            pl.BlockSpec((None, 2, kblk, nb, LANES), lambda ci, t: (order, 0, jnp.clip(t - n1, 0, n2 - 1), 0, ci)),
            const((2 * nb, 2 * nb)), const((2 * nb, 2 * nb)),
            const((nb, na, 2 * na)),
            seq(zo, col3), seq(mo, col3),
            pl.BlockSpec((1, LANES), lambda ci, t: (0, ci)),
        ],
        out_specs=seq(0, col3),
        out_shape=jax.ShapeDtypeStruct((npair, na2, nb, c), jnp.uint32),
        scratch_shapes=[pltpu.VMEM((npair, na * (nb + SUB), LANES), jnp.uint32)],
        compiler_params=_cparams("parallel", "arbitrary"),
        name="fft_long_conv",
    )(view(x), tables["f1"], kf, tables["f2"], tables["g2"], tables["g1"], view(z), view(m), skip.reshape(1, c))
    return out.reshape(npair, n, c)


def _fft_filter_kernel(x_ref, f1_ref, f2_ref, ss_ref, o_ref, w_ref, *, n1):
    t = pl.program_id(1)
    _, na, jb, _ = x_ref.shape
    order, kblk, nb = o_ref.shape[0], o_ref.shape[2], o_ref.shape[3]
    pitch = w_ref.shape[1] // na
    x2 = _flat_rows(x_ref)
    lane = lambda v, q: v[:, q * LANES:(q + 1) * LANES]

    @pl.when(t < n1)
    def _():
        for j in range(jb):
            xs = _lane_cat(list(_unpack_c(x2[pl.ds(j, na, stride=jb), :])))
            res = jnp.dot(f1_ref[t * jb + j], xs.astype(BF16), preferred_element_type=F32)
            packed = _pack_c(res[:na], res[na:])
            for o in range(order):
                w_ref[o, pl.ds(t * jb + j, na, stride=pitch), :] = lane(packed, o)

    @pl.when(t >= n1)
    def _():
        scale = _lane_cat([lax.rsqrt(ss_ref[o, 0:1, :] + 1e-12) for o in range(order)])
        for k in range(kblk):
            rows = pl.ds(pl.multiple_of(((t - n1) * kblk + k) * pitch, SUB), nb)
            ar, ai = _unpack_c(_lane_cat([w_ref[o, rows, :] for o in range(order)]))
            x = jnp.dot(f2_ref[...], _stack_bf16(ar, ai), preferred_element_type=F32) * scale
            for o in range(order):
                o_ref[o, 0, k] = lane(x[:nb], o).astype(o_ref.dtype)
                o_ref[o, 1, k] = lane(x[nb:], o).astype(o_ref.dtype)


def fft_filter_spectrum(kt, ss, n, tables, *, jb=16, kblk=32):
    na, nb = _fft_dims(n)
    order, c = ss.shape[0], kt.shape[-1]
    jb, kblk = (nb, na) if na * nb <= SHORT_SEQ else (min(jb, nb), min(kblk, na))
    n1, n2 = nb // jb, na // kblk
    return pl.pallas_call(
        functools.partial(_fft_filter_kernel, n1=n1),
        grid=(c // LANES, n1 + n2),
        in_specs=[
            pl.BlockSpec((1, na, jb, LANES), lambda ci, t: (0, 0, jnp.minimum(t, n1 - 1), ci)),
            _resident((nb, 2 * na, na), lambda ci, t: (0, 0, 0)),
            _resident((2 * nb, 2 * nb), lambda ci, t: (0, 0)),
            pl.BlockSpec((order, 8, LANES), lambda ci, t: (0, 0, ci)),
        ],
        out_specs=pl.BlockSpec((order, 2, kblk, nb, LANES), lambda ci, t: (0, 0, jnp.maximum(t - n1, 0), 0, ci)),
        out_shape=jax.ShapeDtypeStruct((order, 2, na, nb, c), BF16),
        scratch_shapes=[pltpu.VMEM((order, na * (nb + SUB), LANES), jnp.uint32)],
        compiler_params=_cparams("parallel", "arbitrary"),
        name="fft_filter_spectrum",
    )(kt.reshape(1, na, nb, c), tables["f1_real"], tables["f2"], ss)


def hyena_mixer_block(h, mods, g, prm):
    bsz, n, d = h.shape
    assert bsz % 2 == 0 and d % LANES == 0, "batch rows travel in pairs, channels in LANES-wide columns"
    tables = _fft_tables(n)
    proj = hyena_in_proj(h, mods, g, prm["w_in"], prm["b_in"], prm["conv_w"], prm["conv_b"])
    kt, ss = hyena_filters_time(n, prm["f_w1"], prm["f_b1"], prm["f_w2"], prm["f_b2"], prm["f_w3"],
                                prm["f_b3"], prm["f_freq"], prm["log_decay"])
    kf = fft_filter_spectrum(kt, ss, n, tables)
    z = fft_long_conv(proj, 2 * d, kf, 0, proj, 2 * d, proj, 0, prm["skip"][0], n, d, tables)
    y = fft_long_conv(z, 0, kf, 1, z, 0, proj, d, prm["skip"][1], n, d, tables)
    return linear_residual(y, h, mods[:, 2:3, :], prm["w_out"], prm["b_out"])


def _mlstm_pre_kernel(hp_ref, h_ref, hn_ref, mod_ref, g_ref, wup_ref, cw_ref, cb_ref, wq_ref, wk_ref, wkt_ref, wv_ref,
                      wg_ref, bg_ref, q_ref, kt_ref, v_ref, xc_ref, z_ref, gt_ref, *, k_scale):
    tm = h_ref.shape[0]
    inner = q_ref.shape[-1]
    grp = wq_ref.shape[-1]
    rows = jnp.concatenate([hp_ref[...], h_ref[...], hn_ref[...]], axis=0)
    u = _rms_mod(rows, g_ref[...], mod_ref[0:1, :], mod_ref[1:2, :])
    up = _bdot(u, wup_ref[...])
    xm_h = _mask_seq_ends(up[:, :inner], tm)
    z_ref[...] = up[HALO:HALO + tm, inner:].astype(z_ref.dtype)
    xc_f = _silu(_conv3(xm_h, cw_ref, tm) + cb_ref[...])
    xc_ref[...] = xc_f.astype(xc_ref.dtype)
    xc_all, xm_all = xc_f.astype(BF16), xm_h[HALO:HALO + tm].astype(BF16)
    gates = jnp.zeros(gt_ref.shape, F32) + bg_ref[...]
    for gi in range(inner // grp):
        lo, hi = gi * grp, (gi + 1) * grp
        xc, xm = xc_all[:, lo:hi], xm_all[:, lo:hi]
        q = jnp.dot(xc, wq_ref[gi], preferred_element_type=F32)
        k = jnp.dot(xc, wk_ref[gi], preferred_element_type=F32)
        v = jnp.dot(xm, wv_ref[gi], preferred_element_type=F32)
        q_ref[:, lo:hi] = q.astype(q_ref.dtype)
        kt_ref[lo:hi, :] = (_nt(wkt_ref[gi], xc) * k_scale).astype(kt_ref.dtype)
        v_ref[:, lo:hi] = v.astype(v_ref.dtype)
        gates += (_bdot(q, wg_ref[lo:hi, :]) + _bdot(k, wg_ref[inner + lo:inner + hi, :])
                  + _bdot(v, wg_ref[2 * inner + lo:2 * inner + hi, :]))
    gt_ref[...] = gates


def mlstm_pre(h, mods, g, prm, *, tm=512):
    bsz, n, d = h.shape
    inner = prm["w_up"].shape[1] // 2
    ngate = prm["w_gate"].shape[1]
    grp = prm["wq_dense"].shape[-1]
    tm = _tile(n, tm)
    const = lambda shape: pl.BlockSpec(shape, lambda b_, i: (0,) * len(shape))
    row_spec = lambda w: pl.BlockSpec((None, tm, w), lambda b_, i: (b_, i, 0))
    mm = jax.ShapeDtypeStruct((bsz, n, inner), BF16)
    return pl.pallas_call(
        functools.partial(_mlstm_pre_kernel, k_scale=float((inner // ML_HEADS) ** -0.5)),
        grid=(bsz, n // tm),
        in_specs=_halo_specs(n, tm, d) + [
            pl.BlockSpec((None, 3, d), lambda b_, i: (b_, 0, 0)),
            const((1, d)), _resident((d, 2 * inner), lambda b_, i: (0, 0)), const((3, inner)), const((1, inner)),
            const((inner // grp, grp, grp)), const((inner // grp, grp, grp)), const((inner // grp, grp, grp)),
            const((inner // grp, grp, grp)), const((3 * inner, ngate)), const((1, ngate)),
        ],
        out_specs=[row_spec(inner), pl.BlockSpec((None, inner, tm), lambda b_, i: (b_, 0, i))]
                  + [row_spec(inner)] * 3 + [row_spec(ngate)],
        out_shape=[mm, jax.ShapeDtypeStruct((bsz, inner, n), BF16), mm, mm, mm,
                   jax.ShapeDtypeStruct((bsz, n, ngate), F32)],
        compiler_params=_cparams("parallel", "parallel"),
        name="mlstm_pre",
    )(h, h, h, mods, g.reshape(1, d), prm["w_up"], prm["conv_w"], prm["conv_b"].reshape(1, inner),
      prm["wq_dense"], prm["wk_dense"], prm["wkt_dense"], prm["wv_dense"], prm["w_gate"],
      prm["b_gate"].reshape(1, ngate))


def _dense_blockdiag(w, grp):
    nb, bs, _ = w.shape
    per = grp // bs
    rows = jnp.tile(w.reshape(nb // per, grp, bs), (1, 1, per))
    blk = np.arange(grp) // bs
    return (rows * jnp.asarray(blk[:, None] == blk[None, :], w.dtype)).astype(BF16)


def _split3(x):
    p1 = x.astype(BF16)
    r1 = x - p1.astype(F32)
    p2 = r1.astype(BF16)
    p3 = (r1 - p2.astype(F32)).astype(BF16)
    return p1, p2, p3


def _nt(a, b):
    return lax.dot_general(a, b, (((1,), (1,)), ((), ())), preferred_element_type=F32)


def _scan_kernel(*refs, has_init, emit_state, heads_per_step):
    it = iter(refs)
    q_ref, kt_ref, v_ref, gt_ref = next(it), next(it), next(it), next(it)
    allow_ref, allow_t_ref, neg_ref = next(it), next(it), next(it)
    init_refs = [next(it) for _ in range(3)] if has_init else None
    h_ref = next(it)
    final_refs = [next(it) for _ in range(3)] if emit_state else None
    state = c_sc, n_sc, m_sc = next(it), next(it), next(it)
    c = pl.program_id(3)
    nrow, ch = q_ref.shape[0], q_ref.shape[1]
    dh = q_ref.shape[2] // heads_per_step

    @pl.when(c == 0)
    def _():
        for sc, src in zip(state, init_refs or [None] * 3):
            sc[...] = jnp.zeros_like(sc) if src is None else src[...]

    allow, allow_t, neg = allow_ref[...], allow_t_ref[...], neg_ref[...]
    ones_rows = jnp.ones((SUB, ch), BF16)

    streams = [(r, hh) for r in range(nrow) for hh in range(heads_per_step)]
    cols = [slice(hh * dh, (hh + 1) * dh) for hh in range(heads_per_step)]

    def gate_terms(st):
        i_row = gt_ref[st[0], st[1], 0:1, :]
        f_row = gt_ref[st[0], st[1], 1:2, :]
        lf_row = jnp.minimum(f_row, 0.0) - jnp.log1p(jnp.exp(-jnp.abs(f_row)))
        pieces = jnp.concatenate(list(_split3(lf_row)) + [jnp.zeros((SUB - 3, ch), BF16)], axis=0)
        cum_col = _nt(allow, pieces)
        cum_row = jnp.dot(pieces, allow_t, preferred_element_type=F32)
        b_col = cum_col[:, 0:1] + cum_col[:, 1:2] + cum_col[:, 2:3]
        b_row = cum_row[0:1, :] + cum_row[1:2, :] + cum_row[2:3, :]
        total = jnp.sum(lf_row, axis=1, keepdims=True)
        e_row = i_row - b_row
        m_st = m_sc[st[0], st[1], 0:1, 0:1]
        dmat = (b_col + e_row) + neg
        a_col = b_col + m_st
        m_j = jnp.maximum(a_col, jnp.max(dmat, axis=1, keepdims=True))
        g_row = total + e_row
        m_new = jnp.maximum(total + m_st, jnp.max(g_row, axis=1, keepdims=True))
        return dict(w_inter=jnp.exp(a_col - m_j), dexp=jnp.exp(dmat - m_j), floor=jnp.exp(-m_j), m_new=m_new,
                    decay=jnp.exp(total + m_st - m_new), w_row=jnp.exp(g_row - m_new))

    def read_out(st, t):
        r, cs = st[0], cols[st[1]]
        q, kt, v = q_ref[r, :, cs], kt_ref[r, cs, :], v_ref[r, :, cs]
        s = jnp.dot(q, kt, preferred_element_type=F32) * t["dexp"]
        num = t["w_inter"] * jnp.dot(q, c_sc[st].astype(BF16), preferred_element_type=F32) + _bdot(s, v)
        qn = (t["w_inter"] * jnp.sum(q.astype(F32) * n_sc[st[0], st[1], 0:1, :], axis=1, keepdims=True)
              + jnp.sum(s, axis=1, keepdims=True))
        h_ref[r, :, cs] = (num / jnp.maximum(jnp.abs(qn), t["floor"])).astype(h_ref.dtype)

    def update(st, t):
        r, cs = st[0], cols[st[1]]
        kwt = (kt_ref[r, cs, :].astype(F32) * t["w_row"]).astype(BF16)
        c_sc[st] = t["decay"] * c_sc[st] + jnp.dot(kwt, v_ref[r, :, cs], preferred_element_type=F32)
        n_sc[st] = t["decay"] * n_sc[st] + _nt(ones_rows, kwt)
        m_sc[st] = jnp.broadcast_to(t["m_new"], m_sc.shape[2:])

    terms = [gate_terms(st) for st in streams]
    for st, t in zip(streams, terms):
        read_out(st, t)
        update(st, t)

    if emit_state:
        @pl.when(c == pl.num_programs(3) - 1)
        def _():
            for dst, sc in zip(final_refs, state):
                dst[...] = sc[...]


@functools.lru_cache(maxsize=None)
def _scan_masks(ch):
    r = np.arange(ch)
    causal = (r[None, :] <= r[:, None]).astype(np.float32)
    allow = np.stack([causal, causal.T])
    neg = np.where(allow > 0, 0.0, -np.inf).astype(np.float32)
    return allow.astype(BF16), np.transpose(allow, (0, 2, 1)).astype(BF16), neg


def mlstm_scan(q, kt, v, gates_t, init=None, *, emit_state=False, rows_per_step=2):
    bsz, n, inner = q.shape
    nh = gates_t.shape[2]
    dh = inner // nh
    hp = nh if nh <= 4 else (2 if nh % 2 == 0 else 1)
    nr = rows_per_step if bsz % rows_per_step == 0 else 1
    ch = min(SCAN_CHUNK, n)
    nc = n // ch
    ceff = lambda d, c: c + d * (nc - 1 - 2 * c)
    seq = pl.BlockSpec((nr, ch, hp * dh), lambda d, b, hd, c: (b, ceff(d, c), hd))
    state_shapes = [(dh, dh), (SUB, dh), (SUB, 128)]
    st_specs = [pl.BlockSpec((None, nr, hp) + s, lambda d, b, hd, c: (d, b, hd, 0, 0)) for s in state_shapes]
    by_dir = pl.BlockSpec((None, ch, ch), lambda d, b, hd, c: (d, 0, 0))
    in_specs = [seq, pl.BlockSpec((nr, hp * dh, ch), lambda d, b, hd, c: (b, hd, ceff(d, c))), seq,
                pl.BlockSpec((nr, None, hp, 2, ch), lambda d, b, hd, c: (b, d, hd, 0, ceff(d, c))),
                by_dir, by_dir, by_dir]
    args = [q, kt, v, gates_t, *_scan_masks(ch)]
    if init is not None:
        in_specs += st_specs
        args += list(init)
    out_specs = [pl.BlockSpec((None, nr, ch, hp * dh), lambda d, b, hd, c: (d, b, ceff(d, c), hd))]
    out_shape = [jax.ShapeDtypeStruct((2, bsz, n, inner), BF16)]
    if emit_state:
        out_specs += st_specs
        out_shape += [jax.ShapeDtypeStruct((2, bsz, nh) + s, F32) for s in state_shapes]
    res = pl.pallas_call(
        functools.partial(_scan_kernel, has_init=init is not None, emit_state=emit_state, heads_per_step=hp),
        grid=(2, bsz // nr, nh // hp, nc),
        in_specs=in_specs,
        out_specs=out_specs,
        out_shape=out_shape,
        scratch_shapes=[pltpu.VMEM((nr, hp) + s, F32) for s in state_shapes],
        compiler_params=_cparams("parallel", "parallel", "parallel", "arbitrary"),
        name="mlstm_scan",
    )(*args)
    return (res[0], tuple(res[1:])) if emit_state else (res[0], None)


def _mlstm_post_kernel(hf_ref, hb_ref, xc_ref, z_ref, res_ref, gate_ref, ng_ref, sk_ref, wd_ref, o_ref):
    inner = xc_ref.shape[-1]
    dh = inner // ML_HEADS
    acc = jnp.zeros(o_ref.shape, F32)
    for hd in range(ML_HEADS):
        lo, hi = hd * dh, (hd + 1) * dh
        x = hf_ref[:, lo:hi].astype(F32) + hb_ref[:, lo:hi].astype(F32)
        xz = x - jnp.mean(x, axis=-1, keepdims=True)
        var = jnp.mean(xz * xz, axis=-1, keepdims=True)
        hn = xz * lax.rsqrt(var + ML_NORM_EPS) * ng_ref[:, lo:hi]
        t = (hn + sk_ref[:, lo:hi] * xc_ref[:, lo:hi].astype(F32)) * _silu(z_ref[:, lo:hi].astype(F32))
        acc += _bdot(t, wd_ref[lo:hi, :])
    o_ref[...] = res_ref[...] + gate_ref[...] * acc


def mlstm_post(hs, xc, z, h_res, gate, prm, *, tm=512):
    bsz, n, inner = xc.shape
    d = h_res.shape[-1]
    tm = _tile(n, tm)
    const = lambda shape: pl.BlockSpec(shape, lambda b_, i: (0,) * len(shape))
    return pl.pallas_call(
        _mlstm_post_kernel,
        grid=(bsz, n // tm),
        in_specs=[
            pl.BlockSpec((None, None, tm, inner), lambda b_, i: (0, b_, i, 0)),
            pl.BlockSpec((None, None, tm, inner), lambda b_, i: (1, b_, i, 0)),
            pl.BlockSpec((None, tm, inner), lambda b_, i: (b_, i, 0)),
            pl.BlockSpec((None, tm, inner), lambda b_, i: (b_, i, 0)),
            pl.BlockSpec((None, tm, d), lambda b_, i: (b_, i, 0)),
            pl.BlockSpec((None, 1, d), lambda b_, i: (b_, 0, 0)),
            const((1, inner)), const((1, inner)), _resident((inner, d), lambda b_, i: (0, 0)),
        ],
        out_specs=pl.BlockSpec((None, tm, d), lambda b_, i: (b_, i, 0)),
        out_shape=jax.ShapeDtypeStruct((bsz, n, d), F32),
        compiler_params=_cparams("parallel", "parallel"),
        name="mlstm_post",
    )(hs, hs, xc, z, h_res, gate, prm["norm_g"].reshape(1, inner), prm["skip"].reshape(1, inner), prm["w_down"])


def _gates_by_head(gates, nh):
    bsz, n, _ = gates.shape
    return jnp.transpose(gates.reshape(bsz, n, 2, 2, nh), (0, 2, 4, 3, 1))


def mlstm_mixer_block(h_ctx, h_lat, mods_ctx, mods_lat, g, prm, *, with_ctx_out):
    pc = mlstm_pre(h_ctx, mods_ctx, g, prm)
    pq = mlstm_pre(h_lat, mods_lat, g, prm)
    hs_ctx, state = mlstm_scan(pc[0], pc[1], pc[2], _gates_by_head(pc[5], ML_HEADS), emit_state=True)
    hs_lat, _ = mlstm_scan(pq[0], pq[1], pq[2], _gates_by_head(pq[5], ML_HEADS), init=state)
    out_lat = mlstm_post(hs_lat, pq[3], pq[4], h_lat, mods_lat[:, 2:3, :], prm)
    out_ctx = mlstm_post(hs_ctx, pc[3], pc[4], h_ctx, mods_ctx[:, 2:3, :], prm) if with_ctx_out else None
    return out_ctx, out_lat


def _pos_embed_2d(n_tokens, d):
    rows = n_tokens // GRID_W
    quarter = d // 4
    omega = 1.0 / (POS_BASE ** (jnp.arange(quarter, dtype=F32) / quarter))
    ang_r = jnp.arange(rows, dtype=F32)[:, None] * omega
    ang_c = jnp.arange(GRID_W, dtype=F32)[:, None] * omega
    row_emb = jnp.concatenate([jnp.sin(ang_r), jnp.cos(ang_r)], axis=-1)
    col_emb = jnp.concatenate([jnp.sin(ang_c), jnp.cos(ang_c)], axis=-1)
    emb = jnp.concatenate([
        jnp.broadcast_to(row_emb[:, None, :], (rows, GRID_W, d // 2)),
        jnp.broadcast_to(col_emb[None, :, :], (rows, GRID_W, d // 2))], axis=-1)
    return emb.reshape(rows * GRID_W, d)


def kernel(x, c, ctx, c_ctx, ada_w, ada_b, norm_g, final_g, ffn_w_in, ffn_w_out, hy_w_in, hy_b_in, hy_conv_w, hy_conv_b, hy_f_w1, hy_f_b1, hy_f_w2, hy_f_b2, hy_f_w3, hy_f_b3, hy_f_freq, hy_log_decay, hy_skip, hy_w_out, hy_b_out, ml_w_up, ml_conv_w, ml_conv_b, ml_w_q, ml_w_k, ml_w_v, ml_w_gate, ml_b_gate, ml_norm_g, ml_skip, ml_w_down):
    bsz, n_lat, d = x.shape
    n_ctx = ctx.shape[1]
    depth = ada_w.shape[0]
    n_mixers = 2

    cond_rows = 8
    cond = jnp.zeros((cond_rows, d), F32).at[:bsz].set(c).at[bsz].set(c_ctx)
    mods_all = ada_modulation(cond, ada_w, ada_b).reshape(depth, cond_rows, N_MOD, d)

    ffn_w_in = ffn_w_in.astype(BF16)
    ffn_w_out = ffn_w_out.astype(BF16)
    pos = _pos_embed_2d(n_lat, d)

    h_lat = x
    h_ctx = ctx.reshape(1, bsz * n_ctx, d)
    for l in range(depth):
        last = l == depth - 1
        j = l // n_mixers
        m_lat = mods_all[l, :bsz]
        m_ctx = mods_all[l, bsz:bsz + 1]
        m_ctx_b = jnp.broadcast_to(m_ctx, (bsz, N_MOD, d))

        h_lat = ffn_block(h_lat, m_lat[:, 0:3], norm_g[l, 0], ffn_w_in, ffn_w_out, widx=(l, 0),
                          pos=pos if l == 0 else None)
        h_ctx = ffn_block(h_ctx, m_ctx[:, 0:3], norm_g[l, 0], ffn_w_in, ffn_w_out, widx=(l, 0))

        h_ctx = h_ctx.reshape(bsz, n_ctx, d)
        if l % n_mixers == 0:
            prm = dict(w_in=hy_w_in[j].astype(BF16), b_in=hy_b_in[j], conv_w=hy_conv_w[j], conv_b=hy_conv_b[j],
                       f_w1=hy_f_w1[j], f_b1=hy_f_b1[j], f_w2=hy_f_w2[j], f_b2=hy_f_b2[j], f_w3=hy_f_w3[j],
                       f_b3=hy_f_b3[j], f_freq=hy_f_freq[j], log_decay=hy_log_decay[j], skip=hy_skip[j],
                       w_out=hy_w_out[j].astype(BF16), b_out=hy_b_out[j])
            h_lat = hyena_mixer_block(h_lat, m_lat[:, 3:6], norm_g[l, 1], prm)
            if not last:
                h_ctx = hyena_mixer_block(h_ctx, m_ctx_b[:, 3:6], norm_g[l, 1], prm)
        else:
            prm = dict(w_up=ml_w_up[j].astype(BF16), conv_w=ml_conv_w[j], conv_b=ml_conv_b[j],
                       wq_dense=_dense_blockdiag(ml_w_q[j], ML_QKV_GROUP),
                       wk_dense=_dense_blockdiag(ml_w_k[j], ML_QKV_GROUP),
                       wkt_dense=_dense_blockdiag(jnp.swapaxes(ml_w_k[j], 1, 2), ML_QKV_GROUP),
                       wv_dense=_dense_blockdiag(ml_w_v[j], ML_QKV_GROUP),
                       w_gate=ml_w_gate[j].astype(BF16), b_gate=ml_b_gate[j], norm_g=ml_norm_g[j],
                       skip=ml_skip[j], w_down=ml_w_down[j].astype(BF16))
            new_ctx, h_lat = mlstm_mixer_block(h_ctx, h_lat, m_ctx_b[:, 3:6], m_lat[:, 3:6], norm_g[l, 1], prm,
                                               with_ctx_out=not last)
            h_ctx = h_ctx if last else new_ctx
        h_ctx = h_ctx.reshape(1, bsz * n_ctx, d)

        h_lat = ffn_block(h_lat, m_lat[:, 6:9], norm_g[l, 2], ffn_w_in, ffn_w_out, widx=(l, 1),
                          final_g=final_g if last else None)
        if not last:
            h_ctx = ffn_block(h_ctx, m_ctx[:, 6:9], norm_g[l, 2], ffn_w_in, ffn_w_out, widx=(l, 1))
    return h_lat
```
